```python
import math
import jax, jax.numpy as jnp
from jax import lax
import numpy as np

D_MODEL = 1024
BATCH = 8
SEQ = 4096
DEPTH = 1

CTX_LEN = 256
GRID_W = 64

M_WIDTH = D_MODEL
M_HEADDIM = 64
M_HEADS = M_WIDTH // M_HEADDIM
M_GROUPS = 4
M_HEADS_PER_GROUP = M_HEADS // M_GROUPS
M_STATE = 128
M_XBC = M_WIDTH + 2 * M_GROUPS * M_STATE
M_CONV = 3
M_CHUNK = 128
DT_MIN = 1e-3
DT_MAX = 1e-1

H_WIDTH = D_MODEL
H_ORDER = 2
H_CONV = 3
H_EMB = 33
H_FILTER_HIDDEN = 64
H_DECAY_TARGET = 1e-2
H_FAST_DECAY_PCT = 0.3
H_SLOW_DECAY_PCT = 1.5

NORM_EPS = 1e-6

IN_LAYOUT = (
    ("m_z", M_WIDTH),
    ("m_xbc", M_XBC),
    ("m_dt", 2 * M_HEADS),
    ("h_gate", H_WIDTH),
    ("h_proj", (H_ORDER + 1) * H_WIDTH),
    ("g_m", D_MODEL),
    ("g_h", D_MODEL),
)
ALL_IN_NAMES = ("m_z", "m_xbc", "m_dt", "h_gate", "h_proj", "g_m", "g_h")
IN_TOTAL = M_WIDTH + M_XBC + 2 * M_HEADS + H_WIDTH + (H_ORDER + 1) * H_WIDTH + 2 * D_MODEL

kernel_name = "hybrid_ssd_hyena_gated_prefix_block"


def rmsnorm(x, w):
    x32 = x.astype(jnp.float32)
    y = x32 * lax.rsqrt(jnp.mean(x32 * x32, axis=-1, keepdims=True) + NORM_EPS)
    return (y * w.astype(jnp.float32)).astype(x.dtype)


def depthwise_conv_centred(u, w, b):
    width = w.shape[0]
    y = lax.conv_general_dilated(
        u, w[:, None, :].astype(u.dtype), window_strides=(1,),
        padding=[((width - 1) // 2, width // 2)],
        dimension_numbers=("NWC", "WIO", "NWC"),
        feature_group_count=u.shape[-1])
    return y + b.astype(u.dtype)


def in_projection(h, w_in, names):
    sel, off = [], 0
    for name, width in IN_LAYOUT:
        if name in names:
            sel.append((name, off, width))
        off += width
    w = jnp.concatenate([w_in[:, o:o + wd] for _, o, wd in sel], axis=1)
    y = h @ w
    out, pos = {}, 0
    for name, _, wd in sel:
        out[name] = y[..., pos:pos + wd]
        pos += wd
    return out


def ssd_chunked(xdt, a, bmat, cmat, init_state, with_output):
    bsz, seqlen = xdt.shape[:2]
    nc = seqlen // M_CHUNK
    xdt = xdt.reshape(bsz, nc, M_CHUNK, *xdt.shape[2:])
    a = a.reshape(bsz, nc, M_CHUNK, *a.shape[2:])
    bmat = bmat.reshape(bsz, nc, M_CHUNK, *bmat.shape[2:])
    a_cs = jnp.cumsum(a, axis=2)
    a_tot = a_cs[:, :, -1]
    chunk_states = jnp.einsum("bcsgn,bcsgr,bcsgrp->bcgrpn",
                              bmat, jnp.exp(a_tot[:, :, None] - a_cs), xdt)

    def carry_step(state, inp):
        s_chunk, a_chunk = inp
        return state * jnp.exp(a_chunk)[..., None, None] + s_chunk, state

    final_state, start_states = lax.scan(
        carry_step, init_state,
        (jnp.moveaxis(chunk_states, 1, 0), jnp.moveaxis(a_tot, 1, 0)))
    if not with_output:
        return None, final_state
    cmat = cmat.reshape(bsz, nc, M_CHUNK, *cmat.shape[2:])
    start_states = jnp.moveaxis(start_states, 0, 1)
    idx = jnp.arange(M_CHUNK)
    lower_tri = (idx[:, None] >= idx[None, :])[None, None, :, :, None, None]
    seg = a_cs[:, :, :, None] - a_cs[:, :, None, :]
    decay = jnp.exp(jnp.where(lower_tri, seg, -jnp.inf))
    scores = jnp.einsum("bcqgn,bcsgn->bcqsg", cmat, bmat)
    y_diag = jnp.einsum("bcqsg,bcqsgr,bcsgrp->bcqgrp", scores, decay, xdt)
    y_off = jnp.einsum("bcqgn,bcgrpn,bcqgr->bcqgrp", cmat, start_states, jnp.exp(a_cs))
    y = (y_diag + y_off).reshape(bsz, seqlen, *xdt.shape[3:])
    return y, final_state


def mamba_prep(xbc_raw, dt_raw, conv_w, conv_b, dt_bias, a_log):
    bsz, seqlen = xbc_raw.shape[:2]
    xbc = jax.nn.silu(depthwise_conv_centred(xbc_raw, conv_w, conv_b)).astype(jnp.float32)
    nb = M_GROUPS * M_STATE
    xs = xbc[..., :M_WIDTH].reshape(bsz, seqlen, M_GROUPS, M_HEADS_PER_GROUP, M_HEADDIM)
    bmat = xbc[..., M_WIDTH:M_WIDTH + nb].reshape(bsz, seqlen, M_GROUPS, M_STATE)
    cmat = xbc[..., M_WIDTH + nb:].reshape(bsz, seqlen, M_GROUPS, M_STATE)
    dt = jax.nn.softplus(
        dt_raw.astype(jnp.float32).reshape(bsz, seqlen, 2, M_GROUPS, M_HEADS_PER_GROUP)
        + dt_bias.astype(jnp.float32).reshape(2, M_GROUPS, M_HEADS_PER_GROUP))
    a_cont = -jnp.exp(a_log.astype(jnp.float32)).reshape(2, M_GROUPS, M_HEADS_PER_GROUP)
    return xs, bmat, cmat, dt, dt * a_cont


def mamba_mixer(xbc_l, dt_l, xbc_c, dt_c, conv_w, conv_b, dt_bias, a_log, d_skip, with_ctx_output):
    xs_l, b_l, c_l, dtd_l, a_l = mamba_prep(xbc_l, dt_l, conv_w, conv_b, dt_bias, a_log)
    xs_c, b_c, c_c, dtd_c, a_c = mamba_prep(xbc_c, dt_c, conv_w, conv_b, dt_bias, a_log)
    bsz, seqlen = xs_l.shape[:2]
    d_skip = d_skip.astype(jnp.float32).reshape(M_GROUPS, M_HEADS_PER_GROUP, 1)
    zero_state = jnp.zeros((bsz, M_GROUPS, M_HEADS_PER_GROUP, M_HEADDIM, M_STATE), jnp.float32)
    y_l = d_skip * xs_l
    y_c = d_skip * xs_c if with_ctx_output else None
    for direction in range(2):
        order = (lambda t: t) if direction == 0 else (lambda t: jnp.flip(t, axis=1))
        yc_dir, s_ctx = ssd_chunked(
            order(xs_c * dtd_c[:, :, direction, ..., None]), order(a_c[:, :, direction]),
            order(b_c), order(c_c), zero_state, with_ctx_output)
        yl_dir, _ = ssd_chunked(
            order(xs_l * dtd_l[:, :, direction, ..., None]), order(a_l[:, :, direction]),
            order(b_l), order(c_l), s_ctx, True)
        y_l = y_l + order(yl_dir)
        if with_ctx_output:
            y_c = y_c + order(yc_dir)
    y_l = y_l.reshape(bsz, seqlen, M_WIDTH)
    if with_ctx_output:
        y_c = y_c.reshape(bsz, xs_c.shape[1], M_WIDTH)
    return y_l, y_c


def gated_group_rmsnorm(y, z, w):
    g = y * jax.nn.silu(z.astype(jnp.float32))
    shape = g.shape
    g = g.reshape(*shape[:-1], M_GROUPS, M_WIDTH // M_GROUPS)
    g = g * lax.rsqrt(jnp.mean(g * g, axis=-1, keepdims=True) + NORM_EPS)
    return (g.reshape(shape) * w.astype(jnp.float32)).astype(z.dtype)


def hyena_filters(seqlen, w1, b1, w2, b2, w3, b3, freq, w_out):
    t = jnp.linspace(0.0, 1.0, seqlen, dtype=jnp.float32)[:, None]
    bands = (H_EMB - 1) // 2
    f = jnp.linspace(1e-4, bands - 1, bands, dtype=jnp.float32)[None, :]
    w = (2.0 * math.pi / seqlen) * jnp.arange(seqlen, dtype=jnp.float32)[:, None]
    feats = jnp.concatenate([t, jnp.cos(f * w), -jnp.sin(f * w)], axis=-1)
    freq = freq.astype(jnp.float32)
    hid = jnp.sin(freq * (feats @ w1.astype(jnp.float32) + b1.astype(jnp.float32)))
    hid = jnp.sin(freq * (hid @ w2.astype(jnp.float32) + b2.astype(jnp.float32)))
    hid = jnp.sin(freq * (hid @ w3.astype(jnp.float32) + b3.astype(jnp.float32)))
    filt = (hid @ w_out.astype(jnp.float32)).reshape(seqlen, 2, H_ORDER, H_WIDTH)
    min_decay = math.log(H_DECAY_TARGET) / H_SLOW_DECAY_PCT
    max_decay = math.log(H_DECAY_TARGET) / H_FAST_DECAY_PCT
    deltas = jnp.abs(jnp.linspace(min_decay, max_decay, H_WIDTH, dtype=jnp.float32))
    window = jnp.exp(-t[:, :, None, None] * deltas)
    return filt * window


def bidir_long_conv(u, h_fwd, h_bwd):
    seqlen = u.shape[1]
    k = jnp.concatenate([h_fwd[:1] + h_bwd[:1], h_fwd[1:], jnp.zeros_like(h_fwd[:1]),
                         jnp.flip(h_bwd[1:], axis=0)], axis=0)
    u_f = jnp.fft.rfft(u, n=2 * seqlen, axis=1)
    k_f = jnp.fft.rfft(k, n=2 * seqlen, axis=0)
    return jnp.fft.irfft(u_f * k_f[None], n=2 * seqlen, axis=1)[:, :seqlen]


def hyena_mixer(proj, conv_w, conv_b, w1, b1, w2, b2, w3, b3, freq, w_out, h_bias):
    seqlen = proj.shape[1]
    u = depthwise_conv_centred(proj, conv_w, conv_b).astype(jnp.float32)
    v, x1, x2 = jnp.split(u, 3, axis=-1)
    filt = hyena_filters(seqlen, w1, b1, w2, b2, w3, b3, freq, w_out)
    h_bias = h_bias.astype(jnp.float32)
    z = v
    for o, gate in enumerate((x1, x2)):
        z = gate * (bidir_long_conv(z, filt[:, 0, o], filt[:, 1, o]) + h_bias[o] * z)
    return z


def mixer_output(cols, y_m, y_h, m_norm_w, w_branch_m, w_branch_h, w_out):
    dtype = cols["m_z"].dtype
    ym = gated_group_rmsnorm(y_m, cols["m_z"], m_norm_w)
    yh = (y_h * jax.nn.silu(cols["h_gate"].astype(jnp.float32))).astype(dtype)
    merged = (jax.nn.sigmoid(cols["g_m"]) * (ym @ w_branch_m)
              + jax.nn.sigmoid(cols["g_h"]) * (yh @ w_branch_h))
    return merged @ w_out


def hybrid_layer(x, ctx, c, c_ctx, ada_w, ada_b, norm_w, w_in, m_conv_w, m_conv_b, m_dt_bias,
                 m_a_log, m_d, m_norm_w, h_conv_w, h_conv_b, h_w1, h_b1, h_w2, h_b2, h_w3, h_b3,
                 h_freq, h_w_out, h_bias, w_branch_m, w_branch_h, w_out, update_ctx):
    shift, scale, gate = jnp.split(jax.nn.silu(c) @ ada_w + ada_b, 3, axis=-1)
    shift_c, scale_c, gate_c = jnp.split(jax.nn.silu(c_ctx) @ ada_w + ada_b, 3, axis=-1)
    h = rmsnorm(x, norm_w) * (1.0 + scale[:, None]) + shift[:, None]
    hc = rmsnorm(ctx, norm_w) * (1.0 + scale_c) + shift_c
    cols = in_projection(h, w_in, ALL_IN_NAMES)
    cols_c = in_projection(hc, w_in, ALL_IN_NAMES if update_ctx else ("m_xbc", "m_dt"))
    y_m, y_m_c = mamba_mixer(cols["m_xbc"], cols["m_dt"], cols_c["m_xbc"], cols_c["m_dt"],
                             m_conv_w, m_conv_b, m_dt_bias, m_a_log, m_d, update_ctx)
    y_h = hyena_mixer(cols["h_proj"], h_conv_w, h_conv_b, h_w1, h_b1, h_w2, h_b2, h_w3, h_b3,
                      h_freq, h_w_out, h_bias)
    x_new = x + gate[:, None] * mixer_output(cols, y_m, y_h, m_norm_w, w_branch_m, w_branch_h, w_out)
    if update_ctx:
        y_h_c = hyena_mixer(cols_c["h_proj"], h_conv_w, h_conv_b, h_w1, h_b1, h_w2, h_b2, h_w3,
                            h_b3, h_freq, h_w_out, h_bias)
        ctx = ctx + gate_c * mixer_output(cols_c, y_m_c, y_h_c, m_norm_w, w_branch_m, w_branch_h, w_out)
    return x_new, ctx


def setup_inputs(seed: int = 0) -> dict:
    key = jax.random.key(seed)
    ks = iter(jax.random.split(key, 40))

    def nrm(shape, scale):
        return jax.random.normal(next(ks), shape, jnp.float32) * scale

    x = nrm((BATCH, SEQ, D_MODEL), 1.0)
    c = nrm((BATCH, D_MODEL), 1.0)
    ctx = nrm((BATCH, CTX_LEN, D_MODEL), 1.0)
    c_ctx = nrm((D_MODEL,), 1.0)
    ada_w = nrm((DEPTH, D_MODEL, 3 * D_MODEL), 0.5 * D_MODEL ** -0.5)
    ada_b = nrm((DEPTH, 3 * D_MODEL), 0.02)
    norm_w = 1.0 + nrm((DEPTH, D_MODEL), 0.02)
    w_in = nrm((DEPTH, D_MODEL, IN_TOTAL), D_MODEL ** -0.5)
    m_conv_w = nrm((DEPTH, M_CONV, M_XBC), M_CONV ** -0.5)
    m_conv_b = nrm((DEPTH, M_XBC), 0.02)
    dt0 = jnp.exp(jax.random.uniform(next(ks), (DEPTH, 2, M_HEADS), jnp.float32,
                                     minval=math.log(DT_MIN), maxval=math.log(DT_MAX)))
    m_dt_bias = dt0 + jnp.log(-jnp.expm1(-dt0))
    m_a_log = jnp.log(jax.random.uniform(next(ks), (DEPTH, 2, M_HEADS), jnp.float32,
                                         minval=1.0, maxval=16.0))
    m_d = 1.0 + nrm((DEPTH, M_HEADS), 0.02)
    m_norm_w = 1.0 + nrm((DEPTH, M_WIDTH), 0.02)
    h_conv_w = nrm((DEPTH, H_CONV, (H_ORDER + 1) * H_WIDTH), H_CONV ** -0.5)
    h_conv_b = nrm((DEPTH, (H_ORDER + 1) * H_WIDTH), 0.02)
    h_w1 = nrm((DEPTH, H_EMB, H_FILTER_HIDDEN), H_EMB ** -0.5)
    h_b1 = nrm((DEPTH, H_FILTER_HIDDEN), 0.02)
    h_w2 = nrm((DEPTH, H_FILTER_HIDDEN, H_FILTER_HIDDEN), H_FILTER_HIDDEN ** -0.5)
    h_b2 = nrm((DEPTH, H_FILTER_HIDDEN), 0.02)
    h_w3 = nrm((DEPTH, H_FILTER_HIDDEN, H_FILTER_HIDDEN), H_FILTER_HIDDEN ** -0.5)
    h_b3 = nrm((DEPTH, H_FILTER_HIDDEN), 0.02)
    h_freq = 1.0 + nrm((DEPTH, H_FILTER_HIDDEN), 0.02)
    h_w_out = nrm((DEPTH, H_FILTER_HIDDEN, 2 * H_ORDER * H_WIDTH), 0.02 * H_FILTER_HIDDEN ** -0.5)
    h_bias = nrm((DEPTH, H_ORDER, H_WIDTH), 1.0)
    w_branch_m = nrm((DEPTH, M_WIDTH, D_MODEL), M_WIDTH ** -0.5)
    w_branch_h = nrm((DEPTH, H_WIDTH, D_MODEL), H_WIDTH ** -0.5)
    w_out = nrm((DEPTH, D_MODEL, D_MODEL), D_MODEL ** -0.5)
    final_norm_w = 1.0 + nrm((D_MODEL,), 0.02)
    return {"x": x, "c": c, "ctx": ctx, "c_ctx": c_ctx, "ada_w": ada_w, "ada_b": ada_b,
            "norm_w": norm_w, "w_in": w_in, "m_conv_w": m_conv_w, "m_conv_b": m_conv_b,
            "m_dt_bias": m_dt_bias, "m_a_log": m_a_log, "m_d": m_d, "m_norm_w": m_norm_w,
            "h_conv_w": h_conv_w, "h_conv_b": h_conv_b, "h_w1": h_w1, "h_b1": h_b1,
            "h_w2": h_w2, "h_b2": h_b2, "h_w3": h_w3, "h_b3": h_b3, "h_freq": h_freq,
            "h_w_out": h_w_out, "h_bias": h_bias, "w_branch_m": w_branch_m,
            "w_branch_h": w_branch_h, "w_out": w_out, "final_norm_w": final_norm_w}


def reference(x, c, ctx, c_ctx, ada_w, ada_b, norm_w, w_in, m_conv_w, m_conv_b, m_dt_bias,
              m_a_log, m_d, m_norm_w, h_conv_w, h_conv_b, h_w1, h_b1, h_w2, h_b2, h_w3, h_b3,
              h_freq, h_w_out, h_bias, w_branch_m, w_branch_h, w_out, final_norm_w):
    for li in range(DEPTH):
        x, ctx = hybrid_layer(
            x, ctx, c, c_ctx, ada_w[li], ada_b[li], norm_w[li], w_in[li], m_conv_w[li],
            m_conv_b[li], m_dt_bias[li], m_a_log[li], m_d[li], m_norm_w[li], h_conv_w[li],
            h_conv_b[li], h_w1[li], h_b1[li], h_w2[li], h_b2[li], h_w3[li], h_b3[li],
            h_freq[li], h_w_out[li], h_bias[li], w_branch_m[li], w_branch_h[li], w_out[li],
            update_ctx=li + 1 < DEPTH)
    return rmsnorm(x, final_norm_w)
```

```python
import functools
import math

import jax
import jax.numpy as jnp
from jax import lax
from jax.experimental import pallas as pl
from jax.experimental.pallas import tpu as pltpu

F32 = jnp.float32
BF16 = jnp.bfloat16
HIGHEST = lax.Precision.HIGHEST

LANES = 128
SUBLANES = 8
VMEM_LIMIT = 56 * 1024 * 1024

NORM_EPS = 1e-6
D_MODEL = 1024
M_HEADDIM = 64
M_HEADS = 16
M_GROUPS = 4
M_HPG = M_HEADS // M_GROUPS
M_STATE = 128
M_XBC = D_MODEL + 2 * M_GROUPS * M_STATE
M_CHUNK = 128
HEADDIM_SHIFT = M_HEADDIM.bit_length() - 1
CHUNK_SHIFT = M_CHUNK.bit_length() - 1
H_WIDTH = 1024
H_EMB = 33
H_HID = 64
H_DECAY_TARGET = 1e-2
H_FAST_DECAY_PCT = 0.3
H_SLOW_DECAY_PCT = 1.5
DT_PAD = LANES
DFT_SPLIT = 64

COL_XBC, COL_Z, COL_HGATE, COL_HPROJ, COL_GM, COL_GH = 0, 2, 3, 4, 7, 8
N_MAIN = 9 * 1024


def _dot(a, b, precision=None):
    return jnp.dot(a, b, preferred_element_type=F32, precision=precision)


def _dot_nt(a, b, precision=None):
    return lax.dot_general(a, b, (((1,), (1,)), ((), ())), preferred_element_type=F32,
                           precision=precision)


def _dot_tn(a, b, precision=None):
    return lax.dot_general(a, b, (((0,), (0,)), ((), ())), preferred_element_type=F32,
                           precision=precision)


def _params(*sem):
    return pltpu.CompilerParams(dimension_semantics=sem, vmem_limit_bytes=VMEM_LIMIT)


def _silu(v):
    return v * jax.nn.sigmoid(v)


def _adaln_kernel(c_ref, w_ref, b_ref, o_ref):
    o_ref[...] = _dot(_silu(c_ref[...]), w_ref[...], HIGHEST) + b_ref[...]


def _adaln(c_rows, ada_w, ada_b):
    rows, d = c_rows.shape
    n = ada_w.shape[1]
    tn = 1024
    return pl.pallas_call(
        _adaln_kernel,
        grid=(n // tn,),
        in_specs=[pl.BlockSpec((rows, d), lambda j: (0, 0)),
                  pl.BlockSpec((d, tn), lambda j: (0, j)),
                  pl.BlockSpec((1, tn), lambda j: (0, j))],
        out_specs=pl.BlockSpec((rows, tn), lambda j: (0, j)),
        out_shape=jax.ShapeDtypeStruct((rows, n), F32),
        compiler_params=_params("parallel"),
        name="adaln",
    )(c_rows, ada_w, ada_b.reshape(1, n))


def _inproj_kernel(x_ref, shift_ref, scale_ref, nw_ref, w_ref, wdt_ref, wdtt_ref,
                   o_ref, dt_ref, dtt_ref, h_scr):
    @pl.when(pl.program_id(2) == 0)
    def _():
        xt = x_ref[0]
        ms = jnp.mean(xt * xt, axis=-1, keepdims=True)
        hn = xt * lax.rsqrt(ms + NORM_EPS) * nw_ref[...]
        h = hn * (1.0 + scale_ref[0]) + shift_ref[0]
        h_scr[...] = h.astype(BF16)
        dt_ref[0] = _dot(h, wdt_ref[...], HIGHEST)
        dtt_ref[0] = _dot_nt(wdtt_ref[...], h, HIGHEST)

    o_ref[0] = _dot(h_scr[...], w_ref[...])


def _inproj(x, shift, scale, norm_w, w_main, w_dt, w_dt_t, tm):
    b, l, d = x.shape
    n = w_main.shape[1]
    tn = 1024
    ndt = w_dt_t.shape[0]
    return pl.pallas_call(
        _inproj_kernel,
        grid=(b, l // tm, n // tn),
        in_specs=[pl.BlockSpec((1, tm, d), lambda bi, i, j: (bi, i, 0)),
                  pl.BlockSpec((1, 1, d), lambda bi, i, j: (bi, 0, 0)),
                  pl.BlockSpec((1, 1, d), lambda bi, i, j: (bi, 0, 0)),
                  pl.BlockSpec((1, d), lambda bi, i, j: (0, 0)),
                  pl.BlockSpec((d, tn), lambda bi, i, j: (0, j)),
                  pl.BlockSpec((d, DT_PAD), lambda bi, i, j: (0, 0)),
                  pl.BlockSpec((ndt, d), lambda bi, i, j: (0, 0))],
        out_specs=[pl.BlockSpec((1, tm, tn), lambda bi, i, j: (bi, i, j)),
                   pl.BlockSpec((1, tm, DT_PAD), lambda bi, i, j: (bi, i, 0)),
                   pl.BlockSpec((1, ndt, tm), lambda bi, i, j: (bi, 0, i))],
        out_shape=[jax.ShapeDtypeStruct((b, l, n), F32),
                   jax.ShapeDtypeStruct((b, l, DT_PAD), F32),
                   jax.ShapeDtypeStruct((b, ndt, l), F32)],
        scratch_shapes=[pltpu.VMEM((tm, d), BF16)],
        compiler_params=_params("parallel", "parallel", "arbitrary"),
        name="inproj",
    )(x, shift, scale, norm_w, w_main, w_dt, w_dt_t)


def _dwconv_kernel(cur_ref, prev_ref, next_ref, w_ref, b_ref, o_ref, *, apply_silu):
    r = pl.program_id(1)
    nr = pl.num_programs(1)
    u = cur_ref[0]
    rows = u.shape[0]
    prev_row = jnp.where(r > 0, prev_ref[0][SUBLANES - 1:SUBLANES, :], 0.0)
    next_row = jnp.where(r < nr - 1, next_ref[0][0:1, :], 0.0)
    ridx = lax.broadcasted_iota(jnp.int32, (rows, 1), 0)
    up = jnp.where(ridx == 0, prev_row, pltpu.roll(u, 1, 0))
    un = jnp.where(ridx == rows - 1, next_row, pltpu.roll(u, rows - 1, 0))
    w = w_ref[...]
    y = w[0:1, :] * up + w[1:2, :] * u + w[2:3, :] * un + b_ref[...]
    if apply_silu:
        y = _silu(y)
    o_ref[0] = y


def _dwconv(src, col_off_blocks, conv_w, conv_b, apply_silu, name):
    b, l, _ = src.shape
    c = conv_w.shape[1]
    cb = 512
    rr = min(512, l)
    nsub = rr // SUBLANES
    last = l // SUBLANES - 1
    off = col_off_blocks * (1024 // cb)
    return pl.pallas_call(
        functools.partial(_dwconv_kernel, apply_silu=apply_silu),
        grid=(b, l // rr, c // cb),
        in_specs=[pl.BlockSpec((1, rr, cb), lambda bi, r, ci: (bi, r, off + ci)),
                  pl.BlockSpec((1, SUBLANES, cb),
                               lambda bi, r, ci: (bi, jnp.maximum(r * nsub - 1, 0), off + ci)),
                  pl.BlockSpec((1, SUBLANES, cb),
                               lambda bi, r, ci: (bi, jnp.minimum((r + 1) * nsub, last), off + ci)),
                  pl.BlockSpec((3, cb), lambda bi, r, ci: (0, ci)),
                  pl.BlockSpec((1, cb), lambda bi, r, ci: (0, ci))],
        out_specs=pl.BlockSpec((1, rr, cb), lambda bi, r, ci: (bi, r, ci)),
        out_shape=jax.ShapeDtypeStruct((b, l, c), F32),
        compiler_params=_params("parallel", "parallel", "parallel"),
        name=name,
    )(src, src, src, conv_w, conv_b.reshape(1, c))


def _softplus(v):
    return jnp.maximum(v, 0.0) + jnp.log1p(jnp.exp(-jnp.abs(v)))


def _expand_heads(v, e_bf16):
    hi = v.astype(BF16)
    lo = (v - hi.astype(F32)).astype(BF16)
    return _dot(hi, e_bf16) + _dot(lo, e_bf16)


def _ssd_kernel(xl_ref, xc_ref, dtl_ref, dtc_ref, dtlt_ref, dtct_ref, bias_r_ref, a_r_ref,
                bias_c_ref, a_c_ref, dskip_ref, y_ref, state, *, n_ctx_chunks):
    d = pl.program_id(1)
    s = pl.program_id(2)
    is_ctx = s < n_ctx_chunks
    fwd = d == 0
    q = M_CHUNK

    @pl.when(s == 0)
    def _():
        state[...] = jnp.zeros_like(state)

    xbc = jnp.where(is_ctx, xc_ref[0], xl_ref[0])
    x = xbc[:, :D_MODEL]
    bmat = xbc[:, D_MODEL:D_MODEL + M_GROUPS * M_STATE].astype(BF16)
    cmat = xbc[:, D_MODEL + M_GROUPS * M_STATE:].astype(BF16)

    dt_raw = jnp.where(is_ctx, dtc_ref[0], dtl_ref[0])
    dt_all = _softplus(dt_raw + bias_r_ref[...])
    a_all = dt_all * a_r_ref[...]
    dt16 = jnp.where(fwd, dt_all[:, 0:M_HEADS], dt_all[:, M_HEADS:2 * M_HEADS])
    a16 = jnp.where(fwd, a_all[:, 0:M_HEADS], a_all[:, M_HEADS:2 * M_HEADS])
    dt_raw_t = jnp.where(is_ctx, dtct_ref[0], dtlt_ref[0])
    a_all_t = _softplus(dt_raw_t + bias_c_ref[...]) * a_c_ref[...]
    a16_t = jnp.where(fwd, a_all_t[0:M_HEADS, :], a_all_t[M_HEADS:2 * M_HEADS, :])

    ri = lax.broadcasted_iota(jnp.int32, (q, q), 0)
    ci = lax.broadcasted_iota(jnp.int32, (q, q), 1)
    order = jnp.where(fwd, ri - ci, ci - ri)
    mask = order >= 0
    tri = jnp.where(mask, 1.0, 0.0)
    tri_t = jnp.where(order <= 0, 1.0, 0.0)
    acs = _dot(tri, a16, HIGHEST)
    acs_t = _dot(a16_t, tri_t, HIGHEST)
    atot_r = jnp.where(fwd, acs[q - 1:q, :], acs[0:1, :])
    atot_c = jnp.where(fwd, acs_t[:, q - 1:q], acs_t[:, 0:1])
    e_atot_c = jnp.exp(atot_c)

    hid = lax.broadcasted_iota(jnp.int32, (M_HEADS, D_MODEL), 0)
    lid = lax.broadcasted_iota(jnp.int32, (M_HEADS, D_MODEL), 1)
    expand = jnp.where((lid >> HEADDIM_SHIFT) == hid, 1.0, 0.0).astype(BF16)
    xdt = x * _expand_heads(dt16, expand)
    xw = (xdt * _expand_heads(jnp.exp(atot_r - acs), expand)).astype(BF16)

    gw = M_HPG * M_HEADDIM

    @pl.when(jnp.logical_not(is_ctx))
    def _():
        e_acs = _expand_heads(jnp.exp(acs), expand)
        rb = lax.broadcasted_iota(jnp.int32, (M_HPG * q, gw), 0) >> CHUNK_SHIFT
        cb = lax.broadcasted_iota(jnp.int32, (M_HPG * q, gw), 1) >> HEADDIM_SHIFT
        blockdiag = rb == cb
        for g in range(M_GROUPS):
            bg = bmat[:, g * M_STATE:(g + 1) * M_STATE]
            cg = cmat[:, g * M_STATE:(g + 1) * M_STATE]
            scores = _dot_nt(cg, bg)
            ms = []
            for r in range(M_HPG):
                h = g * M_HPG + r
                seg = acs[:, h:h + 1] - acs_t[h:h + 1, :]
                decay = jnp.exp(jnp.where(mask, seg, -jnp.inf))
                ms.append((scores * decay).astype(BF16))
            m_cat = jnp.concatenate(ms, axis=1)
            xg = xdt[:, g * gw:(g + 1) * gw]
            x_bd = jnp.where(blockdiag, jnp.concatenate([xg] * M_HPG, axis=0),
                             0.0).astype(BF16)
            y_diag = _dot(m_cat, x_bd)
            s_g = state[g * M_HPG:(g + 1) * M_HPG].reshape(gw, M_STATE).astype(BF16)
            y_off = _dot_nt(cg, s_g)
            y_g = y_diag + e_acs[:, g * gw:(g + 1) * gw] * y_off
            skip = jnp.where(fwd, dskip_ref[:, g * gw:(g + 1) * gw], 0.0)
            y_ref[0, 0, :, g * gw:(g + 1) * gw] = y_g + skip * x[:, g * gw:(g + 1) * gw]

    for g in range(M_GROUPS):
        bg = bmat[:, g * M_STATE:(g + 1) * M_STATE]
        upd = _dot_tn(xw[:, g * gw:(g + 1) * gw], bg)
        for r in range(M_HPG):
            h = g * M_HPG + r
            state[h] = (state[h] * e_atot_c[h:h + 1, 0:1]
                        + upd[r * M_HEADDIM:(r + 1) * M_HEADDIM, :])


def _ssd(xbc_l, xbc_c, dt_l, dt_c, dtt_l, dtt_c, bias_r, a_r, bias_c, a_c, dskip_row):
    b, l, cw = xbc_l.shape
    lc = xbc_c.shape[1]
    q = M_CHUNK
    nl, nc = l // q, lc // q
    ndt = dtt_l.shape[1]

    def lat(di, si):
        f = jnp.maximum(si - nc, 0)
        return jnp.where(di == 0, f, nl - 1 - f)

    def ctx(di, si):
        g = jnp.minimum(si, nc - 1)
        return jnp.where(di == 0, g, nc - 1 - g)

    small = lambda shape: pl.BlockSpec(shape, lambda bi, di, si: (0, 0))
    return pl.pallas_call(
        functools.partial(_ssd_kernel, n_ctx_chunks=nc),
        grid=(b, 2, nl + nc),
        in_specs=[pl.BlockSpec((1, q, cw), lambda bi, di, si: (bi, lat(di, si), 0)),
                  pl.BlockSpec((1, q, cw), lambda bi, di, si: (bi, ctx(di, si), 0)),
                  pl.BlockSpec((1, q, DT_PAD), lambda bi, di, si: (bi, lat(di, si), 0)),
                  pl.BlockSpec((1, q, DT_PAD), lambda bi, di, si: (bi, ctx(di, si), 0)),
                  pl.BlockSpec((1, ndt, q), lambda bi, di, si: (bi, 0, lat(di, si))),
                  pl.BlockSpec((1, ndt, q), lambda bi, di, si: (bi, 0, ctx(di, si))),
                  small((1, DT_PAD)), small((1, DT_PAD)), small((ndt, 1)), small((ndt, 1)),
                  small((1, D_MODEL))],
        out_specs=pl.BlockSpec((1, 1, q, D_MODEL), lambda bi, di, si: (di, bi, lat(di, si), 0)),
        out_shape=jax.ShapeDtypeStruct((2, b, l, D_MODEL), F32),
        scratch_shapes=[pltpu.VMEM((M_HEADS, M_HEADDIM, M_STATE), F32)],
        compiler_params=_params("parallel", "arbitrary", "arbitrary"),
        name="ssd",
    )(xbc_l, xbc_c, dt_l, dt_c, dtt_l, dtt_c, bias_r, a_r, bias_c, a_c, dskip_row)


def _filter_kernel(fl_ref, w1_ref, b1_ref, w2_ref, b2_ref, w3_ref, b3_ref, freq_ref, wo_ref,
                   deltas_ref, hsum_ref, hdiff_ref, nyq_ref, *, seqlen):
    i = pl.program_id(0)
    tr = hsum_ref.shape[0]
    pos = (i * tr + lax.broadcasted_iota(jnp.int32, (tr, 1), 0))
    posf = pos.astype(F32)
    t = posf / F32(seqlen - 1)
    w = F32(2.0 * math.pi / seqlen) * posf
    ang = fl_ref[...] * w
    lane = lax.broadcasted_iota(jnp.int32, (tr, LANES), 1)
    bands = (H_EMB - 1) // 2
    feats = jnp.where(lane == 0, t,
                      jnp.where(lane <= bands, jnp.cos(ang),
                                jnp.where(lane < H_EMB, -jnp.sin(ang), 0.0)))
    freq = freq_ref[...]
    hid = jnp.sin(freq * (_dot(feats, w1_ref[...], HIGHEST) + b1_ref[...]))
    hid = jnp.sin(freq * (_dot(hid, w2_ref[...], HIGHEST) + b2_ref[...]))
    hid = jnp.sin(freq * (_dot(hid, w3_ref[...], HIGHEST) + b3_ref[...]))
    filt = _dot(hid, wo_ref[...], HIGHEST)
    window = jnp.exp(-t * deltas_ref[...])
    sign = (1 - 2 * (pos & 1)).astype(F32)

    @pl.when(i == 0)
    def _():
        nyq_ref[...] = jnp.zeros_like(nyq_ref)

    hw = H_WIDTH
    for o in range(2):
        hf = filt[:, o * hw:(o + 1) * hw] * window
        hb = filt[:, (2 + o) * hw:(3 + o) * hw] * window
        hs = hf + hb
        hsum_ref[:, o * hw:(o + 1) * hw] = hs.astype(BF16)
        hdiff_ref[:, o * hw:(o + 1) * hw] = (hf - hb).astype(BF16)
        nyq_ref[:, o * hw:(o + 1) * hw] += jnp.sum(hs * sign, axis=0, keepdims=True)


def _hyena_filters(seqlen, w1, b1, w2, b2, w3, b3, freq, w_out):
    tr = min(512, seqlen)
    bands = (H_EMB - 1) // 2
    f = jnp.linspace(1e-4, bands - 1, bands, dtype=F32)
    fl = jnp.zeros((1, LANES), F32).at[0, 1:1 + bands].set(f).at[0, 1 + bands:H_EMB].set(f)
    w1p = jnp.zeros((LANES, H_HID), F32).at[:H_EMB].set(w1)
    min_decay = math.log(H_DECAY_TARGET) / H_SLOW_DECAY_PCT
    max_decay = math.log(H_DECAY_TARGET) / H_FAST_DECAY_PCT
    deltas = jnp.abs(jnp.linspace(min_decay, max_decay, H_WIDTH, dtype=F32)).reshape(1, H_WIDTH)
    row = lambda v: v.reshape(1, -1)
    full = lambda a: pl.BlockSpec(a.shape, lambda i: (0, 0))
    args = (fl, w1p, row(b1), w2, row(b2), w3, row(b3), row(freq), w_out, deltas)
    nout = 2 * H_WIDTH
    return pl.pallas_call(
        functools.partial(_filter_kernel, seqlen=seqlen),
        grid=(seqlen // tr,),
        in_specs=[full(a) for a in args],
        out_specs=[pl.BlockSpec((tr, nout), lambda i: (i, 0)),
                   pl.BlockSpec((tr, nout), lambda i: (i, 0)),
                   pl.BlockSpec((1, nout), lambda i: (0, 0))],
        out_shape=[jax.ShapeDtypeStruct((seqlen, nout), BF16),
                   jax.ShapeDtypeStruct((seqlen, nout), BF16),
                   jax.ShapeDtypeStruct((1, nout), F32)],
        compiler_params=_params("arbitrary"),
        name="hyena_filters",
    )(*args)


def _trig_seed_kernel(c_ref, s_ref, *, seqlen):
    rows, cols = c_ref.shape
    r = lax.broadcasted_iota(jnp.int32, (rows, cols), 0)
    t = lax.broadcasted_iota(jnp.int32, (rows, cols), 1)
    mult = jnp.where(r < DFT_SPLIT, r, (r - DFT_SPLIT) * DFT_SPLIT)
    prod = (mult * t) & (2 * seqlen - 1)
    ang = prod.astype(F32) * F32(math.pi / seqlen)
    c_ref[...] = jnp.cos(ang)
    s_ref[...] = jnp.sin(ang)


def _dft_table_kernel(hc_ref, hs_ref, lc_ref, ls_ref, fc_ref, fs_ref, fst_ref):
    i = pl.program_id(0)
    hc, hs = hc_ref[0], hs_ref[0]
    lc, ls = lc_ref[...], ls_ref[...]
    rows, cols = lc.shape
    fc_ref[...] = (hc * lc - hs * ls).astype(BF16)
    ssym = hs * lc + hc * ls
    rid = lax.broadcasted_iota(jnp.int32, (rows, cols), 0) + i * rows
    cid = lax.broadcasted_iota(jnp.int32, (rows, cols), 1)
    fs_ref[...] = jnp.where(rid == 0, (1 - 2 * (cid & 1)).astype(F32), ssym).astype(BF16)
    fst_ref[...] = jnp.where(cid == 0, (1 - 2 * (rid & 1)).astype(F32), ssym).astype(BF16)


def _dft_tables(seqlen):
    nhi = seqlen // DFT_SPLIT
    rows = DFT_SPLIT + nhi
    seed_c, seed_s = pl.pallas_call(
        functools.partial(_trig_seed_kernel, seqlen=seqlen),
        out_shape=[jax.ShapeDtypeStruct((rows, seqlen), F32)] * 2,
        compiler_params=pltpu.CompilerParams(vmem_limit_bytes=VMEM_LIMIT),
        name="dft_seed",
    )()
    lc, ls = seed_c[:DFT_SPLIT], seed_s[:DFT_SPLIT]
    hc = seed_c[DFT_SPLIT:].reshape(nhi, 1, seqlen)
    hs = seed_s[DFT_SPLIT:].reshape(nhi, 1, seqlen)
    blk = pl.BlockSpec((DFT_SPLIT, seqlen), lambda i: (i, 0))
    return pl.pallas_call(
        _dft_table_kernel,
        grid=(nhi,),
        in_specs=[pl.BlockSpec((1, 1, seqlen), lambda i: (i, 0, 0)),
                  pl.BlockSpec((1, 1, seqlen), lambda i: (i, 0, 0)),
                  pl.BlockSpec((DFT_SPLIT, seqlen), lambda i: (0, 0)),
                  pl.BlockSpec((DFT_SPLIT, seqlen), lambda i: (0, 0))],
        out_specs=[blk, blk, blk],
        out_shape=[jax.ShapeDtypeStruct((seqlen, seqlen), BF16)] * 3,
        compiler_params=_params("parallel"),
        name="dft_tables",
    )(hc, hs, lc, ls)


def _spectrum_kernel(fc_ref, fs_ref, hsum_ref, hdiff_ref, nyq_ref, ka_ref, kb_ref,
                     acc_a, acc_b, *, seqlen):
    m = pl.program_id(1)
    k = pl.program_id(2)

    @pl.when(k == 0)
    def _():
        acc_a[...] = jnp.zeros_like(acc_a)
        acc_b[...] = jnp.zeros_like(acc_b)

    acc_a[...] += _dot(fc_ref[...], hsum_ref[...])
    acc_b[...] += _dot(fs_ref[...], hdiff_ref[...])

    @pl.when(k == pl.num_programs(2) - 1)
    def _():
        tm = acc_a.shape[0]
        rid = lax.broadcasted_iota(jnp.int32, (tm, 1), 0) + m * tm
        scale = jnp.where(rid == 0, F32(0.5 / seqlen), F32(1.0 / seqlen))
        ka_ref[...] = acc_a[...] * scale
        kb_ref[...] = jnp.where(rid == 0, nyq_ref[...], acc_b[...]) * scale


def _filter_spectrum(fc, fs, hsum, hdiff, nyq):
    seqlen, n = hsum.shape
    tm, tk, tn = min(512, seqlen), min(1024, seqlen), 1024
    return pl.pallas_call(
        functools.partial(_spectrum_kernel, seqlen=seqlen),
        grid=(n // tn, seqlen // tm, seqlen // tk),
        in_specs=[pl.BlockSpec((tm, tk), lambda j, m, k: (m, k)),
                  pl.BlockSpec((tm, tk), lambda j, m, k: (m, k)),
                  pl.BlockSpec((tk, tn), lambda j, m, k: (k, j)),
                  pl.BlockSpec((tk, tn), lambda j, m, k: (k, j)),
                  pl.BlockSpec((1, tn), lambda j, m, k: (0, j))],
        out_specs=[pl.BlockSpec((tm, tn), lambda j, m, k: (m, j)),
                   pl.BlockSpec((tm, tn), lambda j, m, k: (m, j))],
        out_shape=[jax.ShapeDtypeStruct((seqlen, n), F32)] * 2,
        scratch_shapes=[pltpu.VMEM((tm, tn), F32), pltpu.VMEM((tm, tn), F32)],
        compiler_params=_params("parallel", "parallel", "arbitrary"),
        name="filter_spectrum",
    )(fc, fs, hsum, hdiff, nyq)


def _conv_fwd_kernel(fc_ref, fs_ref, u_ref, ka_ref, kb_ref, ya_ref, yb_ref, acc_a, acc_b):
    m = pl.program_id(1)
    k = pl.program_id(2)

    @pl.when(k == 0)
    def _():
        acc_a[...] = jnp.zeros_like(acc_a)
        acc_b[...] = jnp.zeros_like(acc_b)

    u = u_ref[0].astype(BF16)
    acc_a[...] += _dot(fc_ref[...], u)
    acc_b[...] += _dot(fs_ref[...], u)

    @pl.when(k == pl.num_programs(2) - 1)
    def _():
        a, bb = acc_a[...], acc_b[...]
        ka, kb = ka_ref[...], kb_ref[...]
        tm = a.shape[0]
        first = (lax.broadcasted_iota(jnp.int32, (tm, 1), 0) + m * tm) == 0
        ya_ref[0] = jnp.where(first, a * ka, a * ka - bb * kb).astype(BF16)
        yb_ref[0] = jnp.where(first, bb * kb, a * kb + bb * ka).astype(BF16)


def _conv_fwd(fc, fs, u, u_col_block, ka, kb, k_col_block):
    b, seqlen, _ = u.shape
    tm, tk, tn = min(512, seqlen), min(1024, seqlen), H_WIDTH
    return pl.pallas_call(
        _conv_fwd_kernel,
        grid=(b, seqlen // tm, seqlen // tk),
        in_specs=[pl.BlockSpec((tm, tk), lambda bi, m, k: (m, k)),
                  pl.BlockSpec((tm, tk), lambda bi, m, k: (m, k)),
                  pl.BlockSpec((1, tk, tn), lambda bi, m, k: (bi, k, u_col_block)),
                  pl.BlockSpec((tm, tn), lambda bi, m, k: (m, k_col_block)),
                  pl.BlockSpec((tm, tn), lambda bi, m, k: (m, k_col_block))],
        out_specs=[pl.BlockSpec((1, tm, tn), lambda bi, m, k: (bi, m, 0)),
                   pl.BlockSpec((1, tm, tn), lambda bi, m, k: (bi, m, 0))],
        out_shape=[jax.ShapeDtypeStruct((b, seqlen, tn), BF16)] * 2,
        scratch_shapes=[pltpu.VMEM((tm, tn), F32), pltpu.VMEM((tm, tn), F32)],
        compiler_params=_params("parallel", "parallel", "arbitrary"),
        name="hyena_conv_fwd",
    )(fc, fs, u, ka, kb)


def _conv_inv_kernel(fc_ref, fst_ref, ya_ref, yb_ref, gate_ref, zin_ref, bias_ref, z_ref, acc):
    k = pl.program_id(2)

    @pl.when(k == 0)
    def _():
        acc[...] = jnp.zeros_like(acc)

    acc[...] += _dot(fc_ref[...], ya_ref[0]) + _dot(fst_ref[...], yb_ref[0])

    @pl.when(k == pl.num_programs(2) - 1)
    def _():
        z_ref[0] = gate_ref[0] * (acc[...] + bias_ref[...] * zin_ref[0])


def _conv_inv(fc, fst, ya, yb, gate_src, gate_block, zin_src, zin_block, bias_row):
    b, seqlen, tn = ya.shape
    tm, tk = min(512, seqlen), min(1024, seqlen)
    return pl.pallas_call(
        _conv_inv_kernel,
        grid=(b, seqlen // tm, seqlen // tk),
        in_specs=[pl.BlockSpec((tm, tk), lambda bi, m, k: (m, k)),
                  pl.BlockSpec((tm, tk), lambda bi, m, k: (m, k)),
                  pl.BlockSpec((1, tk, tn), lambda bi, m, k: (bi, k, 0)),
                  pl.BlockSpec((1, tk, tn), lambda bi, m, k: (bi, k, 0)),
                  pl.BlockSpec((1, tm, tn), lambda bi, m, k: (bi, m, gate_block)),
                  pl.BlockSpec((1, tm, tn), lambda bi, m, k: (bi, m, zin_block)),
                  pl.BlockSpec((1, tn), lambda bi, m, k: (0, 0))],
        out_specs=pl.BlockSpec((1, tm, tn), lambda bi, m, k: (bi, m, 0)),
        out_shape=jax.ShapeDtypeStruct((b, seqlen, tn), F32),
        scratch_shapes=[pltpu.VMEM((tm, tn), F32)],
        compiler_params=_params("parallel", "parallel", "arbitrary"),
        name="hyena_conv_inv",
    )(fc, fst, ya, yb, gate_src, zin_src, bias_row)


def _output_kernel(y_ref, z_ref, hg_ref, gm_ref, gh_ref, yh_ref, x_ref, gate_ref, mnw_ref,
                   fnw_ref, wm_ref, wh_ref, wo_ref, o_ref):
    g = (y_ref[0, 0] + y_ref[1, 0]) * _silu(z_ref[0])
    gw = D_MODEL // M_GROUPS
    parts = []
    for i in range(M_GROUPS):
        gi = g[:, i * gw:(i + 1) * gw]
        parts.append(gi * lax.rsqrt(jnp.mean(gi * gi, axis=-1, keepdims=True) + NORM_EPS))
    ym = (jnp.concatenate(parts, axis=1) * mnw_ref[...]).astype(BF16)
    yh = (yh_ref[0] * _silu(hg_ref[0])).astype(BF16)
    merged = (jax.nn.sigmoid(gm_ref[0]) * _dot(ym, wm_ref[...])
              + jax.nn.sigmoid(gh_ref[0]) * _dot(yh, wh_ref[...]))
    out = _dot(merged.astype(BF16), wo_ref[...])
    xn = x_ref[0] + gate_ref[0] * out
    ms = jnp.mean(xn * xn, axis=-1, keepdims=True)
    o_ref[0] = xn * lax.rsqrt(ms + NORM_EPS) * fnw_ref[...]


def _mixer_output(y, cols, yh, x, gate, m_norm_w, final_norm_w, wm, wh, wo):
    b, l, d = x.shape
    tm = min(256, l)
    tok = lambda blk: pl.BlockSpec((1, tm, d), lambda bi, i: (bi, i, blk))
    wspec = pl.BlockSpec((d, d), lambda bi, i: (0, 0))
    rowspec = pl.BlockSpec((1, d), lambda bi, i: (0, 0))
    return pl.pallas_call(
        _output_kernel,
        grid=(b, l // tm),
        in_specs=[pl.BlockSpec((2, 1, tm, d), lambda bi, i: (0, bi, i, 0)),
                  tok(COL_Z), tok(COL_HGATE), tok(COL_GM), tok(COL_GH), tok(0), tok(0),
                  pl.BlockSpec((1, 1, d), lambda bi, i: (bi, 0, 0)),
                  rowspec, rowspec, wspec, wspec, wspec],
        out_specs=tok(0),
        out_shape=jax.ShapeDtypeStruct((b, l, d), F32),
        compiler_params=_params("parallel", "parallel"),
        name="mixer_output",
    )(y, cols, cols, cols, cols, yh, x, gate, m_norm_w.reshape(1, d),
      final_norm_w.reshape(1, d), wm, wh, wo)


def _layer(x, ctx, c, c_ctx, ada_w, ada_b, norm_w, w_in, m_conv_w, m_conv_b, m_dt_bias, m_a_log,
           m_d, m_norm_w, h_conv_w, h_conv_b, h_w1, h_b1, h_w2, h_b2, h_w3, h_b3, h_freq,
           h_w_out, h_bias, w_branch_m, w_branch_h, w_out, final_norm_w):
    b, l, d = x.shape
    lc = ctx.shape[1]
    ndt = 2 * M_HEADS

    pad_rows = -(b + 1) % SUBLANES
    c_rows = jnp.concatenate([c, c_ctx[None], jnp.zeros((pad_rows, d), F32)], axis=0)
    mod = _adaln(c_rows, ada_w, ada_b)
    shift, scale, gate = (mod[:b, i * d:(i + 1) * d].reshape(b, 1, d) for i in range(3))
    shift_c = jnp.broadcast_to(mod[b, :d], (b, 1, d))
    scale_c = jnp.broadcast_to(mod[b, d:2 * d], (b, 1, d))

    o_z, o_xbc, o_dt = 0, d, d + M_XBC
    o_hg = o_dt + ndt
    o_hp = o_hg + H_WIDTH
    o_gm = o_hp + 3 * H_WIDTH
    w_xbc = w_in[:, o_xbc:o_dt]
    w_main = jnp.concatenate([w_xbc, w_in[:, o_z:o_xbc], w_in[:, o_hg:]], axis=1).astype(BF16)
    w_dt = w_in[:, o_dt:o_hg]
    w_dt_pad = jnp.zeros((d, DT_PAD), F32).at[:, :ndt].set(w_dt)
    w_dt_t = w_dt.T
    nw = norm_w.reshape(1, d)

    cols, dt_l, dtt_l = _inproj(x, shift, scale, nw, w_main, w_dt_pad, w_dt_t, min(1024, l))
    cols_c, dt_c, dtt_c = _inproj(ctx, shift_c, scale_c, nw, w_xbc.astype(BF16), w_dt_pad,
                                  w_dt_t, min(1024, lc))

    xbc_l = _dwconv(cols, COL_XBC, m_conv_w, m_conv_b, True, "mamba_conv")
    xbc_c = _dwconv(cols_c, 0, m_conv_w, m_conv_b, True, "mamba_conv_ctx")
    bias_flat = m_dt_bias.reshape(ndt)
    a_flat = -jnp.exp(m_a_log.reshape(ndt))
    bias_r = jnp.zeros((1, DT_PAD), F32).at[0, :ndt].set(bias_flat)
    a_r = jnp.zeros((1, DT_PAD), F32).at[0, :ndt].set(a_flat)
    dskip_row = jnp.repeat(m_d, M_HEADDIM).reshape(1, d)
    y = _ssd(xbc_l, xbc_c, dt_l, dt_c, dtt_l, dtt_c, bias_r, a_r,
             bias_flat.reshape(ndt, 1), a_flat.reshape(ndt, 1), dskip_row)

    u3 = _dwconv(cols, COL_HPROJ, h_conv_w, h_conv_b, False, "hyena_conv3")
    hsum, hdiff, nyq = _hyena_filters(l, h_w1, h_b1, h_w2, h_b2, h_w3, h_b3, h_freq, h_w_out)
    fc, fs, fst = _dft_tables(l)
    ka, kb = _filter_spectrum(fc, fs, hsum, hdiff, nyq)
    ya, yb = _conv_fwd(fc, fs, u3, 0, ka, kb, 0)
    z2 = _conv_inv(fc, fst, ya, yb, u3, 1, u3, 0, h_bias[0].reshape(1, H_WIDTH))
    ya, yb = _conv_fwd(fc, fs, z2, 0, ka, kb, 1)
    y_h = _conv_inv(fc, fst, ya, yb, u3, 2, z2, 0, h_bias[1].reshape(1, H_WIDTH))

    return _mixer_output(y, cols, y_h, x, gate, m_norm_w, final_norm_w,
                         w_branch_m.astype(BF16), w_branch_h.astype(BF16), w_out.astype(BF16))


def kernel(x, c, ctx, c_ctx, ada_w, ada_b, norm_w, w_in, m_conv_w, m_conv_b, m_dt_bias, m_a_log,
           m_d, m_norm_w, h_conv_w, h_conv_b, h_w1, h_b1, h_w2, h_b2, h_w3, h_b3, h_freq,
           h_w_out, h_bias, w_branch_m, w_branch_h, w_out, final_norm_w):
    assert ada_w.shape[0] == 1, "one trunk layer"
    return _layer(x, ctx, c, c_ctx, ada_w[0], ada_b[0], norm_w[0], w_in[0], m_conv_w[0],
                  m_conv_b[0], m_dt_bias[0], m_a_log[0], m_d[0], m_norm_w[0], h_conv_w[0],
                  h_conv_b[0], h_w1[0], h_b1[0], h_w2[0], h_b2[0], h_w3[0], h_b3[0], h_freq[0],
                  h_w_out[0], h_bias[0], w_branch_m[0], w_branch_h[0], w_out[0], final_norm_w)
```

```python
import functools
import math

import jax
import jax.numpy as jnp
from jax import lax
from jax.experimental import pallas as pl
from jax.experimental.pallas import tpu as pltpu

F32 = jnp.float32
BF16 = jnp.bfloat16
HIGHEST = lax.Precision.HIGHEST

LANES = 128
SUBLANES = 8
VMEM_LIMIT = 56 * 1024 * 1024

NORM_EPS = 1e-6
D_MODEL = 1024
M_HEADDIM = 64
M_HEADS = 16
M_GROUPS = 4
M_HPG = M_HEADS // M_GROUPS
M_STATE = 128
M_XBC = D_MODEL + 2 * M_GROUPS * M_STATE
M_CHUNK = 128
HEADDIM_SHIFT = M_HEADDIM.bit_length() - 1
CHUNK_SHIFT = M_CHUNK.bit_length() - 1
H_WIDTH = 1024
H_EMB = 33
H_HID = 64
H_DECAY_TARGET = 1e-2
H_FAST_DECAY_PCT = 0.3
H_SLOW_DECAY_PCT = 1.5
DT_PAD = LANES
FFT_RADIX = 16
ROW_CHUNK = 16

COL_XBC, COL_Z, COL_HGATE, COL_HPROJ, COL_GM, COL_GH = 0, 2, 3, 4, 7, 8
N_MAIN = 9 * 1024


def _dot(a, b, precision=None):
    return jnp.dot(a, b, preferred_element_type=F32, precision=precision)


def _dot_nt(a, b, precision=None):
    return lax.dot_general(a, b, (((1,), (1,)), ((), ())), preferred_element_type=F32,
                           precision=precision)


def _dot_tn(a, b, precision=None):
    return lax.dot_general(a, b, (((0,), (0,)), ((), ())), preferred_element_type=F32,
                           precision=precision)


def _params(*sem):
    return pltpu.CompilerParams(dimension_semantics=sem, vmem_limit_bytes=VMEM_LIMIT)


def _silu(v):
    return v * jax.nn.sigmoid(v)


def _adaln_kernel(c_ref, w_ref, b_ref, o_ref):
    o_ref[...] = _dot(_silu(c_ref[...]), w_ref[...], HIGHEST) + b_ref[...]


def _adaln(c_rows, ada_w, ada_b):
    rows, d = c_rows.shape
    n = ada_w.shape[1]
    tn = 1024
    return pl.pallas_call(
        _adaln_kernel,
        grid=(n // tn,),
        in_specs=[pl.BlockSpec((rows, d), lambda j: (0, 0)),
                  pl.BlockSpec((d, tn), lambda j: (0, j)),
                  pl.BlockSpec((1, tn), lambda j: (0, j))],
        out_specs=pl.BlockSpec((rows, tn), lambda j: (0, j)),
        out_shape=jax.ShapeDtypeStruct((rows, n), F32),
        compiler_params=_params("parallel"),
        name="adaln",
    )(c_rows, ada_w, ada_b.reshape(1, n))


def _inproj_kernel(x_ref, shift_ref, scale_ref, nw_ref, w_ref, wdt_ref, wdtt_ref,
                   o_ref, dt_ref, dtt_ref, h_scr):
    @pl.when(pl.program_id(2) == 0)
    def _():
        xt = x_ref[0]
        ms = jnp.mean(xt * xt, axis=-1, keepdims=True)
        hn = xt * lax.rsqrt(ms + NORM_EPS) * nw_ref[...]
        h = hn * (1.0 + scale_ref[0]) + shift_ref[0]
        h_scr[...] = h.astype(BF16)
        dt_ref[0] = _dot(h, wdt_ref[...], HIGHEST)
        dtt_ref[0] = _dot_nt(wdtt_ref[...], h, HIGHEST)

    o_ref[0] = _dot(h_scr[...], w_ref[...])


def _inproj(x, shift, scale, norm_w, w_main, w_dt, w_dt_t, tm):
    b, l, d = x.shape
    n = w_main.shape[1]
    tn = 1024
    ndt = w_dt_t.shape[0]
    return pl.pallas_call(
        _inproj_kernel,
        grid=(b, l // tm, n // tn),
        in_specs=[pl.BlockSpec((1, tm, d), lambda bi, i, j: (bi, i, 0)),
                  pl.BlockSpec((1, 1, d), lambda bi, i, j: (bi, 0, 0)),
                  pl.BlockSpec((1, 1, d), lambda bi, i, j: (bi, 0, 0)),
                  pl.BlockSpec((1, d), lambda bi, i, j: (0, 0)),
                  pl.BlockSpec((d, tn), lambda bi, i, j: (0, j)),
                  pl.BlockSpec((d, DT_PAD), lambda bi, i, j: (0, 0)),
                  pl.BlockSpec((ndt, d), lambda bi, i, j: (0, 0))],
        out_specs=[pl.BlockSpec((1, tm, tn), lambda bi, i, j: (bi, i, j)),
                   pl.BlockSpec((1, tm, DT_PAD), lambda bi, i, j: (bi, i, 0)),
                   pl.BlockSpec((1, ndt, tm), lambda bi, i, j: (bi, 0, i))],
        out_shape=[jax.ShapeDtypeStruct((b, l, n), F32),
                   jax.ShapeDtypeStruct((b, l, DT_PAD), F32),
                   jax.ShapeDtypeStruct((b, ndt, l), F32)],
        scratch_shapes=[pltpu.VMEM((tm, d), BF16)],
        compiler_params=_params("parallel", "parallel", "arbitrary"),
        name="inproj",
    )(x, shift, scale, norm_w, w_main, w_dt, w_dt_t)


def _dwconv_kernel(cur_ref, prev_ref, next_ref, w_ref, b_ref, o_ref, *, apply_silu):
    r = pl.program_id(1)
    nr = pl.num_programs(1)
    u = cur_ref[0]
    rows = u.shape[0]
    prev_row = jnp.where(r > 0, prev_ref[0][SUBLANES - 1:SUBLANES, :], 0.0)
    next_row = jnp.where(r < nr - 1, next_ref[0][0:1, :], 0.0)
    ridx = lax.broadcasted_iota(jnp.int32, (rows, 1), 0)
    up = jnp.where(ridx == 0, prev_row, pltpu.roll(u, 1, 0))
    un = jnp.where(ridx == rows - 1, next_row, pltpu.roll(u, rows - 1, 0))
    w = w_ref[...]
    y = w[0:1, :] * up + w[1:2, :] * u + w[2:3, :] * un + b_ref[...]
    if apply_silu:
        y = _silu(y)
    o_ref[0] = y.astype(o_ref.dtype)


def _dwconv(src, col_off_blocks, conv_w, conv_b, apply_silu, name, out_dtype=F32):
    b, l, _ = src.shape
    c = conv_w.shape[1]
    cb = 512
    rr = min(512, l)
    nsub = rr // SUBLANES
    last = l // SUBLANES - 1
    off = col_off_blocks * (1024 // cb)
    return pl.pallas_call(
        functools.partial(_dwconv_kernel, apply_silu=apply_silu),
        grid=(b, l // rr, c // cb),
        in_specs=[pl.BlockSpec((1, rr, cb), lambda bi, r, ci: (bi, r, off + ci)),
                  pl.BlockSpec((1, SUBLANES, cb),
                               lambda bi, r, ci: (bi, jnp.maximum(r * nsub - 1, 0), off + ci)),
                  pl.BlockSpec((1, SUBLANES, cb),
                               lambda bi, r, ci: (bi, jnp.minimum((r + 1) * nsub, last), off + ci)),
                  pl.BlockSpec((3, cb), lambda bi, r, ci: (0, ci)),
                  pl.BlockSpec((1, cb), lambda bi, r, ci: (0, ci))],
        out_specs=pl.BlockSpec((1, rr, cb), lambda bi, r, ci: (bi, r, ci)),
        out_shape=jax.ShapeDtypeStruct((b, l, c), out_dtype),
        compiler_params=_params("parallel", "parallel", "parallel"),
        name=name,
    )(src, src, src, conv_w, conv_b.reshape(1, c))


def _softplus(v):
    return jnp.maximum(v, 0.0) + jnp.log1p(jnp.exp(-jnp.abs(v)))


def _expand_heads(v, e_bf16):
    hi = v.astype(BF16)
    lo = (v - hi.astype(F32)).astype(BF16)
    return _dot(hi, e_bf16) + _dot(lo, e_bf16)


def _ssd_kernel(xl_ref, xc_ref, dtl_ref, dtc_ref, dtlt_ref, dtct_ref, bias_r_ref, a_r_ref,
                bias_c_ref, a_c_ref, dskip_ref, y_ref, state, *, n_ctx_chunks):
    d = pl.program_id(1)
    s = pl.program_id(2)
    is_ctx = s < n_ctx_chunks
    fwd = d == 0
    q = M_CHUNK

    @pl.when(s == 0)
    def _():
        state[...] = jnp.zeros_like(state)

    xbc = jnp.where(is_ctx, xc_ref[0], xl_ref[0])
    x = xbc[:, :D_MODEL]
    bmat = xbc[:, D_MODEL:D_MODEL + M_GROUPS * M_STATE].astype(BF16)
    cmat = xbc[:, D_MODEL + M_GROUPS * M_STATE:].astype(BF16)

    dt_raw = jnp.where(is_ctx, dtc_ref[0], dtl_ref[0])
    dt_all = _softplus(dt_raw + bias_r_ref[...])
    a_all = dt_all * a_r_ref[...]
    dt16 = jnp.where(fwd, dt_all[:, 0:M_HEADS], dt_all[:, M_HEADS:2 * M_HEADS])
    a16 = jnp.where(fwd, a_all[:, 0:M_HEADS], a_all[:, M_HEADS:2 * M_HEADS])
    dt_raw_t = jnp.where(is_ctx, dtct_ref[0], dtlt_ref[0])
    a_all_t = _softplus(dt_raw_t + bias_c_ref[...]) * a_c_ref[...]
    a16_t = jnp.where(fwd, a_all_t[0:M_HEADS, :], a_all_t[M_HEADS:2 * M_HEADS, :])

    ri = lax.broadcasted_iota(jnp.int32, (q, q), 0)
    ci = lax.broadcasted_iota(jnp.int32, (q, q), 1)
    order = jnp.where(fwd, ri - ci, ci - ri)
    mask = order >= 0
    tri = jnp.where(mask, 1.0, 0.0)
    tri_t = jnp.where(order <= 0, 1.0, 0.0)
    acs = _dot(tri, a16, HIGHEST)
    acs_t = _dot(a16_t, tri_t, HIGHEST)
    atot_r = jnp.where(fwd, acs[q - 1:q, :], acs[0:1, :])
    atot_c = jnp.where(fwd, acs_t[:, q - 1:q], acs_t[:, 0:1])
    e_atot_c = jnp.exp(atot_c)

    hid = lax.broadcasted_iota(jnp.int32, (M_HEADS, D_MODEL), 0)
    lid = lax.broadcasted_iota(jnp.int32, (M_HEADS, D_MODEL), 1)
    expand = jnp.where((lid >> HEADDIM_SHIFT) == hid, 1.0, 0.0).astype(BF16)
    xdt = x * _expand_heads(dt16, expand)
    xw = (xdt * _expand_heads(jnp.exp(atot_r - acs), expand)).astype(BF16)

    gw = M_HPG * M_HEADDIM

    @pl.when(jnp.logical_not(is_ctx))
    def _():
        e_acs = _expand_heads(jnp.exp(acs), expand)
        rb = lax.broadcasted_iota(jnp.int32, (M_HPG * q, gw), 0) >> CHUNK_SHIFT
        cb = lax.broadcasted_iota(jnp.int32, (M_HPG * q, gw), 1) >> HEADDIM_SHIFT
        blockdiag = rb == cb
        for g in range(M_GROUPS):
            bg = bmat[:, g * M_STATE:(g + 1) * M_STATE]
            cg = cmat[:, g * M_STATE:(g + 1) * M_STATE]
            scores = _dot_nt(cg, bg)
            ms = []
            for r in range(M_HPG):
                h = g * M_HPG + r
                seg = acs[:, h:h + 1] - acs_t[h:h + 1, :]
                decay = jnp.exp(jnp.where(mask, seg, -jnp.inf))
                ms.append((scores * decay).astype(BF16))
            m_cat = jnp.concatenate(ms, axis=1)
            xg = xdt[:, g * gw:(g + 1) * gw]
            x_bd = jnp.where(blockdiag, jnp.concatenate([xg] * M_HPG, axis=0),
                             0.0).astype(BF16)
            y_diag = _dot(m_cat, x_bd)
            s_g = state[g * M_HPG:(g + 1) * M_HPG].reshape(gw, M_STATE).astype(BF16)
            y_off = _dot_nt(cg, s_g)
            y_g = y_diag + e_acs[:, g * gw:(g + 1) * gw] * y_off
            skip = jnp.where(fwd, dskip_ref[:, g * gw:(g + 1) * gw], 0.0)
            y_ref[0, 0, :, g * gw:(g + 1) * gw] = y_g + skip * x[:, g * gw:(g + 1) * gw]

    for g in range(M_GROUPS):
        bg = bmat[:, g * M_STATE:(g + 1) * M_STATE]
        upd = _dot_tn(xw[:, g * gw:(g + 1) * gw], bg)
        for r in range(M_HPG):
            h = g * M_HPG + r
            state[h] = (state[h] * e_atot_c[h:h + 1, 0:1]
                        + upd[r * M_HEADDIM:(r + 1) * M_HEADDIM, :])


def _ssd(xbc_l, xbc_c, dt_l, dt_c, dtt_l, dtt_c, bias_r, a_r, bias_c, a_c, dskip_row):
    b, l, cw = xbc_l.shape
    lc = xbc_c.shape[1]
    q = M_CHUNK
    nl, nc = l // q, lc // q
    ndt = dtt_l.shape[1]

    def lat(di, si):
        f = jnp.maximum(si - nc, 0)
        return jnp.where(di == 0, f, nl - 1 - f)

    def ctx(di, si):
        g = jnp.minimum(si, nc - 1)
        return jnp.where(di == 0, g, nc - 1 - g)

    small = lambda shape: pl.BlockSpec(shape, lambda bi, di, si: (0, 0))
    return pl.pallas_call(
        functools.partial(_ssd_kernel, n_ctx_chunks=nc),
        grid=(b, 2, nl + nc),
        in_specs=[pl.BlockSpec((1, q, cw), lambda bi, di, si: (bi, lat(di, si), 0)),
                  pl.BlockSpec((1, q, cw), lambda bi, di, si: (bi, ctx(di, si), 0)),
                  pl.BlockSpec((1, q, DT_PAD), lambda bi, di, si: (bi, lat(di, si), 0)),
                  pl.BlockSpec((1, q, DT_PAD), lambda bi, di, si: (bi, ctx(di, si), 0)),
                  pl.BlockSpec((1, ndt, q), lambda bi, di, si: (bi, 0, lat(di, si))),
                  pl.BlockSpec((1, ndt, q), lambda bi, di, si: (bi, 0, ctx(di, si))),
                  small((1, DT_PAD)), small((1, DT_PAD)), small((ndt, 1)), small((ndt, 1)),
                  small((1, D_MODEL))],
        out_specs=pl.BlockSpec((1, 1, q, D_MODEL), lambda bi, di, si: (di, bi, lat(di, si), 0)),
        out_shape=jax.ShapeDtypeStruct((2, b, l, D_MODEL), F32),
        scratch_shapes=[pltpu.VMEM((M_HEADS, M_HEADDIM, M_STATE), F32)],
        compiler_params=_params("parallel", "arbitrary", "arbitrary"),
        name="ssd",
    )(xbc_l, xbc_c, dt_l, dt_c, dtt_l, dtt_c, bias_r, a_r, bias_c, a_c, dskip_row)


def _filter_kernel(fl_ref, w1_ref, b1_ref, w2_ref, b2_ref, w3_ref, b3_ref, freq_ref, wo_ref,
                   deltas_ref, h_ref, *, seqlen):
    i = pl.program_id(0)
    tr = h_ref.shape[0]
    pos = (i * tr + lax.broadcasted_iota(jnp.int32, (tr, 1), 0))
    posf = pos.astype(F32)
    t = posf / F32(seqlen - 1)
    w = F32(2.0 * math.pi / seqlen) * posf
    ang = fl_ref[...] * w
    lane = lax.broadcasted_iota(jnp.int32, (tr, LANES), 1)
    bands = (H_EMB - 1) // 2
    feats = jnp.where(lane == 0, t,
                      jnp.where(lane <= bands, jnp.cos(ang),
                                jnp.where(lane < H_EMB, -jnp.sin(ang), 0.0)))
    freq = freq_ref[...]
    hid = jnp.sin(freq * (_dot(feats, w1_ref[...], HIGHEST) + b1_ref[...]))
    hid = jnp.sin(freq * (_dot(hid, w2_ref[...], HIGHEST) + b2_ref[...]))
    hid = jnp.sin(freq * (_dot(hid, w3_ref[...], HIGHEST) + b3_ref[...]))
    filt = _dot(hid, wo_ref[...], HIGHEST)
    window = jnp.exp(-t * deltas_ref[...])
    hw = H_WIDTH
    for blk in range(filt.shape[1] // hw):
        h_ref[:, blk * hw:(blk + 1) * hw] = filt[:, blk * hw:(blk + 1) * hw] * window


def _hyena_filters(seqlen, w1, b1, w2, b2, w3, b3, freq, w_out):
    tr = min(512, seqlen)
    bands = (H_EMB - 1) // 2
    f = jnp.linspace(1e-4, bands - 1, bands, dtype=F32)
    fl = jnp.zeros((1, LANES), F32).at[0, 1:1 + bands].set(f).at[0, 1 + bands:H_EMB].set(f)
    w1p = jnp.zeros((LANES, H_HID), F32).at[:H_EMB].set(w1)
    min_decay = math.log(H_DECAY_TARGET) / H_SLOW_DECAY_PCT
    max_decay = math.log(H_DECAY_TARGET) / H_FAST_DECAY_PCT
    deltas = jnp.abs(jnp.linspace(min_decay, max_decay, H_WIDTH, dtype=F32)).reshape(1, H_WIDTH)
    row = lambda v: v.reshape(1, -1)
    full = lambda a: pl.BlockSpec(a.shape, lambda i: (0, 0))
    args = (fl, w1p, row(b1), w2, row(b2), w3, row(b3), row(freq), w_out, deltas)
    nout = w_out.shape[1]
    return pl.pallas_call(
        functools.partial(_filter_kernel, seqlen=seqlen),
        grid=(seqlen // tr,),
        in_specs=[full(a) for a in args],
        out_specs=pl.BlockSpec((tr, nout), lambda i: (i, 0)),
        out_shape=jax.ShapeDtypeStruct((seqlen, nout), F32),
        compiler_params=_params("parallel"),
        name="hyena_filters",
    )(*args)


def _r_add(a, b):
    if a is None:
        return b
    if b is None:
        return a
    return a + b


def _r_sub(a, b):
    if b is None:
        return a
    if a is None:
        return -b
    return a - b


def _r_scale(a, s):
    if a is None or s == 0.0:
        return None
    if s == 1.0:
        return a
    if s == -1.0:
        return -a
    return a * s


def _c_mul_const(z, w):
    snap = lambda v: float(round(v)) if abs(v - round(v)) < 1e-12 else float(v)
    wr, wi = snap(w.real), snap(w.imag)
    zr, zi = z
    return (_r_sub(_r_scale(zr, wr), _r_scale(zi, wi)),
            _r_add(_r_scale(zr, wi), _r_scale(zi, wr)))


def _fft(xs, sign, first_half_only=False):
    n = len(xs)
    if n == 1:
        return list(xs)
    even, odd = _fft(xs[0::2], sign), _fft(xs[1::2], sign)
    out = [None] * n
    for k in range(n // 2):
        tr, ti = _c_mul_const(odd[k], complex(math.cos(2 * math.pi * k / n),
                                              sign * math.sin(2 * math.pi * k / n)))
        out[k] = (_r_add(even[k][0], tr), _r_add(even[k][1], ti))
        if not first_half_only:
            out[k + n // 2] = (_r_sub(even[k][0], tr), _r_sub(even[k][1], ti))
    return out[:n // 2] if first_half_only else out


def _tile_or_zero(v):
    return jnp.zeros((ROW_CHUNK, LANES), F32) if v is None else v


def _radix_fwd_kernel(z_ref, o_ref, *, real_input):
    _, _, half, rows_total, width = z_ref.shape

    def body(i, carry):
        rows = pl.ds(pl.multiple_of(i * ROW_CHUNK, ROW_CHUNK), ROW_CHUNK)
        for j in range(width // LANES):
            lanes = slice(j * LANES, (j + 1) * LANES)
            xs = []
            for t1 in range(half):
                re = z_ref[0, 0, t1, rows, lanes].astype(F32)
                im = None if real_input else z_ref[0, 1, t1, rows, lanes].astype(F32)
                xs.append((re, im))
            xs += [(None, None)] * half
            for k1, (re, im) in enumerate(_fft(xs, -1.0)):
                o_ref[0, k1, 0, rows, lanes] = _tile_or_zero(re).astype(BF16)
                o_ref[0, k1, 1, rows, lanes] = _tile_or_zero(im).astype(BF16)
        return carry

    lax.fori_loop(0, rows_total // ROW_CHUNK, body, 0)


def _radix_fwd(z, col_block, ncols, real_input, name):
    p, parts, half, m, _ = z.shape
    tt, cb = min(256, m), 256
    off = col_block * (H_WIDTH // cb)
    return pl.pallas_call(
        functools.partial(_radix_fwd_kernel, real_input=real_input),
        grid=(p, m // tt, ncols // cb),
        in_specs=[pl.BlockSpec((1, parts, half, tt, cb), lambda pi, ti, ci: (pi, 0, 0, ti, off + ci))],
        out_specs=pl.BlockSpec((1, 2 * half, 2, tt, cb), lambda pi, ti, ci: (pi, 0, 0, ti, ci)),
        out_shape=jax.ShapeDtypeStruct((p, 2 * half, 2, m, ncols), BF16),
        compiler_params=_params("parallel", "parallel", "parallel"),
        name=name,
    )(z)


def _radix_inv_kernel(sb_ref, gate_ref, zin_ref, bias_ref, o_ref):
    _, radix, _, rows_total, width = sb_ref.shape

    def body(i, carry):
        rows = pl.ds(pl.multiple_of(i * ROW_CHUNK, ROW_CHUNK), ROW_CHUNK)
        for j in range(width // LANES):
            lanes = slice(j * LANES, (j + 1) * LANES)
            ys = [(sb_ref[0, k1, 0, rows, lanes].astype(F32),
                   sb_ref[0, k1, 1, rows, lanes].astype(F32)) for k1 in range(radix)]
            bias = bias_ref[:, lanes]
            for t1, parts in enumerate(_fft(ys, 1.0, first_half_only=True)):
                for q in range(2):
                    zin = zin_ref[0, q, t1, rows, lanes].astype(F32)
                    gate = gate_ref[0, q, t1, rows, lanes].astype(F32)
                    o_ref[0, q, t1, rows, lanes] = (gate * (parts[q] + bias * zin)).astype(o_ref.dtype)
        return carry

    lax.fori_loop(0, rows_total // ROW_CHUNK, body, 0)


def _radix_inv(sb, gate_src, gate_block, zin_src, zin_block, bias_row, name):
    p, radix, _, m, ncols = sb.shape
    half = radix // 2
    tt, cb = min(256, m), 256
    goff, zoff = gate_block * (H_WIDTH // cb), zin_block * (H_WIDTH // cb)
    tok = lambda off: pl.BlockSpec((1, 2, half, tt, cb), lambda pi, ti, ci: (pi, 0, 0, ti, off + ci))
    return pl.pallas_call(
        _radix_inv_kernel,
        grid=(p, m // tt, ncols // cb),
        in_specs=[pl.BlockSpec((1, radix, 2, tt, cb), lambda pi, ti, ci: (pi, 0, 0, ti, ci)),
                  tok(goff), tok(zoff),
                  pl.BlockSpec((1, cb), lambda pi, ti, ci: (0, ci))],
        out_specs=tok(0),
        out_shape=jax.ShapeDtypeStruct((p, 2, half, m, ncols), BF16),
        compiler_params=_params("parallel", "parallel", "parallel"),
        name=name,
    )(sb, gate_src, zin_src, bias_row)


def _dft_seed_kernel(cb_ref, sb_ref, ca_ref, sa_ref, *, n):
    m = cb_ref.shape[0]
    r = lax.broadcasted_iota(jnp.int32, (m, m), 0)
    c = lax.broadcasted_iota(jnp.int32, (m, m), 1)
    beta = ((r * c) & (m - 1)).astype(F32) * F32(2.0 * math.pi / m)
    cb_ref[...] = jnp.cos(beta)
    sb_ref[...] = jnp.sin(beta)
    k1 = lax.broadcasted_iota(jnp.int32, ca_ref.shape, 0)
    t2 = lax.broadcasted_iota(jnp.int32, ca_ref.shape, 1)
    alpha = (k1 * t2).astype(F32) * F32(2.0 * math.pi / n)
    ca_ref[...] = jnp.cos(alpha)
    sa_ref[...] = jnp.sin(alpha)


def _dft_table_kernel(cb_ref, sb_ref, car_ref, sar_ref, cac_ref, sac_ref, efwd_ref, einv_ref):
    m = cb_ref.shape[0]
    cb, sb = cb_ref[...], sb_ref[...]
    car, sar = car_ref[0], sar_ref[0]
    cos_f = (car * cb - sar * sb).astype(BF16)
    sin_f = (sar * cb + car * sb).astype(BF16)
    efwd_ref[0, :m, :m] = cos_f
    efwd_ref[0, :m, m:] = sin_f
    efwd_ref[0, m:, :m] = -sin_f
    efwd_ref[0, m:, m:] = cos_f
    cac, sac = cac_ref[0], sac_ref[0]
    cos_i = (cac * cb - sac * sb).astype(BF16)
    sin_i = (sac * cb + cac * sb).astype(BF16)
    einv_ref[0, :m, :m] = cos_i
    einv_ref[0, :m, m:] = -sin_i
    einv_ref[0, m:, :m] = sin_i
    einv_ref[0, m:, m:] = cos_i


def _dft_tables(n):
    m = n // FFT_RADIX
    cb, sb, ca, sa = pl.pallas_call(
        functools.partial(_dft_seed_kernel, n=n),
        out_shape=[jax.ShapeDtypeStruct((m, m), F32)] * 2
        + [jax.ShapeDtypeStruct((FFT_RADIX, m), F32)] * 2,
        compiler_params=pltpu.CompilerParams(vmem_limit_bytes=VMEM_LIMIT),
        name="dft_seed",
    )()
    full = pl.BlockSpec((m, m), lambda i: (0, 0))
    rowspec = pl.BlockSpec((1, 1, m), lambda i: (i, 0, 0))
    colspec = pl.BlockSpec((1, m, 1), lambda i: (i, 0, 0))
    tab = pl.BlockSpec((1, 2 * m, 2 * m), lambda i: (i, 0, 0))
    return pl.pallas_call(
        _dft_table_kernel,
        grid=(FFT_RADIX,),
        in_specs=[full, full, rowspec, rowspec, colspec, colspec],
        out_specs=[tab, tab],
        out_shape=[jax.ShapeDtypeStruct((FFT_RADIX, 2 * m, 2 * m), BF16)] * 2,
        compiler_params=_params("parallel"),
        name="dft_tables",
    )(cb, sb, ca.reshape(FFT_RADIX, 1, m), sa.reshape(FFT_RADIX, 1, m),
      ca.reshape(FFT_RADIX, m, 1), sa.reshape(FFT_RADIX, m, 1))


def _filter_spectrum_kernel(efwd_ref, hf_ref, hb_ref, kr_ref, ki_ref, *, n):
    m = kr_ref.shape[1]
    xf = _dot(efwd_ref[0], hf_ref[0, 0])
    xb = _dot(efwd_ref[0], hb_ref[0, 0])
    kr_ref[0] = (xf[:m] + xb[:m]) * F32(1.0 / n)
    ki_ref[0] = (xf[m:] - xb[m:]) * F32(1.0 / n)


def _filter_spectrum(efwd, sah):
    radix, m2, _ = efwd.shape
    n = radix * m2 // 2
    orders = sah.shape[3] // (2 * H_WIDTH)
    slab = lambda off: pl.BlockSpec((1, 1, m2, H_WIDTH), lambda k1, o: (0, k1, 0, off + o))
    kspec = pl.BlockSpec((1, m2 // 2, H_WIDTH), lambda k1, o: (k1, 0, o))
    return pl.pallas_call(
        functools.partial(_filter_spectrum_kernel, n=n),
        grid=(radix, orders),
        in_specs=[pl.BlockSpec((1, m2, m2), lambda k1, o: (k1, 0, 0)), slab(0), slab(orders)],
        out_specs=[kspec, kspec],
        out_shape=[jax.ShapeDtypeStruct((radix, m2 // 2, orders * H_WIDTH), F32)] * 2,
        compiler_params=_params("parallel", "arbitrary"),
        name="filter_spectrum",
    )(efwd, sah, sah)


def _spectral_kernel(efwd_ref, einv_ref, sa_ref, kr_ref, ki_ref, sb_ref):
    m = kr_ref.shape[1]
    width = sa_ref.shape[3]
    cw = min(512, width)
    for j in range(width // cw):
        cols = slice(j * cw, (j + 1) * cw)
        x = _dot(efwd_ref[0], sa_ref[0, 0, :, cols])
        xr, xi = x[:m], x[m:]
        kr, ki = kr_ref[0, :, cols], ki_ref[0, :, cols]
        y = jnp.concatenate([xr * kr - xi * ki, xr * ki + xi * kr], axis=0).astype(BF16)
        sb_ref[0, 0, :, cols] = _dot(einv_ref[0], y).astype(BF16)


def _spectral(efwd, einv, sa, kr, ki, order):
    p, radix, m2, ncols = sa.shape
    slab = pl.BlockSpec((1, 1, m2, ncols), lambda k1, pi: (pi, k1, 0, 0))
    tab = pl.BlockSpec((1, m2, m2), lambda k1, pi: (k1, 0, 0))
    kspec = pl.BlockSpec((1, m2 // 2, ncols), lambda k1, pi: (k1, 0, order))
    return pl.pallas_call(
        _spectral_kernel,
        grid=(radix, p),
        in_specs=[tab, tab, slab, kspec, kspec],
        out_specs=slab,
        out_shape=jax.ShapeDtypeStruct(sa.shape, BF16),
        compiler_params=_params("parallel", "arbitrary"),
        name="hyena_spectral",
    )(efwd, einv, sa, kr, ki)


def _output_kernel(y_ref, z_ref, hg_ref, gm_ref, gh_ref, yh_ref, x_ref, gate_ref, mnw_ref,
                   fnw_ref, wm_ref, wh_ref, wo_ref, o_ref):
    g = (y_ref[0, 0] + y_ref[1, 0]) * _silu(z_ref[0])
    gw = D_MODEL // M_GROUPS
    parts = []
    for i in range(M_GROUPS):
        gi = g[:, i * gw:(i + 1) * gw]
        parts.append(gi * lax.rsqrt(jnp.mean(gi * gi, axis=-1, keepdims=True) + NORM_EPS))
    ym = (jnp.concatenate(parts, axis=1) * mnw_ref[...]).astype(BF16)
    yh = (yh_ref[0] * _silu(hg_ref[0])).astype(BF16)
    merged = (jax.nn.sigmoid(gm_ref[0]) * _dot(ym, wm_ref[...])
              + jax.nn.sigmoid(gh_ref[0]) * _dot(yh, wh_ref[...]))
    out = _dot(merged.astype(BF16), wo_ref[...])
    xn = x_ref[0] + gate_ref[0] * out
    ms = jnp.mean(xn * xn, axis=-1, keepdims=True)
    o_ref[0] = xn * lax.rsqrt(ms + NORM_EPS) * fnw_ref[...]


def _mixer_output(y, cols, yh, x, gate, m_norm_w, final_norm_w, wm, wh, wo):
    b, l, d = x.shape
    tm = min(256, l)
    tok = lambda blk: pl.BlockSpec((1, tm, d), lambda bi, i: (bi, i, blk))
    wspec = pl.BlockSpec((d, d), lambda bi, i: (0, 0))
    rowspec = pl.BlockSpec((1, d), lambda bi, i: (0, 0))
    return pl.pallas_call(
        _output_kernel,
        grid=(b, l // tm),
        in_specs=[pl.BlockSpec((2, 1, tm, d), lambda bi, i: (0, bi, i, 0)),
                  tok(COL_Z), tok(COL_HGATE), tok(COL_GM), tok(COL_GH), tok(0), tok(0),
                  pl.BlockSpec((1, 1, d), lambda bi, i: (bi, 0, 0)),
                  rowspec, rowspec, wspec, wspec, wspec],
        out_specs=tok(0),
        out_shape=jax.ShapeDtypeStruct((b, l, d), F32),
        compiler_params=_params("parallel", "parallel"),
        name="mixer_output",
    )(y, cols, cols, cols, cols, yh, x, gate, m_norm_w.reshape(1, d),
      final_norm_w.reshape(1, d), wm, wh, wo)


def _layer(x, ctx, c, c_ctx, ada_w, ada_b, norm_w, w_in, m_conv_w, m_conv_b, m_dt_bias, m_a_log,
           m_d, m_norm_w, h_conv_w, h_conv_b, h_w1, h_b1, h_w2, h_b2, h_w3, h_b3, h_freq,
           h_w_out, h_bias, w_branch_m, w_branch_h, w_out, final_norm_w):
    b, l, d = x.shape
    lc = ctx.shape[1]
    ndt = 2 * M_HEADS

    pad_rows = -(b + 1) % SUBLANES
    c_rows = jnp.concatenate([c, c_ctx[None], jnp.zeros((pad_rows, d), F32)], axis=0)
    mod = _adaln(c_rows, ada_w, ada_b)
    shift, scale, gate = (mod[:b, i * d:(i + 1) * d].reshape(b, 1, d) for i in range(3))
    shift_c = jnp.broadcast_to(mod[b, :d], (b, 1, d))
    scale_c = jnp.broadcast_to(mod[b, d:2 * d], (b, 1, d))

    o_z, o_xbc, o_dt = 0, d, d + M_XBC
    o_hg = o_dt + ndt
    o_hp = o_hg + H_WIDTH
    o_gm = o_hp + 3 * H_WIDTH
    w_xbc = w_in[:, o_xbc:o_dt]
    w_main = jnp.concatenate([w_xbc, w_in[:, o_z:o_xbc], w_in[:, o_hg:]], axis=1).astype(BF16)
    w_dt = w_in[:, o_dt:o_hg]
    w_dt_pad = jnp.zeros((d, DT_PAD), F32).at[:, :ndt].set(w_dt)
    w_dt_t = w_dt.T
    nw = norm_w.reshape(1, d)

    cols, dt_l, dtt_l = _inproj(x, shift, scale, nw, w_main, w_dt_pad, w_dt_t, min(1024, l))
    cols_c, dt_c, dtt_c = _inproj(ctx, shift_c, scale_c, nw, w_xbc.astype(BF16), w_dt_pad,
                                  w_dt_t, min(1024, lc))

    xbc_l = _dwconv(cols, COL_XBC, m_conv_w, m_conv_b, True, "mamba_conv")
    xbc_c = _dwconv(cols_c, 0, m_conv_w, m_conv_b, True, "mamba_conv_ctx")
    bias_flat = m_dt_bias.reshape(ndt)
    a_flat = -jnp.exp(m_a_log.reshape(ndt))
    bias_r = jnp.zeros((1, DT_PAD), F32).at[0, :ndt].set(bias_flat)
    a_r = jnp.zeros((1, DT_PAD), F32).at[0, :ndt].set(a_flat)
    dskip_row = jnp.repeat(m_d, M_HEADDIM).reshape(1, d)
    y = _ssd(xbc_l, xbc_c, dt_l, dt_c, dtt_l, dtt_c, bias_r, a_r,
             bias_flat.reshape(ndt, 1), a_flat.reshape(ndt, 1), dskip_row)

    assert b % 2 == 0, "batch rows are packed in pairs as complex signals"
    half = FFT_RADIX // 2
    m = l // half
    pairs = b // 2
    u3 = _dwconv(cols, COL_HPROJ, h_conv_w, h_conv_b, False, "hyena_conv3", BF16)
    u3 = u3.reshape(pairs, 2, half, m, 3 * H_WIDTH)
    filt = _hyena_filters(l, h_w1, h_b1, h_w2, h_b2, h_w3, h_b3, h_freq, h_w_out)
    efwd, einv = _dft_tables(2 * l)
    sah = _radix_fwd(filt.reshape(1, 1, half, m, filt.shape[1]), 0, filt.shape[1], True,
                     "filter_radix")
    kr, ki = _filter_spectrum(efwd, sah.reshape(1, FFT_RADIX, 2 * m, filt.shape[1]))
    z = u3
    for order in range(2):
        sa = _radix_fwd(z, 0, H_WIDTH, False, "hyena_radix_fwd")
        sb = _spectral(efwd, einv, sa.reshape(pairs, FFT_RADIX, 2 * m, H_WIDTH), kr, ki, order)
        z = _radix_inv(sb.reshape(pairs, FFT_RADIX, 2, m, H_WIDTH), u3, 1 + order, z, 0,
                       h_bias[order].reshape(1, H_WIDTH), "hyena_radix_inv")
    y_h = z.reshape(b, l, H_WIDTH)

    return _mixer_output(y, cols, y_h, x, gate, m_norm_w, final_norm_w,
                         w_branch_m.astype(BF16), w_branch_h.astype(BF16), w_out.astype(BF16))


def kernel(x, c, ctx, c_ctx, ada_w, ada_b, norm_w, w_in, m_conv_w, m_conv_b, m_dt_bias, m_a_log,
           m_d, m_norm_w, h_conv_w, h_conv_b, h_w1, h_b1, h_w2, h_b2, h_w3, h_b3, h_freq,
           h_w_out, h_bias, w_branch_m, w_branch_h, w_out, final_norm_w):
    assert ada_w.shape[0] == 1, "one trunk layer"
    return _layer(x, ctx, c, c_ctx, ada_w[0], ada_b[0], norm_w[0], w_in[0], m_conv_w[0],
                  m_conv_b[0], m_dt_bias[0], m_a_log[0], m_d[0], m_norm_w[0], h_conv_w[0],
                  h_conv_b[0], h_w1[0], h_b1[0], h_w2[0], h_b2[0], h_w3[0], h_b3[0], h_freq[0],
                  h_w_out[0], h_bias[0], w_branch_m[0], w_branch_h[0], w_out[0], final_norm_w)
```

```python
import functools
import math

import jax
import jax.numpy as jnp
from jax import lax
from jax.experimental import pallas as pl
from jax.experimental.pallas import tpu as pltpu

F32 = jnp.float32
BF16 = jnp.bfloat16
HIGHEST = lax.Precision.HIGHEST

LANES = 128
SUBLANES = 8
VMEM_LIMIT = 56 * 1024 * 1024

NORM_EPS = 1e-6
D_MODEL = 1024
M_HEADDIM = 64
M_HEADS = 16
M_GROUPS = 4
M_HPG = M_HEADS // M_GROUPS
M_STATE = 128
M_XBC = D_MODEL + 2 * M_GROUPS * M_STATE
M_CHUNK = 128
HEADDIM_SHIFT = M_HEADDIM.bit_length() - 1
CHUNK_SHIFT = M_CHUNK.bit_length() - 1
H_WIDTH = 1024
H_EMB = 33
H_HID = 64
H_DECAY_TARGET = 1e-2
H_FAST_DECAY_PCT = 0.3
H_SLOW_DECAY_PCT = 1.5
DT_PAD = LANES
FFT_RADIX = 32
ROW_CHUNK = 16

COL_XBC, COL_Z, COL_HGATE, COL_HPROJ, COL_GM, COL_GH = 0, 2, 3, 4, 7, 8
N_MAIN = 9 * 1024


def _dot(a, b, precision=None):
    return jnp.dot(a, b, preferred_element_type=F32, precision=precision)


def _dot_nt(a, b, precision=None):
    return lax.dot_general(a, b, (((1,), (1,)), ((), ())), preferred_element_type=F32,
                           precision=precision)


def _dot_tn(a, b, precision=None):
    return lax.dot_general(a, b, (((0,), (0,)), ((), ())), preferred_element_type=F32,
                           precision=precision)


def _params(*sem):
    return pltpu.CompilerParams(dimension_semantics=sem, vmem_limit_bytes=VMEM_LIMIT)


def _silu(v):
    return v * jax.nn.sigmoid(v)


def _adaln_kernel(c_ref, w_ref, b_ref, o_ref):
    o_ref[...] = _dot(_silu(c_ref[...]), w_ref[...], HIGHEST) + b_ref[...]


def _adaln(c_rows, ada_w, ada_b):
    rows, d = c_rows.shape
    n = ada_w.shape[1]
    tn = 1024
    return pl.pallas_call(
        _adaln_kernel,
        grid=(n // tn,),
        in_specs=[pl.BlockSpec((rows, d), lambda j: (0, 0)),
                  pl.BlockSpec((d, tn), lambda j: (0, j)),
                  pl.BlockSpec((1, tn), lambda j: (0, j))],
        out_specs=pl.BlockSpec((rows, tn), lambda j: (0, j)),
        out_shape=jax.ShapeDtypeStruct((rows, n), F32),
        compiler_params=_params("parallel"),
        name="adaln",
    )(c_rows, ada_w, ada_b.reshape(1, n))


def _split_bf16(v):
    hi = v.astype(BF16)
    return hi, (v - hi.astype(F32)).astype(BF16)


def _softplus(v):
    return jnp.maximum(v, 0.0) + jnp.log1p(jnp.exp(-jnp.abs(v)))


def _inproj_kernel(x_ref, shift_ref, scale_ref, nw_ref, w_ref, wdt_hi_ref, wdt_lo_ref, dtb_ref,
                   o_ref, dt_ref, h_scr):
    @pl.when(pl.program_id(2) == 0)
    def _():
        xt = x_ref[0]
        ms = jnp.mean(xt * xt, axis=-1, keepdims=True)
        hn = xt * lax.rsqrt(ms + NORM_EPS) * nw_ref[...]
        h = hn * (1.0 + scale_ref[0]) + shift_ref[0]
        hi, lo = _split_bf16(h)
        h_scr[...] = hi
        dt_raw = (_dot(hi, wdt_hi_ref[...]) + _dot(lo, wdt_hi_ref[...])
                  + _dot(hi, wdt_lo_ref[...]))
        dt_ref[0] = _softplus(dt_raw + dtb_ref[...])

    o_ref[0] = _dot(h_scr[...], w_ref[...]).astype(o_ref.dtype)


def _inproj(x, shift, scale, norm_w, w_main, w_dt, dt_bias, tm):
    b, l, d = x.shape
    n = w_main.shape[1]
    tn = 1024
    w_dt_hi, w_dt_lo = _split_bf16(w_dt)
    return pl.pallas_call(
        _inproj_kernel,
        grid=(b, l // tm, n // tn),
        in_specs=[pl.BlockSpec((1, tm, d), lambda bi, i, j: (bi, i, 0)),
                  pl.BlockSpec((1, 1, d), lambda bi, i, j: (bi, 0, 0)),
                  pl.BlockSpec((1, 1, d), lambda bi, i, j: (bi, 0, 0)),
                  pl.BlockSpec((1, d), lambda bi, i, j: (0, 0)),
                  pl.BlockSpec((d, tn), lambda bi, i, j: (0, j)),
                  pl.BlockSpec((d, DT_PAD), lambda bi, i, j: (0, 0)),
                  pl.BlockSpec((d, DT_PAD), lambda bi, i, j: (0, 0)),
                  pl.BlockSpec((1, DT_PAD), lambda bi, i, j: (0, 0))],
        out_specs=[pl.BlockSpec((1, tm, tn), lambda bi, i, j: (bi, i, j)),
                   pl.BlockSpec((1, tm, DT_PAD), lambda bi, i, j: (bi, i, 0))],
        out_shape=[jax.ShapeDtypeStruct((b, l, n), BF16),
                   jax.ShapeDtypeStruct((b, l, DT_PAD), F32)],
        scratch_shapes=[pltpu.VMEM((tm, d), BF16)],
        compiler_params=_params("parallel", "parallel", "arbitrary"),
        name="inproj",
    )(x, shift, scale, norm_w, w_main, w_dt_hi, w_dt_lo, dt_bias)


HALO = 16


def _dwconv_kernel(*refs, apply_silu, has_prefix):
    if has_prefix:
        cur_ref, prev_ref, next_ref, pre_ref, w_ref, b_ref, o_ref = refs
    else:
        cur_ref, prev_ref, next_ref, w_ref, b_ref, o_ref = refs
    r = pl.program_id(1)
    nr = pl.num_programs(1)
    first_main = 1 if has_prefix else 0
    u = cur_ref[0].astype(F32)
    if has_prefix:
        u = jnp.where(r == 0, pre_ref[0].astype(F32), u)
    rows = u.shape[0]
    has_prev = r > first_main
    has_next = jnp.logical_and(r >= first_main, r < nr - 1)
    prev_row = jnp.where(has_prev, prev_ref[0][HALO - 1:HALO, :].astype(F32), 0.0)
    next_row = jnp.where(has_next, next_ref[0][0:1, :].astype(F32), 0.0)
    ridx = lax.broadcasted_iota(jnp.int32, (rows, 1), 0)
    up = jnp.where(ridx == 0, prev_row, pltpu.roll(u, 1, 0))
    un = jnp.where(ridx == rows - 1, next_row, pltpu.roll(u, rows - 1, 0))
    w = w_ref[...]
    y = w[0:1, :] * up + w[1:2, :] * u + w[2:3, :] * un + b_ref[...]
    if apply_silu:
        y = _silu(y)
    o_ref[0] = y.astype(o_ref.dtype)


def _dwconv(src, col_off_blocks, conv_w, conv_b, apply_silu, name, prefix=None):
    b, l, _ = src.shape
    c = conv_w.shape[1]
    cb = 512
    rr = min(256, l)
    nh = rr // HALO
    last = l // HALO - 1
    off = col_off_blocks * (1024 // cb)
    npre = 0 if prefix is None else 1
    if prefix is not None:
        assert prefix.shape[1] == rr, "the prefix sequence must be exactly one row block"
    main = lambda r: jnp.maximum(r - npre, 0)
    in_specs = [pl.BlockSpec((1, rr, cb), lambda bi, r, ci: (bi, main(r), off + ci)),
                pl.BlockSpec((1, HALO, cb),
                             lambda bi, r, ci: (bi, jnp.maximum(main(r) * nh - 1, 0), off + ci)),
                pl.BlockSpec((1, HALO, cb),
                             lambda bi, r, ci: (bi, jnp.minimum((main(r) + 1) * nh, last), off + ci))]
    args = [src, src, src]
    if prefix is not None:
        in_specs.append(pl.BlockSpec((1, rr, cb), lambda bi, r, ci: (bi, 0, ci)))
        args.append(prefix)
    in_specs += [pl.BlockSpec((3, cb), lambda bi, r, ci: (0, ci)),
                 pl.BlockSpec((1, cb), lambda bi, r, ci: (0, ci))]
    return pl.pallas_call(
        functools.partial(_dwconv_kernel, apply_silu=apply_silu, has_prefix=prefix is not None),
        grid=(b, l // rr + npre, c // cb),
        in_specs=in_specs,
        out_specs=pl.BlockSpec((1, rr, cb), lambda bi, r, ci: (bi, r, ci)),
        out_shape=jax.ShapeDtypeStruct((b, l + npre * rr, c), BF16),
        compiler_params=_params("parallel", "parallel", "parallel"),
        name=name,
    )(*args, conv_w, conv_b.reshape(1, c))


def _expand_heads(v, e_bf16):
    hi = v.astype(BF16)
    lo = (v - hi.astype(F32)).astype(BF16)
    return _dot(hi, e_bf16) + _dot(lo, e_bf16)


def _cumsum_dot(tri_bf16, v):
    hi = v.astype(BF16)
    rest = v - hi.astype(F32)
    mid = rest.astype(BF16)
    lo = (rest - mid.astype(F32)).astype(BF16)
    return _dot(tri_bf16, hi) + _dot(tri_bf16, mid) + _dot(tri_bf16, lo)


def _ssd_chunk(fwd, xbc, dt_all, a_row, dskip_row, y_ref, state):
    q = M_CHUNK
    off = 0 if fwd else M_HEADS
    last = q - 1 if fwd else 0
    x = xbc[:, :D_MODEL].astype(F32)
    bmat = xbc[:, D_MODEL:D_MODEL + M_GROUPS * M_STATE]
    cmat = xbc[:, D_MODEL + M_GROUPS * M_STATE:]

    a_all = dt_all * a_row
    ri = lax.broadcasted_iota(jnp.int32, (q, q), 0)
    ci = lax.broadcasted_iota(jnp.int32, (q, q), 1)
    mask = (ri >= ci) if fwd else (ri <= ci)
    acs_all = _cumsum_dot(jnp.where(mask, 1.0, 0.0).astype(BF16), a_all)
    acs_t = acs_all.T[off:off + M_HEADS, :]
    acs = acs_all[:, off:off + M_HEADS]
    e_atot_c = jnp.exp(acs_t[:, last:last + 1])

    hid = lax.broadcasted_iota(jnp.int32, (M_HEADS, D_MODEL), 0)
    lid = lax.broadcasted_iota(jnp.int32, (M_HEADS, D_MODEL), 1)
    expand = jnp.where((lid >> HEADDIM_SHIFT) == hid, 1.0, 0.0).astype(BF16)
    per_head = jnp.concatenate([dt_all[:, off:off + M_HEADS],
                                jnp.exp(acs[last:last + 1, :] - acs),
                                jnp.exp(acs)], axis=0)
    wide = _expand_heads(per_head, expand)
    xdt = x * wide[:q]
    xw = (xdt * wide[q:2 * q]).astype(BF16)
    e_acs = wide[2 * q:]

    gw = M_HPG * M_HEADDIM
    rb = lax.broadcasted_iota(jnp.int32, (M_HPG * q, gw), 0) >> CHUNK_SHIFT
    cb = lax.broadcasted_iota(jnp.int32, (M_HPG * q, gw), 1) >> HEADDIM_SHIFT
    blockdiag = rb == cb
    for g in range(M_GROUPS):
        bg = bmat[:, g * M_STATE:(g + 1) * M_STATE]
        cg = cmat[:, g * M_STATE:(g + 1) * M_STATE]
        scores = _dot_nt(cg, bg)
        ms = []
        for r in range(M_HPG):
            h = g * M_HPG + r
            seg = acs[:, h:h + 1] - acs_t[h:h + 1, :]
            decay = jnp.exp(jnp.where(mask, seg, -jnp.inf))
            ms.append((scores * decay).astype(BF16))
        m_cat = jnp.concatenate(ms, axis=1)
        xg = xdt[:, g * gw:(g + 1) * gw]
        x_bd = jnp.where(blockdiag, jnp.concatenate([xg] * M_HPG, axis=0),
                         0.0).astype(BF16)
        y_diag = _dot(m_cat, x_bd)
        s_g = state[g * M_HPG:(g + 1) * M_HPG].reshape(gw, M_STATE).astype(BF16)
        y_g = y_diag + e_acs[:, g * gw:(g + 1) * gw] * _dot_nt(cg, s_g)
        if fwd:
            y_g = y_g + dskip_row[:, g * gw:(g + 1) * gw] * x[:, g * gw:(g + 1) * gw]
        y_ref[0, :, g * gw:(g + 1) * gw] = y_g.astype(y_ref.dtype)
        upd = _dot_tn(xw[:, g * gw:(g + 1) * gw], bg)
        for r in range(M_HPG):
            h = g * M_HPG + r
            state[h] = (state[h] * e_atot_c[h:h + 1, 0:1]
                        + upd[r * M_HEADDIM:(r + 1) * M_HEADDIM, :])


def _ssd_kernel(xf_ref, xb_ref, dtlf_ref, dtcf_ref, dtlb_ref, dtcb_ref, a_ref,
                dskip_ref, yf_ref, yb_ref, state_f, state_b, *, n_ctx_chunks):
    s = pl.program_id(1)
    is_ctx = s < n_ctx_chunks

    @pl.when(s == 0)
    def _():
        state_f[...] = jnp.zeros_like(state_f)
        state_b[...] = jnp.zeros_like(state_b)

    dt_f = jnp.where(is_ctx, dtcf_ref[0], dtlf_ref[0])
    dt_b = jnp.where(is_ctx, dtcb_ref[0], dtlb_ref[0])
    a_row = -jnp.exp(a_ref[...])
    _ssd_chunk(True, xf_ref[0], dt_f, a_row, dskip_ref[...], yf_ref, state_f)
    _ssd_chunk(False, xb_ref[0], dt_b, a_row, dskip_ref[...], yb_ref, state_b)


def _ssd(xbc, dt_l, dt_c, a_r, dskip_row):
    b, lt, cw = xbc.shape
    l, lc = dt_l.shape[1], dt_c.shape[1]
    q = M_CHUNK
    nl, nc = l // q, lc // q
    assert lt == l + lc

    lat_f = lambda si: jnp.maximum(si - nc, 0)
    lat_b = lambda si: nl - 1 - lat_f(si)
    ctx_f = lambda si: jnp.minimum(si, nc - 1)
    ctx_b = lambda si: nc - 1 - ctx_f(si)
    row_b = lambda si: jnp.where(si < nc, ctx_b(si), nc + lat_b(si))

    small = lambda shape: pl.BlockSpec(shape, lambda bi, si: (0, 0))
    dt_spec = lambda fn: pl.BlockSpec((1, q, DT_PAD), lambda bi, si: (bi, fn(si), 0))
    state = pltpu.VMEM((M_HEADS, M_HEADDIM, M_STATE), F32)
    return pl.pallas_call(
        functools.partial(_ssd_kernel, n_ctx_chunks=nc),
        grid=(b, nl + nc),
        in_specs=[pl.BlockSpec((1, q, cw), lambda bi, si: (bi, si, 0)),
                  pl.BlockSpec((1, q, cw), lambda bi, si: (bi, row_b(si), 0)),
                  dt_spec(lat_f), dt_spec(ctx_f), dt_spec(lat_b), dt_spec(ctx_b),
                  small((1, DT_PAD)), small((1, D_MODEL))],
        out_specs=[pl.BlockSpec((1, q, D_MODEL), lambda bi, si: (bi, lat_f(si), 0)),
                   pl.BlockSpec((1, q, D_MODEL), lambda bi, si: (bi, lat_b(si), 0))],
        out_shape=[jax.ShapeDtypeStruct((b, l, D_MODEL), BF16)] * 2,
        scratch_shapes=[state, state],
        compiler_params=_params("parallel", "arbitrary"),
        name="ssd",
    )(xbc, xbc, dt_l, dt_c, dt_l, dt_c, a_r, dskip_row)


def _filter_kernel(fl_ref, w1_ref, b1_ref, w2_ref, b2_ref, w3_ref, b3_ref, freq_ref, wo_ref,
                   deltas_ref, h_ref, *, seqlen):
    i = pl.program_id(0)
    tr = h_ref.shape[0]
    pos = (i * tr + lax.broadcasted_iota(jnp.int32, (tr, 1), 0))
    posf = pos.astype(F32)
    t = posf / F32(seqlen - 1)
    w = F32(2.0 * math.pi / seqlen) * posf
    ang = fl_ref[...] * w
    lane = lax.broadcasted_iota(jnp.int32, (tr, LANES), 1)
    bands = (H_EMB - 1) // 2
    feats = jnp.where(lane == 0, t,
                      jnp.where(lane <= bands, jnp.cos(ang),
                                jnp.where(lane < H_EMB, -jnp.sin(ang), 0.0)))
    freq = freq_ref[...]
    hid = jnp.sin(freq * (_dot(feats, w1_ref[...], HIGHEST) + b1_ref[...]))
    hid = jnp.sin(freq * (_dot(hid, w2_ref[...], HIGHEST) + b2_ref[...]))
    hid = jnp.sin(freq * (_dot(hid, w3_ref[...], HIGHEST) + b3_ref[...]))
    filt = _dot(hid, wo_ref[...], HIGHEST)
    window = jnp.exp(-t * deltas_ref[...])
    hw = H_WIDTH
    for blk in range(filt.shape[1] // hw):
        h_ref[:, blk * hw:(blk + 1) * hw] = filt[:, blk * hw:(blk + 1) * hw] * window


def _hyena_filters(seqlen, w1, b1, w2, b2, w3, b3, freq, w_out):
    tr = min(512, seqlen)
    bands = (H_EMB - 1) // 2
    f = jnp.linspace(1e-4, bands - 1, bands, dtype=F32)
    fl = jnp.zeros((1, LANES), F32).at[0, 1:1 + bands].set(f).at[0, 1 + bands:H_EMB].set(f)
    w1p = jnp.zeros((LANES, H_HID), F32).at[:H_EMB].set(w1)
    min_decay = math.log(H_DECAY_TARGET) / H_SLOW_DECAY_PCT
    max_decay = math.log(H_DECAY_TARGET) / H_FAST_DECAY_PCT
    deltas = jnp.abs(jnp.linspace(min_decay, max_decay, H_WIDTH, dtype=F32)).reshape(1, H_WIDTH)
    row = lambda v: v.reshape(1, -1)
    full = lambda a: pl.BlockSpec(a.shape, lambda i: (0, 0))
    args = (fl, w1p, row(b1), w2, row(b2), w3, row(b3), row(freq), w_out, deltas)
    nout = w_out.shape[1]
    return pl.pallas_call(
        functools.partial(_filter_kernel, seqlen=seqlen),
        grid=(seqlen // tr,),
        in_specs=[full(a) for a in args],
        out_specs=pl.BlockSpec((tr, nout), lambda i: (i, 0)),
        out_shape=jax.ShapeDtypeStruct((seqlen, nout), F32),
        compiler_params=_params("parallel"),
        name="hyena_filters",
    )(*args)


def _r_add(a, b):
    if a is None:
        return b
    if b is None:
        return a
    return a + b


def _r_sub(a, b):
    if b is None:
        return a
    if a is None:
        return -b
    return a - b


def _r_scale(a, s):
    if a is None or s == 0.0:
        return None
    if s == 1.0:
        return a
    if s == -1.0:
        return -a
    return a * s


def _c_mul_const(z, w):
    snap = lambda v: float(round(v)) if abs(v - round(v)) < 1e-12 else float(v)
    wr, wi = snap(w.real), snap(w.imag)
    zr, zi = z
    return (_r_sub(_r_scale(zr, wr), _r_scale(zi, wi)),
            _r_add(_r_scale(zr, wi), _r_scale(zi, wr)))


def _fft(xs, sign, first_half_only=False):
    n = len(xs)
    if n == 1:
        return list(xs)
    even, odd = _fft(xs[0::2], sign), _fft(xs[1::2], sign)
    out = [None] * n
    for k in range(n // 2):
        tr, ti = _c_mul_const(odd[k], complex(math.cos(2 * math.pi * k / n),
                                              sign * math.sin(2 * math.pi * k / n)))
        out[k] = (_r_add(even[k][0], tr), _r_add(even[k][1], ti))
        if not first_half_only:
            out[k + n // 2] = (_r_sub(even[k][0], tr), _r_sub(even[k][1], ti))
    return out[:n // 2] if first_half_only else out


def _tile_or_zero(v):
    return jnp.zeros((ROW_CHUNK, LANES), F32) if v is None else v


def _radix_fwd_kernel(z_ref, o_ref, *, real_input):
    _, _, half, rows_total, width = z_ref.shape

    def body(i, carry):
        rows = pl.ds(pl.multiple_of(i * ROW_CHUNK, ROW_CHUNK), ROW_CHUNK)
        for j in range(width // LANES):
            lanes = slice(j * LANES, (j + 1) * LANES)
            xs = []
            for t1 in range(half):
                re = z_ref[0, 0, t1, rows, lanes].astype(F32)
                im = None if real_input else z_ref[0, 1, t1, rows, lanes].astype(F32)
                xs.append((re, im))
            xs += [(None, None)] * half
            for k1, (re, im) in enumerate(_fft(xs, -1.0)):
                o_ref[0, k1, 0, rows, lanes] = _tile_or_zero(re).astype(BF16)
                o_ref[0, k1, 1, rows, lanes] = _tile_or_zero(im).astype(BF16)
        return carry

    lax.fori_loop(0, rows_total // ROW_CHUNK, body, 0)


def _radix_fwd(z, col_block, ncols, real_input, name):
    p, parts, half, m, _ = z.shape
    tt, cb = min(256, m), 256
    off = col_block * (H_WIDTH // cb)
    return pl.pallas_call(
        functools.partial(_radix_fwd_kernel, real_input=real_input),
        grid=(p, m // tt, ncols // cb),
        in_specs=[pl.BlockSpec((1, parts, half, tt, cb), lambda pi, ti, ci: (pi, 0, 0, ti, off + ci))],
        out_specs=pl.BlockSpec((1, 2 * half, 2, tt, cb), lambda pi, ti, ci: (pi, 0, 0, ti, ci)),
        out_shape=jax.ShapeDtypeStruct((p, 2 * half, 2, m, ncols), BF16),
        compiler_params=_params("parallel", "parallel", "parallel"),
        name=name,
    )(z)


def _radix_inv_kernel(sb_ref, gate_ref, zin_ref, bias_ref, o_ref):
    _, radix, _, rows_total, width = sb_ref.shape

    def body(i, carry):
        rows = pl.ds(pl.multiple_of(i * ROW_CHUNK, ROW_CHUNK), ROW_CHUNK)
        for j in range(width // LANES):
            lanes = slice(j * LANES, (j + 1) * LANES)
            ys = [(sb_ref[0, k1, 0, rows, lanes].astype(F32),
                   sb_ref[0, k1, 1, rows, lanes].astype(F32)) for k1 in range(radix)]
            bias = bias_ref[:, lanes]
            for t1, parts in enumerate(_fft(ys, 1.0, first_half_only=True)):
                for q in range(2):
                    zin = zin_ref[0, q, t1, rows, lanes].astype(F32)
                    gate = gate_ref[0, q, t1, rows, lanes].astype(F32)
                    o_ref[0, q, t1, rows, lanes] = (gate * (parts[q] + bias * zin)).astype(o_ref.dtype)
        return carry

    lax.fori_loop(0, rows_total // ROW_CHUNK, body, 0)


def _radix_inv(sb, gate_src, gate_block, zin_src, zin_block, bias_row, name):
    p, radix, _, m, ncols = sb.shape
    half = radix // 2
    tt, cb = min(256, m), 256
    goff, zoff = gate_block * (H_WIDTH // cb), zin_block * (H_WIDTH // cb)
    tok = lambda off: pl.BlockSpec((1, 2, half, tt, cb), lambda pi, ti, ci: (pi, 0, 0, ti, off + ci))
    return pl.pallas_call(
        _radix_inv_kernel,
        grid=(p, m // tt, ncols // cb),
        in_specs=[pl.BlockSpec((1, radix, 2, tt, cb), lambda pi, ti, ci: (pi, 0, 0, ti, ci)),
                  tok(goff), tok(zoff),
                  pl.BlockSpec((1, cb), lambda pi, ti, ci: (0, ci))],
        out_specs=tok(0),
        out_shape=jax.ShapeDtypeStruct((p, 2, half, m, ncols), BF16),
        compiler_params=_params("parallel", "parallel", "parallel"),
        name=name,
    )(sb, gate_src, zin_src, bias_row)


def _dft_seed_kernel(cb_ref, sb_ref, ca_ref, sa_ref, *, n):
    m = cb_ref.shape[0]
    r = lax.broadcasted_iota(jnp.int32, (m, m), 0)
    c = lax.broadcasted_iota(jnp.int32, (m, m), 1)
    beta = ((r * c) & (m - 1)).astype(F32) * F32(2.0 * math.pi / m)
    cb_ref[...] = jnp.cos(beta)
    sb_ref[...] = jnp.sin(beta)
    k1 = lax.broadcasted_iota(jnp.int32, ca_ref.shape, 0)
    t2 = lax.broadcasted_iota(jnp.int32, ca_ref.shape, 1)
    alpha = (k1 * t2).astype(F32) * F32(2.0 * math.pi / n)
    ca_ref[...] = jnp.cos(alpha)
    sa_ref[...] = jnp.sin(alpha)


def _dft_table_kernel(cb_ref, sb_ref, car_ref, sar_ref, cac_ref, sac_ref, efwd_ref, einv_ref):
    m = cb_ref.shape[0]
    cb, sb = cb_ref[...], sb_ref[...]
    car, sar = car_ref[0], sar_ref[0]
    cos_f = (car * cb - sar * sb).astype(BF16)
    sin_f = (sar * cb + car * sb).astype(BF16)
    efwd_ref[0, :m, :m] = cos_f
    efwd_ref[0, :m, m:] = sin_f
    efwd_ref[0, m:, :m] = -sin_f
    efwd_ref[0, m:, m:] = cos_f
    cac, sac = cac_ref[0], sac_ref[0]
    cos_i = (cac * cb - sac * sb).astype(BF16)
    sin_i = (sac * cb + cac * sb).astype(BF16)
    einv_ref[0, :m, :m] = cos_i
    einv_ref[0, :m, m:] = -sin_i
    einv_ref[0, m:, :m] = sin_i
    einv_ref[0, m:, m:] = cos_i


def _dft_tables(n):
    m = n // FFT_RADIX
    cb, sb, ca, sa = pl.pallas_call(
        functools.partial(_dft_seed_kernel, n=n),
        out_shape=[jax.ShapeDtypeStruct((m, m), F32)] * 2
        + [jax.ShapeDtypeStruct((FFT_RADIX, m), F32)] * 2,
        compiler_params=pltpu.CompilerParams(vmem_limit_bytes=VMEM_LIMIT),
        name="dft_seed",
    )()
    full = pl.BlockSpec((m, m), lambda i: (0, 0))
    rowspec = pl.BlockSpec((1, 1, m), lambda i: (i, 0, 0))
    colspec = pl.BlockSpec((1, m, 1), lambda i: (i, 0, 0))
    tab = pl.BlockSpec((1, 2 * m, 2 * m), lambda i: (i, 0, 0))
    return pl.pallas_call(
        _dft_table_kernel,
        grid=(FFT_RADIX,),
        in_specs=[full, full, rowspec, rowspec, colspec, colspec],
        out_specs=[tab, tab],
        out_shape=[jax.ShapeDtypeStruct((FFT_RADIX, 2 * m, 2 * m), BF16)] * 2,
        compiler_params=_params("parallel"),
        name="dft_tables",
    )(cb, sb, ca.reshape(FFT_RADIX, 1, m), sa.reshape(FFT_RADIX, 1, m),
      ca.reshape(FFT_RADIX, m, 1), sa.reshape(FFT_RADIX, m, 1))


def _filter_spectrum_kernel(efwd_ref, hf_ref, hb_ref, kr_ref, ki_ref, *, n):
    m = kr_ref.shape[1]
    xf = _dot(efwd_ref[0], hf_ref[0, 0])
    xb = _dot(efwd_ref[0], hb_ref[0, 0])
    kr_ref[0] = (xf[:m] + xb[:m]) * F32(1.0 / n)
    ki_ref[0] = (xf[m:] - xb[m:]) * F32(1.0 / n)


def _filter_spectrum(efwd, sah):
    radix, m2, _ = efwd.shape
    n = radix * m2 // 2
    orders = sah.shape[3] // (2 * H_WIDTH)
    slab = lambda off: pl.BlockSpec((1, 1, m2, H_WIDTH), lambda k1, o: (0, k1, 0, off + o))
    kspec = pl.BlockSpec((1, m2 // 2, H_WIDTH), lambda k1, o: (k1, 0, o))
    return pl.pallas_call(
        functools.partial(_filter_spectrum_kernel, n=n),
        grid=(radix, orders),
        in_specs=[pl.BlockSpec((1, m2, m2), lambda k1, o: (k1, 0, 0)), slab(0), slab(orders)],
        out_specs=[kspec, kspec],
        out_shape=[jax.ShapeDtypeStruct((radix, m2 // 2, orders * H_WIDTH), F32)] * 2,
        compiler_params=_params("parallel", "arbitrary"),
        name="filter_spectrum",
    )(efwd, sah, sah)


def _spectral_kernel(efwd_ref, einv_ref, sa_ref, kr_ref, ki_ref, sb_ref):
    m = kr_ref.shape[1]
    width = sa_ref.shape[3]
    cw = min(512, width)
    for j in range(width // cw):
        cols = slice(j * cw, (j + 1) * cw)
        x = _dot(efwd_ref[0], sa_ref[0, 0, :, cols])
        xr, xi = x[:m], x[m:]
        kr, ki = kr_ref[0, :, cols], ki_ref[0, :, cols]
        y = jnp.concatenate([xr * kr - xi * ki, xr * ki + xi * kr], axis=0).astype(BF16)
        sb_ref[0, 0, :, cols] = _dot(einv_ref[0], y).astype(BF16)


def _spectral(efwd, einv, sa, kr, ki, order):
    p, radix, m2, ncols = sa.shape
    slab = pl.BlockSpec((1, 1, m2, ncols), lambda k1, pi: (pi, k1, 0, 0))
    tab = pl.BlockSpec((1, m2, m2), lambda k1, pi: (k1, 0, 0))
    kspec = pl.BlockSpec((1, m2 // 2, ncols), lambda k1, pi: (k1, 0, order))
    return pl.pallas_call(
        _spectral_kernel,
        grid=(radix, p),
        in_specs=[tab, tab, slab, kspec, kspec],
        out_specs=slab,
        out_shape=jax.ShapeDtypeStruct(sa.shape, BF16),
        compiler_params=_params("parallel", "arbitrary"),
        name="hyena_spectral",
    )(efwd, einv, sa, kr, ki)


def _output_kernel(yf_ref, yb_ref, z_ref, hg_ref, gm_ref, gh_ref, yh_ref, x_ref, gate_ref, mnw_ref,
                   fnw_ref, wm_ref, wh_ref, wo_ref, o_ref):
    f32 = lambda ref: ref[0].astype(F32)
    g = (f32(yf_ref) + f32(yb_ref)) * _silu(f32(z_ref))
    gw = D_MODEL // M_GROUPS
    parts = []
    for i in range(M_GROUPS):
        gi = g[:, i * gw:(i + 1) * gw]
        parts.append(gi * lax.rsqrt(jnp.mean(gi * gi, axis=-1, keepdims=True) + NORM_EPS))
    ym = (jnp.concatenate(parts, axis=1) * mnw_ref[...]).astype(BF16)
    yh = (f32(yh_ref) * _silu(f32(hg_ref))).astype(BF16)
    merged = (jax.nn.sigmoid(f32(gm_ref)) * _dot(ym, wm_ref[...])
              + jax.nn.sigmoid(f32(gh_ref)) * _dot(yh, wh_ref[...]))
    out = _dot(merged.astype(BF16), wo_ref[...])
    xn = x_ref[0] + gate_ref[0] * out
    ms = jnp.mean(xn * xn, axis=-1, keepdims=True)
    o_ref[0] = xn * lax.rsqrt(ms + NORM_EPS) * fnw_ref[...]


def _mixer_output(y_f, y_b, cols, yh, x, gate, m_norm_w, final_norm_w, wm, wh, wo):
    b, l, d = x.shape
    tm = min(512, l)
    tok = lambda blk: pl.BlockSpec((1, tm, d), lambda bi, i: (bi, i, blk))
    wspec = pl.BlockSpec((d, d), lambda bi, i: (0, 0))
    rowspec = pl.BlockSpec((1, d), lambda bi, i: (0, 0))
    return pl.pallas_call(
        _output_kernel,
        grid=(b, l // tm),
        in_specs=[tok(0), tok(0),
                  tok(COL_Z), tok(COL_HGATE), tok(COL_GM), tok(COL_GH), tok(0), tok(0),
                  pl.BlockSpec((1, 1, d), lambda bi, i: (bi, 0, 0)),
                  rowspec, rowspec, wspec, wspec, wspec],
        out_specs=tok(0),
        out_shape=jax.ShapeDtypeStruct((b, l, d), F32),
        compiler_params=_params("parallel", "parallel"),
        name="mixer_output",
    )(y_f, y_b, cols, cols, cols, cols, yh, x, gate, m_norm_w.reshape(1, d),
      final_norm_w.reshape(1, d), wm, wh, wo)


def _layer(x, ctx, c, c_ctx, ada_w, ada_b, norm_w, w_in, m_conv_w, m_conv_b, m_dt_bias, m_a_log,
           m_d, m_norm_w, h_conv_w, h_conv_b, h_w1, h_b1, h_w2, h_b2, h_w3, h_b3, h_freq,
           h_w_out, h_bias, w_branch_m, w_branch_h, w_out, final_norm_w):
    b, l, d = x.shape
    lc = ctx.shape[1]
    ndt = 2 * M_HEADS

    pad_rows = -(b + 1) % SUBLANES
    c_rows = jnp.concatenate([c, c_ctx[None], jnp.zeros((pad_rows, d), F32)], axis=0)
    mod = _adaln(c_rows, ada_w, ada_b)
    shift, scale, gate = (mod[:b, i * d:(i + 1) * d].reshape(b, 1, d) for i in range(3))
    shift_c = jnp.broadcast_to(mod[b, :d], (b, 1, d))
    scale_c = jnp.broadcast_to(mod[b, d:2 * d], (b, 1, d))

    o_z, o_xbc, o_dt = 0, d, d + M_XBC
    o_hg = o_dt + ndt
    o_hp = o_hg + H_WIDTH
    o_gm = o_hp + 3 * H_WIDTH
    w_xbc = w_in[:, o_xbc:o_dt]
    w_main = jnp.concatenate([w_xbc, w_in[:, o_z:o_xbc], w_in[:, o_hg:]], axis=1).astype(BF16)
    w_dt_pad = jnp.zeros((d, DT_PAD), F32).at[:, :ndt].set(w_in[:, o_dt:o_hg])
    nw = norm_w.reshape(1, d)

    bias_r = jnp.zeros((1, DT_PAD), F32).at[0, :ndt].set(m_dt_bias.reshape(ndt))
    cols, dt_l = _inproj(x, shift, scale, nw, w_main, w_dt_pad, bias_r, min(1024, l))
    cols_c, dt_c = _inproj(ctx, shift_c, scale_c, nw, w_xbc.astype(BF16), w_dt_pad, bias_r,
                           min(1024, lc))

    xbc = _dwconv(cols, COL_XBC, m_conv_w, m_conv_b, True, "mamba_conv", prefix=cols_c)
    a_r = jnp.zeros((1, DT_PAD), F32).at[0, :ndt].set(m_a_log.reshape(ndt))
    dskip_row = jnp.repeat(m_d, M_HEADDIM).reshape(1, d)
    y_f, y_b = _ssd(xbc, dt_l, dt_c, a_r, dskip_row)

    assert b % 2 == 0, "batch rows are packed in pairs as complex signals"
    half = FFT_RADIX // 2
    m = l // half
    pairs = b // 2
    u3 = _dwconv(cols, COL_HPROJ, h_conv_w, h_conv_b, False, "hyena_conv3")
    u3 = u3.reshape(pairs, 2, half, m, 3 * H_WIDTH)
    filt = _hyena_filters(l, h_w1, h_b1, h_w2, h_b2, h_w3, h_b3, h_freq, h_w_out)
    efwd, einv = _dft_tables(2 * l)
    sah = _radix_fwd(filt.reshape(1, 1, half, m, filt.shape[1]), 0, filt.shape[1], True,
                     "filter_radix")
    kr, ki = _filter_spectrum(efwd, sah.reshape(1, FFT_RADIX, 2 * m, filt.shape[1]))
    z = u3
    for order in range(2):
        sa = _radix_fwd(z, 0, H_WIDTH, False, "hyena_radix_fwd")
        sb = _spectral(efwd, einv, sa.reshape(pairs, FFT_RADIX, 2 * m, H_WIDTH), kr, ki, order)
        z = _radix_inv(sb.reshape(pairs, FFT_RADIX, 2, m, H_WIDTH), u3, 1 + order, z, 0,
                       h_bias[order].reshape(1, H_WIDTH), "hyena_radix_inv")
    y_h = z.reshape(b, l, H_WIDTH)

    return _mixer_output(y_f, y_b, cols, y_h, x, gate, m_norm_w, final_norm_w,
                         w_branch_m.astype(BF16), w_branch_h.astype(BF16), w_out.astype(BF16))


def kernel(x, c, ctx, c_ctx, ada_w, ada_b, norm_w, w_in, m_conv_w, m_conv_b, m_dt_bias, m_a_log,
           m_d, m_norm_w, h_conv_w, h_conv_b, h_w1, h_b1, h_w2, h_b2, h_w3, h_b3, h_freq,
           h_w_out, h_bias, w_branch_m, w_branch_h, w_out, final_norm_w):
    assert ada_w.shape[0] == 1, "one trunk layer"
    return _layer(x, ctx, c, c_ctx, ada_w[0], ada_b[0], norm_w[0], w_in[0], m_conv_w[0],
                  m_conv_b[0], m_dt_bias[0], m_a_log[0], m_d[0], m_norm_w[0], h_conv_w[0],
                  h_conv_b[0], h_w1[0], h_b1[0], h_w2[0], h_b2[0], h_w3[0], h_b3[0], h_freq[0],
                  h_w_out[0], h_bias[0], w_branch_m[0], w_branch_h[0], w_out[0], final_norm_w)
```

```python
import functools
import math

import jax
import jax.numpy as jnp
from jax import lax
from jax.experimental import pallas as pl
from jax.experimental.pallas import tpu as pltpu

F32 = jnp.float32
BF16 = jnp.bfloat16
HIGHEST = lax.Precision.HIGHEST

LANES = 128
SUBLANES = 8
VMEM_LIMIT = 56 * 1024 * 1024

NORM_EPS = 1e-6
D_MODEL = 1024
M_HEADDIM = 64
M_HEADS = 16
M_GROUPS = 4
M_HPG = M_HEADS // M_GROUPS
M_STATE = 128
M_XBC = D_MODEL + 2 * M_GROUPS * M_STATE
M_CHUNK = 128
HEADDIM_SHIFT = M_HEADDIM.bit_length() - 1
CHUNK_SHIFT = M_CHUNK.bit_length() - 1
H_WIDTH = 1024
H_EMB = 33
H_HID = 64
H_DECAY_TARGET = 1e-2
H_FAST_DECAY_PCT = 0.3
H_SLOW_DECAY_PCT = 1.5
DT_PAD = LANES
FFT_RADIX = 32
ROW_CHUNK = 16
SPECTRAL_SLABS = 2

COL_XBC, COL_Z, COL_HGATE, COL_HPROJ, COL_GM, COL_GH = 0, 2, 3, 4, 7, 8
N_MAIN = 9 * 1024


def _dot(a, b, precision=None):
    return jnp.dot(a, b, preferred_element_type=F32, precision=precision)


def _dot_nt(a, b, precision=None):
    return lax.dot_general(a, b, (((1,), (1,)), ((), ())), preferred_element_type=F32,
                           precision=precision)


def _dot_tn(a, b, precision=None):
    return lax.dot_general(a, b, (((0,), (0,)), ((), ())), preferred_element_type=F32,
                           precision=precision)


def _params(*sem):
    return pltpu.CompilerParams(dimension_semantics=sem, vmem_limit_bytes=VMEM_LIMIT)


def _silu(v):
    return v * jax.nn.sigmoid(v)


def _adaln_kernel(c_ref, w_ref, b_ref, o_ref):
    o_ref[...] = _dot(_silu(c_ref[...]), w_ref[...], HIGHEST) + b_ref[...]


def _adaln(c_rows, ada_w, ada_b):
    rows, d = c_rows.shape
    n = ada_w.shape[1]
    tn = 1024
    return pl.pallas_call(
        _adaln_kernel,
        grid=(n // tn,),
        in_specs=[pl.BlockSpec((rows, d), lambda j: (0, 0)),
                  pl.BlockSpec((d, tn), lambda j: (0, j)),
                  pl.BlockSpec((1, tn), lambda j: (0, j))],
        out_specs=pl.BlockSpec((rows, tn), lambda j: (0, j)),
        out_shape=jax.ShapeDtypeStruct((rows, n), F32),
        compiler_params=_params("parallel"),
        name="adaln",
    )(c_rows, ada_w, ada_b.reshape(1, n))


def _split_bf16(v):
    hi = v.astype(BF16)
    return hi, (v - hi.astype(F32)).astype(BF16)


def _softplus(v):
    return jnp.maximum(v, 0.0) + jnp.log1p(jnp.exp(-jnp.abs(v)))


def _inproj_kernel(x_ref, shift_ref, scale_ref, nw_ref, w_ref, wdt_hi_ref, wdt_lo_ref, dtb_ref,
                   o_ref, dt_ref, h_scr):
    @pl.when(pl.program_id(2) == 0)
    def _():
        xt = x_ref[0]
        ms = jnp.mean(xt * xt, axis=-1, keepdims=True)
        hn = xt * lax.rsqrt(ms + NORM_EPS) * nw_ref[...]
        h = hn * (1.0 + scale_ref[0]) + shift_ref[0]
        hi, lo = _split_bf16(h)
        h_scr[...] = hi
        dt_raw = (_dot(hi, wdt_hi_ref[...]) + _dot(lo, wdt_hi_ref[...])
                  + _dot(hi, wdt_lo_ref[...]))
        dt_ref[0] = _softplus(dt_raw + dtb_ref[...])

    o_ref[0] = _dot(h_scr[...], w_ref[...]).astype(o_ref.dtype)


def _inproj(x, shift, scale, norm_w, w_main, w_dt, dt_bias, tm):
    b, l, d = x.shape
    n = w_main.shape[1]
    tn = max(t for t in range(LANES, INPROJ_MAX_TN + 1, LANES) if n % t == 0)
    w_dt_hi, w_dt_lo = _split_bf16(w_dt)
    return pl.pallas_call(
        _inproj_kernel,
        grid=(b, l // tm, n // tn),
        in_specs=[pl.BlockSpec((1, tm, d), lambda bi, i, j: (bi, i, 0)),
                  pl.BlockSpec((1, 1, d), lambda bi, i, j: (bi, 0, 0)),
                  pl.BlockSpec((1, 1, d), lambda bi, i, j: (bi, 0, 0)),
                  pl.BlockSpec((1, d), lambda bi, i, j: (0, 0)),
                  pl.BlockSpec((d, tn), lambda bi, i, j: (0, j)),
                  pl.BlockSpec((d, DT_PAD), lambda bi, i, j: (0, 0)),
                  pl.BlockSpec((d, DT_PAD), lambda bi, i, j: (0, 0)),
                  pl.BlockSpec((1, DT_PAD), lambda bi, i, j: (0, 0))],
        out_specs=[pl.BlockSpec((1, tm, tn), lambda bi, i, j: (bi, i, j)),
                   pl.BlockSpec((1, tm, DT_PAD), lambda bi, i, j: (bi, i, 0))],
        out_shape=[jax.ShapeDtypeStruct((b, l, n), BF16),
                   jax.ShapeDtypeStruct((b, l, DT_PAD), F32)],
        scratch_shapes=[pltpu.VMEM((tm, d), BF16)],
        compiler_params=_params("parallel", "parallel", "arbitrary"),
        name="inproj",
    )(x, shift, scale, norm_w, w_main, w_dt_hi, w_dt_lo, dt_bias)


INPROJ_MAX_TN = 2304
HALO = 16
CONV_SUB = 256


def _dwconv_kernel(*refs, apply_silu, has_prefix):
    if has_prefix:
        cur_ref, prev_ref, next_ref, pre_ref, w_ref, b_ref, o_ref = refs
    else:
        cur_ref, prev_ref, next_ref, w_ref, b_ref, o_ref = refs
    r = pl.program_id(1)
    nr = pl.num_programs(1)
    first_main = 1 if has_prefix else 0
    rows = cur_ref.shape[1]
    sub = min(CONV_SUB, rows)
    has_prev = r > first_main
    has_next = jnp.logical_and(r >= first_main, r < nr - 1)
    ri = lax.broadcasted_iota(jnp.int32, (sub, sub), 0)
    ci = lax.broadcasted_iota(jnp.int32, (sub, sub), 1)
    shift_down = jnp.where(ri == ci + 1, 1.0, 0.0).astype(BF16)
    shift_up = jnp.where(ri + 1 == ci, 1.0, 0.0).astype(BF16)
    slab = lax.broadcasted_iota(jnp.int32, (SUBLANES, 1), 0)
    w = w_ref[...]
    bias = b_ref[...]
    f32_row = lambda ref, k: ref[0, k:k + HALO, :][0:1, :].astype(F32)
    prev_row = jnp.where(has_prev, prev_ref[0][HALO - 1:HALO, :].astype(F32), 0.0)
    for k in range(rows // sub):
        ub = cur_ref[0, k * sub:(k + 1) * sub, :]
        if has_prefix:
            ub = jnp.where(r == 0, pre_ref[0, k * sub:(k + 1) * sub, :], ub)
        u = ub.astype(F32)
        if k + 1 < rows // sub:
            next_row = f32_row(cur_ref, (k + 1) * sub)
        else:
            next_row = jnp.where(has_next, next_ref[0][0:1, :].astype(F32), 0.0)
        up = _dot(shift_down, ub)
        un = _dot(shift_up, ub)
        up = jnp.concatenate([jnp.where(slab == 0, prev_row, up[:SUBLANES]), up[SUBLANES:]], axis=0)
        un = jnp.concatenate([un[:-SUBLANES],
                              jnp.where(slab == SUBLANES - 1, next_row, un[-SUBLANES:])], axis=0)
        y = w[0:1, :] * up + w[1:2, :] * u + w[2:3, :] * un + bias
        if apply_silu:
            y = _silu(y)
        o_ref[0, k * sub:(k + 1) * sub, :] = y.astype(o_ref.dtype)
        prev_row = u[sub - 1:sub, :]


def _dwconv(src, col_off_blocks, conv_w, conv_b, apply_silu, name, prefix=None):
    b, l, _ = src.shape
    c = conv_w.shape[1]
    cb = 1024
    npre = 0 if prefix is None else 1
    rr = min(512, l) if prefix is None else prefix.shape[1]
    assert l % rr == 0 and (prefix is None or rr <= CONV_SUB), "prefix must be one shift block"
    nh = rr // HALO
    last = l // HALO - 1
    off = col_off_blocks * (1024 // cb)
    main = lambda r: jnp.maximum(r - npre, 0)
    in_specs = [pl.BlockSpec((1, rr, cb), lambda bi, r, ci: (bi, main(r), off + ci)),
                pl.BlockSpec((1, HALO, cb),
                             lambda bi, r, ci: (bi, jnp.maximum(main(r) * nh - 1, 0), off + ci)),
                pl.BlockSpec((1, HALO, cb),
                             lambda bi, r, ci: (bi, jnp.minimum((main(r) + 1) * nh, last), off + ci))]
    args = [src, src, src]
    if prefix is not None:
        in_specs.append(pl.BlockSpec((1, rr, cb), lambda bi, r, ci: (bi, 0, ci)))
        args.append(prefix)
    in_specs += [pl.BlockSpec((3, cb), lambda bi, r, ci: (0, ci)),
                 pl.BlockSpec((1, cb), lambda bi, r, ci: (0, ci))]
    return pl.pallas_call(
        functools.partial(_dwconv_kernel, apply_silu=apply_silu, has_prefix=prefix is not None),
        grid=(b, l // rr + npre, c // cb),
        in_specs=in_specs,
        out_specs=pl.BlockSpec((1, rr, cb), lambda bi, r, ci: (bi, r, ci)),
        out_shape=jax.ShapeDtypeStruct((b, l + npre * rr, c), BF16),
        compiler_params=_params("parallel", "parallel", "parallel"),
        name=name,
    )(*args, conv_w, conv_b.reshape(1, c))


def _cumsum_dot(tri_bf16, v):
    hi = v.astype(BF16)
    rest = v - hi.astype(F32)
    mid = rest.astype(BF16)
    lo = (rest - mid.astype(F32)).astype(BF16)
    return _dot(tri_bf16, hi) + _dot(tri_bf16, mid) + _dot(tri_bf16, lo)


def _expand_heads(v, e_bf16):
    hi = v.astype(BF16)
    lo = (v - hi.astype(F32)).astype(BF16)
    return _dot(hi, e_bf16) + _dot(lo, e_bf16)


def _ssd_chunk(fwd, xbc, dt_all, a_row, dskip_row, y_ref, state):
    q = M_CHUNK
    off = 0 if fwd else M_HEADS
    last = q - 1 if fwd else 0
    x = xbc[:, :D_MODEL].astype(F32)
    bmat = xbc[:, D_MODEL:D_MODEL + M_GROUPS * M_STATE]
    cmat = xbc[:, D_MODEL + M_GROUPS * M_STATE:]

    a_all = dt_all * a_row
    ri = lax.broadcasted_iota(jnp.int32, (q, q), 0)
    ci = lax.broadcasted_iota(jnp.int32, (q, q), 1)
    mask = (ri >= ci) if fwd else (ri <= ci)
    acs_all = _cumsum_dot(jnp.where(mask, 1.0, 0.0).astype(BF16), a_all)
    acs_t = acs_all.T[off:off + M_HEADS, :]
    acs = acs_all[:, off:off + M_HEADS]
    e_atot_c = jnp.exp(acs_t[:, last:last + 1])

    hid = lax.broadcasted_iota(jnp.int32, (M_HEADS, D_MODEL), 0)
    lid = lax.broadcasted_iota(jnp.int32, (M_HEADS, D_MODEL), 1)
    expand = jnp.where((lid >> HEADDIM_SHIFT) == hid, 1.0, 0.0).astype(BF16)
    per_head = jnp.concatenate([dt_all[:, off:off + M_HEADS],
                                jnp.exp(acs[last:last + 1, :] - acs),
                                jnp.exp(acs)], axis=0)
    wide = _dot(per_head.astype(BF16), expand)
    xdt = x * wide[:q]
    xw = (xdt * wide[q:2 * q]).astype(BF16)
    e_acs = wide[2 * q:]

    gw = M_HPG * M_HEADDIM
    rb = lax.broadcasted_iota(jnp.int32, (M_HPG * q, gw), 0) >> CHUNK_SHIFT
    cb = lax.broadcasted_iota(jnp.int32, (M_HPG * q, gw), 1) >> HEADDIM_SHIFT
    blockdiag = rb == cb
    for g in range(M_GROUPS):
        bg = bmat[:, g * M_STATE:(g + 1) * M_STATE]
        cg = cmat[:, g * M_STATE:(g + 1) * M_STATE]
        scores = _dot_nt(cg, bg)
        ms = []
        for r in range(M_HPG):
            h = g * M_HPG + r
            seg = acs[:, h:h + 1] - acs_t[h:h + 1, :]
            decay = jnp.exp(jnp.where(mask, seg, -jnp.inf))
            ms.append((scores * decay).astype(BF16))
        m_cat = jnp.concatenate(ms, axis=1)
        xg = xdt[:, g * gw:(g + 1) * gw]
        x_bd = jnp.where(blockdiag, jnp.concatenate([xg] * M_HPG, axis=0),
                         0.0).astype(BF16)
        y_diag = _dot(m_cat, x_bd)
        s_g = state[g * M_HPG:(g + 1) * M_HPG].reshape(gw, M_STATE).astype(BF16)
        y_g = y_diag + e_acs[:, g * gw:(g + 1) * gw] * _dot_nt(cg, s_g)
        if fwd:
            y_g = y_g + dskip_row[:, g * gw:(g + 1) * gw] * x[:, g * gw:(g + 1) * gw]
        y_ref[0, :, g * gw:(g + 1) * gw] = y_g.astype(y_ref.dtype)
        upd = _dot_tn(xw[:, g * gw:(g + 1) * gw], bg)
        for r in range(M_HPG):
            h = g * M_HPG + r
            state[h] = (state[h] * e_atot_c[h:h + 1, 0:1]
                        + upd[r * M_HEADDIM:(r + 1) * M_HEADDIM, :])


def _ssd_kernel(xf_ref, xb_ref, dtlf_ref, dtcf_ref, dtlb_ref, dtcb_ref, a_ref,
                dskip_ref, yf_ref, yb_ref, state_f, state_b, *, n_ctx_chunks):
    s = pl.program_id(1)
    is_ctx = s < n_ctx_chunks

    @pl.when(s == 0)
    def _():
        state_f[...] = jnp.zeros_like(state_f)
        state_b[...] = jnp.zeros_like(state_b)

    dt_f = jnp.where(is_ctx, dtcf_ref[0], dtlf_ref[0])
    dt_b = jnp.where(is_ctx, dtcb_ref[0], dtlb_ref[0])
    a_row = -jnp.exp(a_ref[...])
    _ssd_chunk(True, xf_ref[0], dt_f, a_row, dskip_ref[...], yf_ref, state_f)
    _ssd_chunk(False, xb_ref[0], dt_b, a_row, dskip_ref[...], yb_ref, state_b)


def _ssd(xbc, dt_l, dt_c, a_r, dskip_row):
    b, lt, cw = xbc.shape
    l, lc = dt_l.shape[1], dt_c.shape[1]
    q = M_CHUNK
    nl, nc = l // q, lc // q
    assert lt == l + lc

    lat_f = lambda si: jnp.maximum(si - nc, 0)
    lat_b = lambda si: nl - 1 - lat_f(si)
    ctx_f = lambda si: jnp.minimum(si, nc - 1)
    ctx_b = lambda si: nc - 1 - ctx_f(si)
    row_b = lambda si: jnp.where(si < nc, ctx_b(si), nc + lat_b(si))

    small = lambda shape: pl.BlockSpec(shape, lambda bi, si: (0, 0))
    dt_spec = lambda fn: pl.BlockSpec((1, q, DT_PAD), lambda bi, si: (bi, fn(si), 0))
    state = pltpu.VMEM((M_HEADS, M_HEADDIM, M_STATE), F32)
    return pl.pallas_call(
        functools.partial(_ssd_kernel, n_ctx_chunks=nc),
        grid=(b, nl + nc),
        in_specs=[pl.BlockSpec((1, q, cw), lambda bi, si: (bi, si, 0)),
                  pl.BlockSpec((1, q, cw), lambda bi, si: (bi, row_b(si), 0)),
                  dt_spec(lat_f), dt_spec(ctx_f), dt_spec(lat_b), dt_spec(ctx_b),
                  small((1, DT_PAD)), small((1, D_MODEL))],
        out_specs=[pl.BlockSpec((1, q, D_MODEL), lambda bi, si: (bi, lat_f(si), 0)),
                   pl.BlockSpec((1, q, D_MODEL), lambda bi, si: (bi, lat_b(si), 0))],
        out_shape=[jax.ShapeDtypeStruct((b, l, D_MODEL), BF16)] * 2,
        scratch_shapes=[state, state],
        compiler_params=_params("parallel", "arbitrary"),
        name="ssd",
    )(xbc, xbc, dt_l, dt_c, dt_l, dt_c, a_r, dskip_row)


def _filter_kernel(fl_ref, w1_ref, b1_ref, w2_ref, b2_ref, w3_ref, b3_ref, freq_ref, wo_hi_ref,
                   wo_lo_ref, deltas_ref, h_ref, *, seqlen):
    i = pl.program_id(0)
    tr = h_ref.shape[0]
    pos = (i * tr + lax.broadcasted_iota(jnp.int32, (tr, 1), 0))
    posf = pos.astype(F32)
    t = posf / F32(seqlen - 1)
    w = F32(2.0 * math.pi / seqlen) * posf
    ang = fl_ref[...] * w
    lane = lax.broadcasted_iota(jnp.int32, (tr, LANES), 1)
    bands = (H_EMB - 1) // 2
    feats = jnp.where(lane == 0, t,
                      jnp.where(lane <= bands, jnp.cos(ang),
                                jnp.where(lane < H_EMB, -jnp.sin(ang), 0.0)))
    freq = freq_ref[...]
    hid = jnp.sin(freq * (_dot(feats, w1_ref[...], HIGHEST) + b1_ref[...]))
    hid = jnp.sin(freq * (_dot(hid, w2_ref[...], HIGHEST) + b2_ref[...]))
    hid = jnp.sin(freq * (_dot(hid, w3_ref[...], HIGHEST) + b3_ref[...]))
    hid_hi, hid_lo = _split_bf16(hid)
    filt = (_dot(hid_hi, wo_hi_ref[...]) + _dot(hid_lo, wo_hi_ref[...])
            + _dot(hid_hi, wo_lo_ref[...]))
    window = jnp.exp(-t * deltas_ref[...])
    hw = H_WIDTH
    for blk in range(filt.shape[1] // hw):
        h_ref[:, blk * hw:(blk + 1) * hw] = filt[:, blk * hw:(blk + 1) * hw] * window


def _hyena_filters(seqlen, w1, b1, w2, b2, w3, b3, freq, w_out):
    tr = min(512, seqlen)
    bands = (H_EMB - 1) // 2
    f = jnp.linspace(1e-4, bands - 1, bands, dtype=F32)
    fl = jnp.zeros((1, LANES), F32).at[0, 1:1 + bands].set(f).at[0, 1 + bands:H_EMB].set(f)
    w1p = jnp.zeros((LANES, H_HID), F32).at[:H_EMB].set(w1)
    min_decay = math.log(H_DECAY_TARGET) / H_SLOW_DECAY_PCT
    max_decay = math.log(H_DECAY_TARGET) / H_FAST_DECAY_PCT
    deltas = jnp.abs(jnp.linspace(min_decay, max_decay, H_WIDTH, dtype=F32)).reshape(1, H_WIDTH)
    row = lambda v: v.reshape(1, -1)
    full = lambda a: pl.BlockSpec(a.shape, lambda i: (0, 0))
    args = (fl, w1p, row(b1), w2, row(b2), w3, row(b3), row(freq), *_split_bf16(w_out), deltas)
    nout = w_out.shape[1]
    return pl.pallas_call(
        functools.partial(_filter_kernel, seqlen=seqlen),
        grid=(seqlen // tr,),
        in_specs=[full(a) for a in args],
        out_specs=pl.BlockSpec((tr, nout), lambda i: (i, 0)),
        out_shape=jax.ShapeDtypeStruct((seqlen, nout), F32),
        compiler_params=_params("parallel"),
        name="hyena_filters",
    )(*args)


def _r_add(a, b):
    if a is None:
        return b
    if b is None:
        return a
    return a + b


def _r_sub(a, b):
    if b is None:
        return a
    if a is None:
        return -b
    return a - b


def _r_scale(a, s):
    if a is None or s == 0.0:
        return None
    if s == 1.0:
        return a
    if s == -1.0:
        return -a
    return a * s


def _c_mul_const(z, w):
    snap = lambda v: float(round(v)) if abs(v - round(v)) < 1e-12 else float(v)
    wr, wi = snap(w.real), snap(w.imag)
    zr, zi = z
    return (_r_sub(_r_scale(zr, wr), _r_scale(zi, wi)),
            _r_add(_r_scale(zr, wi), _r_scale(zi, wr)))


def _fft(xs, sign, first_half_only=False):
    n = len(xs)
    if n == 1:
        return list(xs)
    even, odd = _fft(xs[0::2], sign), _fft(xs[1::2], sign)
    out = [None] * n
    for k in range(n // 2):
        tr, ti = _c_mul_const(odd[k], complex(math.cos(2 * math.pi * k / n),
                                              sign * math.sin(2 * math.pi * k / n)))
        out[k] = (_r_add(even[k][0], tr), _r_add(even[k][1], ti))
        if not first_half_only:
            out[k + n // 2] = (_r_sub(even[k][0], tr), _r_sub(even[k][1], ti))
    return out[:n // 2] if first_half_only else out


def _tile_or_zero(v):
    return jnp.zeros((ROW_CHUNK, LANES), F32) if v is None else v


def _radix_fwd_kernel(z_ref, o_ref, *, real_input):
    _, _, half, rows_total, width = z_ref.shape

    def body(i, carry):
        rows = pl.ds(pl.multiple_of(i * ROW_CHUNK, ROW_CHUNK), ROW_CHUNK)
        for j in range(width // LANES):
            lanes = slice(j * LANES, (j + 1) * LANES)
            xs = []
            for t1 in range(half):
                re = z_ref[0, 0, t1, rows, lanes].astype(F32)
                im = None if real_input else z_ref[0, 1, t1, rows, lanes].astype(F32)
                xs.append((re, im))
            xs += [(None, None)] * half
            for k1, (re, im) in enumerate(_fft(xs, -1.0)):
                o_ref[0, k1, 0, rows, lanes] = _tile_or_zero(re).astype(BF16)
                o_ref[0, k1, 1, rows, lanes] = _tile_or_zero(im).astype(BF16)
        return carry

    lax.fori_loop(0, rows_total // ROW_CHUNK, body, 0)


def _radix_fwd(z, col_block, ncols, real_input, name):
    p, parts, half, m, _ = z.shape
    tt, cb = min(256, m), 256
    off = col_block * (H_WIDTH // cb)
    return pl.pallas_call(
        functools.partial(_radix_fwd_kernel, real_input=real_input),
        grid=(p, m // tt, ncols // cb),
        in_specs=[pl.BlockSpec((1, parts, half, tt, cb), lambda pi, ti, ci: (pi, 0, 0, ti, off + ci))],
        out_specs=pl.BlockSpec((1, 2 * half, 2, tt, cb), lambda pi, ti, ci: (pi, 0, 0, ti, ci)),
        out_shape=jax.ShapeDtypeStruct((p, 2 * half, 2, m, ncols), BF16),
        compiler_params=_params("parallel", "parallel", "parallel"),
        name=name,
    )(z)


def _radix_inv_kernel(sb_ref, gate_ref, zin_ref, bias_ref, o_ref):
    _, radix, _, rows_total, width = sb_ref.shape

    def body(i, carry):
        rows = pl.ds(pl.multiple_of(i * ROW_CHUNK, ROW_CHUNK), ROW_CHUNK)
        for j in range(width // LANES):
            lanes = slice(j * LANES, (j + 1) * LANES)
            ys = [(sb_ref[0, k1, 0, rows, lanes].astype(F32),
                   sb_ref[0, k1, 1, rows, lanes].astype(F32)) for k1 in range(radix)]
            bias = bias_ref[:, lanes]
            for t1, parts in enumerate(_fft(ys, 1.0, first_half_only=True)):
                for q in range(2):
                    zin = zin_ref[0, q, t1, rows, lanes].astype(F32)
                    gate = gate_ref[0, q, t1, rows, lanes].astype(F32)
                    o_ref[0, q, t1, rows, lanes] = (gate * (parts[q] + bias * zin)).astype(o_ref.dtype)
        return carry

    lax.fori_loop(0, rows_total // ROW_CHUNK, body, 0)


def _radix_inv(sb, gate_src, gate_block, zin_src, zin_block, bias_row, name):
    p, radix, _, m, ncols = sb.shape
    half = radix // 2
    tt, cb = min(256, m), 256
    goff, zoff = gate_block * (H_WIDTH // cb), zin_block * (H_WIDTH // cb)
    tok = lambda off: pl.BlockSpec((1, 2, half, tt, cb), lambda pi, ti, ci: (pi, 0, 0, ti, off + ci))
    return pl.pallas_call(
        _radix_inv_kernel,
        grid=(p, m // tt, ncols // cb),
        in_specs=[pl.BlockSpec((1, radix, 2, tt, cb), lambda pi, ti, ci: (pi, 0, 0, ti, ci)),
                  tok(goff), tok(zoff),
                  pl.BlockSpec((1, cb), lambda pi, ti, ci: (0, ci))],
        out_specs=tok(0),
        out_shape=jax.ShapeDtypeStruct((p, 2, half, m, ncols), BF16),
        compiler_params=_params("parallel", "parallel", "parallel"),
        name=name,
    )(sb, gate_src, zin_src, bias_row)


def _dft_seed_kernel(cb_ref, sb_ref, ca_ref, sa_ref, *, n):
    m = cb_ref.shape[0]
    r = lax.broadcasted_iota(jnp.int32, (m, m), 0)
    c = lax.broadcasted_iota(jnp.int32, (m, m), 1)
    beta = ((r * c) & (m - 1)).astype(F32) * F32(2.0 * math.pi / m)
    cb_ref[...] = jnp.cos(beta)
    sb_ref[...] = jnp.sin(beta)
    k1 = lax.broadcasted_iota(jnp.int32, ca_ref.shape, 0)
    t2 = lax.broadcasted_iota(jnp.int32, ca_ref.shape, 1)
    alpha = (k1 * t2).astype(F32) * F32(2.0 * math.pi / n)
    ca_ref[...] = jnp.cos(alpha)
    sa_ref[...] = jnp.sin(alpha)


def _dft_table_kernel(cb_ref, sb_ref, car_ref, sar_ref, cac_ref, sac_ref, efwd_ref, einv_ref):
    m = cb_ref.shape[0]
    cb, sb = cb_ref[...], sb_ref[...]
    car, sar = car_ref[0], sar_ref[0]
    cos_f = (car * cb - sar * sb).astype(BF16)
    sin_f = (sar * cb + car * sb).astype(BF16)
    efwd_ref[0, :m, :m] = cos_f
    efwd_ref[0, :m, m:] = sin_f
    efwd_ref[0, m:, :m] = -sin_f
    efwd_ref[0, m:, m:] = cos_f
    cac, sac = cac_ref[0], sac_ref[0]
    cos_i = (cac * cb - sac * sb).astype(BF16)
    sin_i = (sac * cb + cac * sb).astype(BF16)
    einv_ref[0, :m, :m] = cos_i
    einv_ref[0, :m, m:] = -sin_i
    einv_ref[0, m:, :m] = sin_i
    einv_ref[0, m:, m:] = cos_i


def _dft_tables(n):
    m = n // FFT_RADIX
    cb, sb, ca, sa = pl.pallas_call(
        functools.partial(_dft_seed_kernel, n=n),
        out_shape=[jax.ShapeDtypeStruct((m, m), F32)] * 2
        + [jax.ShapeDtypeStruct((FFT_RADIX, m), F32)] * 2,
        compiler_params=pltpu.CompilerParams(vmem_limit_bytes=VMEM_LIMIT),
        name="dft_seed",
    )()
    full = pl.BlockSpec((m, m), lambda i: (0, 0))
    rowspec = pl.BlockSpec((1, 1, m), lambda i: (i, 0, 0))
    colspec = pl.BlockSpec((1, m, 1), lambda i: (i, 0, 0))
    tab = pl.BlockSpec((1, 2 * m, 2 * m), lambda i: (i, 0, 0))
    return pl.pallas_call(
        _dft_table_kernel,
        grid=(FFT_RADIX,),
        in_specs=[full, full, rowspec, rowspec, colspec, colspec],
        out_specs=[tab, tab],
        out_shape=[jax.ShapeDtypeStruct((FFT_RADIX, 2 * m, 2 * m), BF16)] * 2,
        compiler_params=_params("parallel"),
        name="dft_tables",
    )(cb, sb, ca.reshape(FFT_RADIX, 1, m), sa.reshape(FFT_RADIX, 1, m),
      ca.reshape(FFT_RADIX, m, 1), sa.reshape(FFT_RADIX, m, 1))


def _filter_spectrum_kernel(efwd_ref, hf_ref, hb_ref, kr_ref, ki_ref, *, n):
    m = kr_ref.shape[1]
    xf = _dot(efwd_ref[0], hf_ref[0, 0])
    xb = _dot(efwd_ref[0], hb_ref[0, 0])
    kr_ref[0] = (xf[:m] + xb[:m]) * F32(1.0 / n)
    ki_ref[0] = (xf[m:] - xb[m:]) * F32(1.0 / n)


def _filter_spectrum(efwd, sah):
    radix, m2, _ = efwd.shape
    n = radix * m2 // 2
    orders = sah.shape[3] // (2 * H_WIDTH)
    slab = lambda off: pl.BlockSpec((1, 1, m2, H_WIDTH), lambda k1, o: (0, k1, 0, off + o))
    kspec = pl.BlockSpec((1, m2 // 2, H_WIDTH), lambda k1, o: (k1, 0, o))
    return pl.pallas_call(
        functools.partial(_filter_spectrum_kernel, n=n),
        grid=(radix, orders),
        in_specs=[pl.BlockSpec((1, m2, m2), lambda k1, o: (k1, 0, 0)), slab(0), slab(orders)],
        out_specs=[kspec, kspec],
        out_shape=[jax.ShapeDtypeStruct((radix, m2 // 2, orders * H_WIDTH), F32)] * 2,
        compiler_params=_params("parallel", "arbitrary"),
        name="filter_spectrum",
    )(efwd, sah, sah)


def _spectral_kernel(efwd_ref, einv_ref, sa_ref, kr_ref, ki_ref, sb_ref):
    slabs, m = kr_ref.shape[0], kr_ref.shape[1]
    width = sa_ref.shape[3]
    cw = min(512, width)
    for s in range(slabs):
        for j in range(width // cw):
            cols = slice(j * cw, (j + 1) * cw)
            x = _dot(efwd_ref[s], sa_ref[0, s, :, cols])
            xr, xi = x[:m], x[m:]
            kr, ki = kr_ref[s, :, cols], ki_ref[s, :, cols]
            y = jnp.concatenate([xr * kr - xi * ki, xr * ki + xi * kr], axis=0).astype(BF16)
            sb_ref[0, s, :, cols] = _dot(einv_ref[s], y).astype(BF16)


def _spectral(efwd, einv, sa, kr, ki, order):
    p, radix, m2, ncols = sa.shape
    ns = SPECTRAL_SLABS
    slab = pl.BlockSpec((1, ns, m2, ncols), lambda k1, pi: (pi, k1, 0, 0))
    tab = pl.BlockSpec((ns, m2, m2), lambda k1, pi: (k1, 0, 0))
    kspec = pl.BlockSpec((ns, m2 // 2, ncols), lambda k1, pi: (k1, 0, order))
    return pl.pallas_call(
        _spectral_kernel,
        grid=(radix // ns, p),
        in_specs=[tab, tab, slab, kspec, kspec],
        out_specs=slab,
        out_shape=jax.ShapeDtypeStruct(sa.shape, BF16),
        compiler_params=_params("parallel", "arbitrary"),
        name="hyena_spectral",
    )(efwd, einv, sa, kr, ki)


def _output_kernel(yf_ref, yb_ref, z_ref, hg_ref, gm_ref, gh_ref, yh_ref, x_ref, gate_ref, mnw_ref,
                   fnw_ref, wm_ref, wh_ref, wo_ref, o_ref):
    f32 = lambda ref: ref[0].astype(F32)
    g = (f32(yf_ref) + f32(yb_ref)) * _silu(f32(z_ref))
    gw = D_MODEL // M_GROUPS
    parts = []
    for i in range(M_GROUPS):
        gi = g[:, i * gw:(i + 1) * gw]
        parts.append(gi * lax.rsqrt(jnp.mean(gi * gi, axis=-1, keepdims=True) + NORM_EPS))
    ym = (jnp.concatenate(parts, axis=1) * mnw_ref[...]).astype(BF16)
    yh = (f32(yh_ref) * _silu(f32(hg_ref))).astype(BF16)
    merged = (jax.nn.sigmoid(f32(gm_ref)) * _dot(ym, wm_ref[...])
              + jax.nn.sigmoid(f32(gh_ref)) * _dot(yh, wh_ref[...]))
    out = _dot(merged.astype(BF16), wo_ref[...])
    xn = x_ref[0] + gate_ref[0] * out
    ms = jnp.mean(xn * xn, axis=-1, keepdims=True)
    o_ref[0] = xn * lax.rsqrt(ms + NORM_EPS) * fnw_ref[...]


def _mixer_output(y_f, y_b, cols, yh, x, gate, m_norm_w, final_norm_w, wm, wh, wo):
    b, l, d = x.shape
    tm = min(512, l)
    tok = lambda blk: pl.BlockSpec((1, tm, d), lambda bi, i: (bi, i, blk))
    wspec = pl.BlockSpec((d, d), lambda bi, i: (0, 0))
    rowspec = pl.BlockSpec((1, d), lambda bi, i: (0, 0))
    return pl.pallas_call(
        _output_kernel,
        grid=(b, l // tm),
        in_specs=[tok(0), tok(0),
                  tok(COL_Z), tok(COL_HGATE), tok(COL_GM), tok(COL_GH), tok(0), tok(0),
                  pl.BlockSpec((1, 1, d), lambda bi, i: (bi, 0, 0)),
                  rowspec, rowspec, wspec, wspec, wspec],
        out_specs=tok(0),
        out_shape=jax.ShapeDtypeStruct((b, l, d), F32),
        compiler_params=_params("parallel", "parallel"),
        name="mixer_output",
    )(y_f, y_b, cols, cols, cols, cols, yh, x, gate, m_norm_w.reshape(1, d),
      final_norm_w.reshape(1, d), wm, wh, wo)


def _layer(x, ctx, c, c_ctx, ada_w, ada_b, norm_w, w_in, m_conv_w, m_conv_b, m_dt_bias, m_a_log,
           m_d, m_norm_w, h_conv_w, h_conv_b, h_w1, h_b1, h_w2, h_b2, h_w3, h_b3, h_freq,
           h_w_out, h_bias, w_branch_m, w_branch_h, w_out, final_norm_w):
    b, l, d = x.shape
    lc = ctx.shape[1]
    ndt = 2 * M_HEADS

    pad_rows = -(b + 1) % SUBLANES
    c_rows = jnp.concatenate([c, c_ctx[None], jnp.zeros((pad_rows, d), F32)], axis=0)
    mod = _adaln(c_rows, ada_w, ada_b)
    shift, scale, gate = (mod[:b, i * d:(i + 1) * d].reshape(b, 1, d) for i in range(3))
    shift_c = jnp.broadcast_to(mod[b, :d], (b, 1, d))
    scale_c = jnp.broadcast_to(mod[b, d:2 * d], (b, 1, d))

    o_z, o_xbc, o_dt = 0, d, d + M_XBC
    o_hg = o_dt + ndt
    o_hp = o_hg + H_WIDTH
    o_gm = o_hp + 3 * H_WIDTH
    w_xbc = w_in[:, o_xbc:o_dt]
    w_main = jnp.concatenate([w_xbc, w_in[:, o_z:o_xbc], w_in[:, o_hg:]], axis=1).astype(BF16)
    w_dt_pad = jnp.zeros((d, DT_PAD), F32).at[:, :ndt].set(w_in[:, o_dt:o_hg])
    nw = norm_w.reshape(1, d)

    bias_r = jnp.zeros((1, DT_PAD), F32).at[0, :ndt].set(m_dt_bias.reshape(ndt))
    cols, dt_l = _inproj(x, shift, scale, nw, w_main, w_dt_pad, bias_r, min(1024, l))
    cols_c, dt_c = _inproj(ctx, shift_c, scale_c, nw, w_xbc.astype(BF16), w_dt_pad, bias_r,
                           min(1024, lc))

    xbc = _dwconv(cols, COL_XBC, m_conv_w, m_conv_b, True, "mamba_conv", prefix=cols_c)
    a_r = jnp.zeros((1, DT_PAD), F32).at[0, :ndt].set(m_a_log.reshape(ndt))
    dskip_row = jnp.repeat(m_d, M_HEADDIM).reshape(1, d)
    y_f, y_b = _ssd(xbc, dt_l, dt_c, a_r, dskip_row)

    assert b % 2 == 0, "batch rows are packed in pairs as complex signals"
    half = FFT_RADIX // 2
    m = l // half
    pairs = b // 2
    u3 = _dwconv(cols, COL_HPROJ, h_conv_w, h_conv_b, False, "hyena_conv3")
    u3 = u3.reshape(pairs, 2, half, m, 3 * H_WIDTH)
    filt = _hyena_filters(l, h_w1, h_b1, h_w2, h_b2, h_w3, h_b3, h_freq, h_w_out)
    efwd, einv = _dft_tables(2 * l)
    sah = _radix_fwd(filt.reshape(1, 1, half, m, filt.shape[1]), 0, filt.shape[1], True,
                     "filter_radix")
    kr, ki = _filter_spectrum(efwd, sah.reshape(1, FFT_RADIX, 2 * m, filt.shape[1]))
    z = u3
    for order in range(2):
        sa = _radix_fwd(z, 0, H_WIDTH, False, "hyena_radix_fwd")
        sb = _spectral(efwd, einv, sa.reshape(pairs, FFT_RADIX, 2 * m, H_WIDTH), kr, ki, order)
        z = _radix_inv(sb.reshape(pairs, FFT_RADIX, 2, m, H_WIDTH), u3, 1 + order, z, 0,
                       h_bias[order].reshape(1, H_WIDTH), "hyena_radix_inv")
    y_h = z.reshape(b, l, H_WIDTH)

    return _mixer_output(y_f, y_b, cols, y_h, x, gate, m_norm_w, final_norm_w,
                         w_branch_m.astype(BF16), w_branch_h.astype(BF16), w_out.astype(BF16))


def kernel(x, c, ctx, c_ctx, ada_w, ada_b, norm_w, w_in, m_conv_w, m_conv_b, m_dt_bias, m_a_log,
           m_d, m_norm_w, h_conv_w, h_conv_b, h_w1, h_b1, h_w2, h_b2, h_w3, h_b3, h_freq,
           h_w_out, h_bias, w_branch_m, w_branch_h, w_out, final_norm_w):
    assert ada_w.shape[0] == 1, "one trunk layer"
    return _layer(x, ctx, c, c_ctx, ada_w[0], ada_b[0], norm_w[0], w_in[0], m_conv_w[0],
                  m_conv_b[0], m_dt_bias[0], m_a_log[0], m_d[0], m_norm_w[0], h_conv_w[0],
                  h_conv_b[0], h_w1[0], h_b1[0], h_w2[0], h_b2[0], h_w3[0], h_b3[0], h_freq[0],
                  h_w_out[0], h_bias[0], w_branch_m[0], w_branch_h[0], w_out[0], final_norm_w)
```

```python
import functools
import math

import jax
import jax.numpy as jnp
from jax import lax
from jax.experimental import pallas as pl
from jax.experimental.pallas import tpu as pltpu

F32 = jnp.float32
BF16 = jnp.bfloat16
HIGHEST = lax.Precision.HIGHEST

LANES = 128
SUBLANES = 8
VMEM_LIMIT = 56 * 1024 * 1024

NORM_EPS = 1e-6
D_MODEL = 1024
M_HEADDIM = 64
M_HEADS = 16
M_GROUPS = 4
M_HPG = M_HEADS // M_GROUPS
M_STATE = 128
M_XBC = D_MODEL + 2 * M_GROUPS * M_STATE
M_CHUNK = 128
HEADDIM_SHIFT = M_HEADDIM.bit_length() - 1
CHUNK_SHIFT = M_CHUNK.bit_length() - 1
H_WIDTH = 1024
H_EMB = 33
H_HID = 64
H_DECAY_TARGET = 1e-2
H_FAST_DECAY_PCT = 0.3
H_SLOW_DECAY_PCT = 1.5
DT_PAD = LANES
FFT_RADIX = 32
ROW_CHUNK = 16
SPECTRAL_SLABS = 2

COL_XBC, COL_Z, COL_HGATE, COL_HPROJ, COL_GM, COL_GH = 0, 2, 3, 4, 7, 8
N_MAIN = 9 * 1024


def _dot(a, b, precision=None):
    return jnp.dot(a, b, preferred_element_type=F32, precision=precision)


def _dot_nt(a, b, precision=None):
    return lax.dot_general(a, b, (((1,), (1,)), ((), ())), preferred_element_type=F32,
                           precision=precision)


def _dot_tn(a, b, precision=None):
    return lax.dot_general(a, b, (((0,), (0,)), ((), ())), preferred_element_type=F32,
                           precision=precision)


def _params(*sem):
    return pltpu.CompilerParams(dimension_semantics=sem, vmem_limit_bytes=VMEM_LIMIT)


def _silu(v):
    return v * jax.nn.sigmoid(v)


def _adaln_kernel(c_ref, w_ref, b_ref, o_ref):
    o_ref[...] = _dot(_silu(c_ref[...]), w_ref[...], HIGHEST) + b_ref[...]


def _adaln(c_rows, ada_w, ada_b):
    rows, d = c_rows.shape
    n = ada_w.shape[1]
    tn = 1024
    return pl.pallas_call(
        _adaln_kernel,
        grid=(n // tn,),
        in_specs=[pl.BlockSpec((rows, d), lambda j: (0, 0)),
                  pl.BlockSpec((d, tn), lambda j: (0, j)),
                  pl.BlockSpec((1, tn), lambda j: (0, j))],
        out_specs=pl.BlockSpec((rows, tn), lambda j: (0, j)),
        out_shape=jax.ShapeDtypeStruct((rows, n), F32),
        compiler_params=_params("parallel"),
        name="adaln",
    )(c_rows, ada_w, ada_b.reshape(1, n))


def _split_bf16(v):
    hi = v.astype(BF16)
    return hi, (v - hi.astype(F32)).astype(BF16)


def _softplus(v):
    return jnp.maximum(v, 0.0) + jnp.log1p(jnp.exp(-jnp.abs(v)))


def _conv_rows(p_ext, tm, w, bias, has_prev, has_next, apply_silu, o_ref):
    sub = min(CONV_SUB, tm)
    ri = lax.broadcasted_iota(jnp.int32, (sub, sub), 0)
    ci = lax.broadcasted_iota(jnp.int32, (sub, sub), 1)
    shift_down = jnp.where(ri == ci + 1, 1.0, 0.0).astype(BF16)
    shift_up = jnp.where(ri + 1 == ci, 1.0, 0.0).astype(BF16)
    slab = lax.broadcasted_iota(jnp.int32, (SUBLANES, 1), 0)
    rounded = lambda v: v.astype(BF16).astype(F32)
    prev_row = jnp.where(has_prev, rounded(p_ext[tm + SUBLANES - 1:tm + SUBLANES, :]), 0.0)
    last_next = jnp.where(has_next, rounded(p_ext[tm + SUBLANES:tm + SUBLANES + 1, :]), 0.0)
    for k in range(tm // sub):
        ub = p_ext[k * sub:(k + 1) * sub, :].astype(BF16)
        u = ub.astype(F32)
        next_row = (rounded(p_ext[(k + 1) * sub:(k + 1) * sub + 1, :]) if k + 1 < tm // sub
                    else last_next)
        up = _dot(shift_down, ub)
        un = _dot(shift_up, ub)
        up = jnp.concatenate([jnp.where(slab == 0, prev_row, up[:SUBLANES]), up[SUBLANES:]], axis=0)
        un = jnp.concatenate([un[:-SUBLANES],
                              jnp.where(slab == SUBLANES - 1, next_row, un[-SUBLANES:])], axis=0)
        y = w[0:1, :] * up + w[1:2, :] * u + w[2:3, :] * un + bias
        if apply_silu:
            y = _silu(y)
        o_ref[0, k * sub:(k + 1) * sub, :] = y.astype(o_ref.dtype)
        prev_row = u[sub - 1:sub, :]


def _inproj_kernel(x_ref, xprev_ref, xnext_ref, shift_ref, scale_ref, nw_ref, w_ref, cw_ref,
                   cb_ref, wdt_hi_ref, wdt_lo_ref, dtb_ref, o_ref, dt_ref, h_scr,
                   *, silu_tiles, conv_tiles):
    i = pl.program_id(1)
    j = pl.program_id(2)
    tm = x_ref.shape[1]

    def modulated(xt):
        ms = jnp.mean(xt * xt, axis=-1, keepdims=True)
        hn = xt * lax.rsqrt(ms + NORM_EPS) * nw_ref[...]
        return hn * (1.0 + scale_ref[0]) + shift_ref[0]

    @pl.when(j == 0)
    def _():
        hi, lo = _split_bf16(modulated(x_ref[0]))
        h_scr[0:tm, :] = hi
        halo = jnp.concatenate([xprev_ref[0], xnext_ref[0]], axis=0)
        h_scr[tm:, :] = modulated(halo).astype(BF16)
        dt_raw = (_dot(hi, wdt_hi_ref[...]) + _dot(lo, wdt_hi_ref[...])
                  + _dot(hi, wdt_lo_ref[...]))
        dt_ref[0] = _softplus(dt_raw + dtb_ref[...])

    p_ext = _dot(h_scr[...], w_ref[...])
    in_range = lambda rng: jnp.logical_and(j >= rng[0], j < rng[1])
    is_silu, is_conv = in_range(silu_tiles), in_range(conv_tiles)
    has_prev, has_next = i > 0, i < pl.num_programs(1) - 1

    @pl.when(jnp.logical_not(jnp.logical_or(is_silu, is_conv)))
    def _():
        o_ref[0] = p_ext[0:tm, :].astype(o_ref.dtype)

    @pl.when(is_silu)
    def _():
        _conv_rows(p_ext, tm, cw_ref[...], cb_ref[...], has_prev, has_next, True, o_ref)

    @pl.when(is_conv)
    def _():
        _conv_rows(p_ext, tm, cw_ref[...], cb_ref[...], has_prev, has_next, False, o_ref)


def _inproj(x, shift, scale, norm_w, w_main, conv_w, conv_b, silu_tiles, conv_tiles, w_dt,
            dt_bias, tm):
    b, l, d = x.shape
    n = w_main.shape[1]
    tn = 1024
    nsub = tm // SUBLANES
    last = l // SUBLANES - 1
    w_dt_hi, w_dt_lo = _split_bf16(w_dt)
    return pl.pallas_call(
        functools.partial(_inproj_kernel, silu_tiles=silu_tiles, conv_tiles=conv_tiles),
        grid=(b, l // tm, n // tn),
        in_specs=[pl.BlockSpec((1, tm, d), lambda bi, i, j: (bi, i, 0)),
                  pl.BlockSpec((1, SUBLANES, d),
                               lambda bi, i, j: (bi, jnp.maximum(i * nsub - 1, 0), 0)),
                  pl.BlockSpec((1, SUBLANES, d),
                               lambda bi, i, j: (bi, jnp.minimum((i + 1) * nsub, last), 0)),
                  pl.BlockSpec((1, 1, d), lambda bi, i, j: (bi, 0, 0)),
                  pl.BlockSpec((1, 1, d), lambda bi, i, j: (bi, 0, 0)),
                  pl.BlockSpec((1, d), lambda bi, i, j: (0, 0)),
                  pl.BlockSpec((d, tn), lambda bi, i, j: (0, j)),
                  pl.BlockSpec((3, tn), lambda bi, i, j: (0, j)),
                  pl.BlockSpec((1, tn), lambda bi, i, j: (0, j)),
                  pl.BlockSpec((d, DT_PAD), lambda bi, i, j: (0, 0)),
                  pl.BlockSpec((d, DT_PAD), lambda bi, i, j: (0, 0)),
                  pl.BlockSpec((1, DT_PAD), lambda bi, i, j: (0, 0))],
        out_specs=[pl.BlockSpec((1, tm, tn), lambda bi, i, j: (bi, i, j)),
                   pl.BlockSpec((1, tm, DT_PAD), lambda bi, i, j: (bi, i, 0))],
        out_shape=[jax.ShapeDtypeStruct((b, l, n), BF16),
                   jax.ShapeDtypeStruct((b, l, DT_PAD), F32)],
        scratch_shapes=[pltpu.VMEM((tm + 2 * SUBLANES, d), BF16)],
        compiler_params=_params("parallel", "parallel", "arbitrary"),
        name="inproj",
    )(x, x, x, shift, scale, norm_w, w_main, conv_w, conv_b, w_dt_hi, w_dt_lo, dt_bias)


CONV_SUB = 256


def _cumsum_dot(tri_bf16, v):
    hi = v.astype(BF16)
    rest = v - hi.astype(F32)
    mid = rest.astype(BF16)
    lo = (rest - mid.astype(F32)).astype(BF16)
    return _dot(tri_bf16, hi) + _dot(tri_bf16, mid) + _dot(tri_bf16, lo)


def _expand_heads(v, e_bf16):
    hi = v.astype(BF16)
    lo = (v - hi.astype(F32)).astype(BF16)
    return _dot(hi, e_bf16) + _dot(lo, e_bf16)


def _ssd_chunk(fwd, xbc, dt_all, a_row, dskip_row, y_ref, state):
    q = M_CHUNK
    off = 0 if fwd else M_HEADS
    last = q - 1 if fwd else 0
    x = xbc[:, :D_MODEL].astype(F32)
    bmat = xbc[:, D_MODEL:D_MODEL + M_GROUPS * M_STATE]
    cmat = xbc[:, D_MODEL + M_GROUPS * M_STATE:]

    a_all = dt_all * a_row
    ri = lax.broadcasted_iota(jnp.int32, (q, q), 0)
    ci = lax.broadcasted_iota(jnp.int32, (q, q), 1)
    mask = (ri >= ci) if fwd else (ri <= ci)
    acs_all = _cumsum_dot(jnp.where(mask, 1.0, 0.0).astype(BF16), a_all)
    acs_t = acs_all.T[off:off + M_HEADS, :]
    acs = acs_all[:, off:off + M_HEADS]
    e_atot_c = jnp.exp(acs_t[:, last:last + 1])

    hid = lax.broadcasted_iota(jnp.int32, (M_HEADS, D_MODEL), 0)
    lid = lax.broadcasted_iota(jnp.int32, (M_HEADS, D_MODEL), 1)
    expand = jnp.where((lid >> HEADDIM_SHIFT) == hid, 1.0, 0.0).astype(BF16)
    per_head = jnp.concatenate([dt_all[:, off:off + M_HEADS],
                                jnp.exp(acs[last:last + 1, :] - acs),
                                jnp.exp(acs)], axis=0)
    wide = _dot(per_head.astype(BF16), expand)
    xdt = x * wide[:q]
    xw = (xdt * wide[q:2 * q]).astype(BF16)
    e_acs = wide[2 * q:]

    gw = M_HPG * M_HEADDIM
    rb = lax.broadcasted_iota(jnp.int32, (M_HPG * q, gw), 0) >> CHUNK_SHIFT
    cb = lax.broadcasted_iota(jnp.int32, (M_HPG * q, gw), 1) >> HEADDIM_SHIFT
    blockdiag = rb == cb
    for g in range(M_GROUPS):
        bg = bmat[:, g * M_STATE:(g + 1) * M_STATE]
        cg = cmat[:, g * M_STATE:(g + 1) * M_STATE]
        scores = _dot_nt(cg, bg)
        ms = []
        for r in range(M_HPG):
            h = g * M_HPG + r
            seg = acs[:, h:h + 1] - acs_t[h:h + 1, :]
            decay = jnp.exp(jnp.where(mask, seg, -jnp.inf))
            ms.append((scores * decay).astype(BF16))
        m_cat = jnp.concatenate(ms, axis=1)
        xg = xdt[:, g * gw:(g + 1) * gw]
        x_bd = jnp.where(blockdiag, jnp.concatenate([xg] * M_HPG, axis=0),
                         0.0).astype(BF16)
        y_diag = _dot(m_cat, x_bd)
        s_g = state[g * M_HPG:(g + 1) * M_HPG].reshape(gw, M_STATE).astype(BF16)
        y_g = y_diag + e_acs[:, g * gw:(g + 1) * gw] * _dot_nt(cg, s_g)
        if fwd:
            y_g = y_g + dskip_row[:, g * gw:(g + 1) * gw] * x[:, g * gw:(g + 1) * gw]
        y_ref[0, :, g * gw:(g + 1) * gw] = y_g.astype(y_ref.dtype)
        upd = _dot_tn(xw[:, g * gw:(g + 1) * gw], bg)
        for r in range(M_HPG):
            h = g * M_HPG + r
            state[h] = (state[h] * e_atot_c[h:h + 1, 0:1]
                        + upd[r * M_HEADDIM:(r + 1) * M_HEADDIM, :])


def _ssd_kernel(xlf_ref, xcf_ref, xlb_ref, xcb_ref, dtlf_ref, dtcf_ref, dtlb_ref, dtcb_ref, a_ref,
                dskip_ref, yf_ref, yb_ref, state_f, state_b, *, n_ctx_chunks):
    s = pl.program_id(1)
    is_ctx = s < n_ctx_chunks

    @pl.when(s == 0)
    def _():
        state_f[...] = jnp.zeros_like(state_f)
        state_b[...] = jnp.zeros_like(state_b)

    dt_f = jnp.where(is_ctx, dtcf_ref[0], dtlf_ref[0])
    dt_b = jnp.where(is_ctx, dtcb_ref[0], dtlb_ref[0])
    a_row = -jnp.exp(a_ref[...])
    x_f = jnp.where(is_ctx, xcf_ref[0], xlf_ref[0])
    x_b = jnp.where(is_ctx, xcb_ref[0], xlb_ref[0])
    _ssd_chunk(True, x_f, dt_f, a_row, dskip_ref[...], yf_ref, state_f)
    _ssd_chunk(False, x_b, dt_b, a_row, dskip_ref[...], yb_ref, state_b)


def _ssd(cols, cols_c, dt_l, dt_c, a_r, dskip_row):
    b, l, _ = cols.shape
    lc = cols_c.shape[1]
    q = M_CHUNK
    nl, nc = l // q, lc // q

    lat_f = lambda si: jnp.maximum(si - nc, 0)
    lat_b = lambda si: nl - 1 - lat_f(si)
    ctx_f = lambda si: jnp.minimum(si, nc - 1)
    ctx_b = lambda si: nc - 1 - ctx_f(si)

    small = lambda shape: pl.BlockSpec(shape, lambda bi, si: (0, 0))
    x_spec = lambda fn: pl.BlockSpec((1, q, M_XBC), lambda bi, si: (bi, fn(si), 0))
    dt_spec = lambda fn: pl.BlockSpec((1, q, DT_PAD), lambda bi, si: (bi, fn(si), 0))
    state = pltpu.VMEM((M_HEADS, M_HEADDIM, M_STATE), F32)
    return pl.pallas_call(
        functools.partial(_ssd_kernel, n_ctx_chunks=nc),
        grid=(b, nl + nc),
        in_specs=[x_spec(lat_f), x_spec(ctx_f), x_spec(lat_b), x_spec(ctx_b),
                  dt_spec(lat_f), dt_spec(ctx_f), dt_spec(lat_b), dt_spec(ctx_b),
                  small((1, DT_PAD)), small((1, D_MODEL))],
        out_specs=[pl.BlockSpec((1, q, D_MODEL), lambda bi, si: (bi, lat_f(si), 0)),
                   pl.BlockSpec((1, q, D_MODEL), lambda bi, si: (bi, lat_b(si), 0))],
        out_shape=[jax.ShapeDtypeStruct((b, l, D_MODEL), BF16)] * 2,
        scratch_shapes=[state, state],
        compiler_params=_params("parallel", "arbitrary"),
        name="ssd",
    )(cols, cols_c, cols, cols_c, dt_l, dt_c, dt_l, dt_c, a_r, dskip_row)


def _filter_kernel(fl_ref, w1_ref, b1_ref, w2_ref, b2_ref, w3_ref, b3_ref, freq_ref, wo_hi_ref,
                   wo_lo_ref, deltas_ref, h_ref, *, seqlen):
    i = pl.program_id(0)
    tr = h_ref.shape[0]
    pos = (i * tr + lax.broadcasted_iota(jnp.int32, (tr, 1), 0))
    posf = pos.astype(F32)
    t = posf / F32(seqlen - 1)
    w = F32(2.0 * math.pi / seqlen) * posf
    ang = fl_ref[...] * w
    lane = lax.broadcasted_iota(jnp.int32, (tr, LANES), 1)
    bands = (H_EMB - 1) // 2
    feats = jnp.where(lane == 0, t,
                      jnp.where(lane <= bands, jnp.cos(ang),
                                jnp.where(lane < H_EMB, -jnp.sin(ang), 0.0)))
    freq = freq_ref[...]
    hid = jnp.sin(freq * (_dot(feats, w1_ref[...], HIGHEST) + b1_ref[...]))
    hid = jnp.sin(freq * (_dot(hid, w2_ref[...], HIGHEST) + b2_ref[...]))
    hid = jnp.sin(freq * (_dot(hid, w3_ref[...], HIGHEST) + b3_ref[...]))
    hid_hi, hid_lo = _split_bf16(hid)
    filt = (_dot(hid_hi, wo_hi_ref[...]) + _dot(hid_lo, wo_hi_ref[...])
            + _dot(hid_hi, wo_lo_ref[...]))
    window = jnp.exp(-t * deltas_ref[...])
    hw = H_WIDTH
    for blk in range(filt.shape[1] // hw):
        h_ref[:, blk * hw:(blk + 1) * hw] = filt[:, blk * hw:(blk + 1) * hw] * window


def _hyena_filters(seqlen, w1, b1, w2, b2, w3, b3, freq, w_out):
    tr = min(512, seqlen)
    bands = (H_EMB - 1) // 2
    f = jnp.linspace(1e-4, bands - 1, bands, dtype=F32)
    fl = jnp.zeros((1, LANES), F32).at[0, 1:1 + bands].set(f).at[0, 1 + bands:H_EMB].set(f)
    w1p = jnp.zeros((LANES, H_HID), F32).at[:H_EMB].set(w1)
    min_decay = math.log(H_DECAY_TARGET) / H_SLOW_DECAY_PCT
    max_decay = math.log(H_DECAY_TARGET) / H_FAST_DECAY_PCT
    deltas = jnp.abs(jnp.linspace(min_decay, max_decay, H_WIDTH, dtype=F32)).reshape(1, H_WIDTH)
    row = lambda v: v.reshape(1, -1)
    full = lambda a: pl.BlockSpec(a.shape, lambda i: (0, 0))
    args = (fl, w1p, row(b1), w2, row(b2), w3, row(b3), row(freq), *_split_bf16(w_out), deltas)
    nout = w_out.shape[1]
    return pl.pallas_call(
        functools.partial(_filter_kernel, seqlen=seqlen),
        grid=(seqlen // tr,),
        in_specs=[full(a) for a in args],
        out_specs=pl.BlockSpec((tr, nout), lambda i: (i, 0)),
        out_shape=jax.ShapeDtypeStruct((seqlen, nout), F32),
        compiler_params=_params("parallel"),
        name="hyena_filters",
    )(*args)


def _r_add(a, b):
    if a is None:
        return b
    if b is None:
        return a
    return a + b


def _r_sub(a, b):
    if b is None:
        return a
    if a is None:
        return -b
    return a - b


def _r_scale(a, s):
    if a is None or s == 0.0:
        return None
    if s == 1.0:
        return a
    if s == -1.0:
        return -a
    return a * s


def _c_mul_const(z, w):
    snap = lambda v: float(round(v)) if abs(v - round(v)) < 1e-12 else float(v)
    wr, wi = snap(w.real), snap(w.imag)
    zr, zi = z
    return (_r_sub(_r_scale(zr, wr), _r_scale(zi, wi)),
            _r_add(_r_scale(zr, wi), _r_scale(zi, wr)))


def _fft(xs, sign, first_half_only=False):
    n = len(xs)
    if n == 1:
        return list(xs)
    even, odd = _fft(xs[0::2], sign), _fft(xs[1::2], sign)
    out = [None] * n
    for k in range(n // 2):
        tr, ti = _c_mul_const(odd[k], complex(math.cos(2 * math.pi * k / n),
                                              sign * math.sin(2 * math.pi * k / n)))
        out[k] = (_r_add(even[k][0], tr), _r_add(even[k][1], ti))
        if not first_half_only:
            out[k + n // 2] = (_r_sub(even[k][0], tr), _r_sub(even[k][1], ti))
    return out[:n // 2] if first_half_only else out


def _tile_or_zero(v):
    return jnp.zeros((ROW_CHUNK, LANES), F32) if v is None else v


def _radix_fwd_kernel(z_ref, o_ref, *, real_input):
    _, _, half, rows_total, width = z_ref.shape

    def body(i, carry):
        rows = pl.ds(pl.multiple_of(i * ROW_CHUNK, ROW_CHUNK), ROW_CHUNK)
        for j in range(width // LANES):
            lanes = slice(j * LANES, (j + 1) * LANES)
            xs = []
            for t1 in range(half):
                re = z_ref[0, 0, t1, rows, lanes].astype(F32)
                im = None if real_input else z_ref[0, 1, t1, rows, lanes].astype(F32)
                xs.append((re, im))
            xs += [(None, None)] * half
            for k1, (re, im) in enumerate(_fft(xs, -1.0)):
                o_ref[0, k1, 0, rows, lanes] = _tile_or_zero(re).astype(BF16)
                o_ref[0, k1, 1, rows, lanes] = _tile_or_zero(im).astype(BF16)
        return carry

    lax.fori_loop(0, rows_total // ROW_CHUNK, body, 0)


def _radix_fwd(z, col_block, ncols, real_input, name):
    p, parts, half, m, _ = z.shape
    tt, cb = min(256, m), 256
    off = col_block * (H_WIDTH // cb)
    return pl.pallas_call(
        functools.partial(_radix_fwd_kernel, real_input=real_input),
        grid=(p, m // tt, ncols // cb),
        in_specs=[pl.BlockSpec((1, parts, half, tt, cb), lambda pi, ti, ci: (pi, 0, 0, ti, off + ci))],
        out_specs=pl.BlockSpec((1, 2 * half, 2, tt, cb), lambda pi, ti, ci: (pi, 0, 0, ti, ci)),
        out_shape=jax.ShapeDtypeStruct((p, 2 * half, 2, m, ncols), BF16),
        compiler_params=_params("parallel", "parallel", "parallel"),
        name=name,
    )(z)


def _radix_inv_kernel(sb_ref, gate_ref, zin_ref, bias_ref, o_ref):
    _, radix, _, rows_total, width = sb_ref.shape

    def body(i, carry):
        rows = pl.ds(pl.multiple_of(i * ROW_CHUNK, ROW_CHUNK), ROW_CHUNK)
        for j in range(width // LANES):
            lanes = slice(j * LANES, (j + 1) * LANES)
            ys = [(sb_ref[0, k1, 0, rows, lanes].astype(F32),
                   sb_ref[0, k1, 1, rows, lanes].astype(F32)) for k1 in range(radix)]
            bias = bias_ref[:, lanes]
            for t1, parts in enumerate(_fft(ys, 1.0, first_half_only=True)):
                for q in range(2):
                    zin = zin_ref[0, q, t1, rows, lanes].astype(F32)
                    gate = gate_ref[0, q, t1, rows, lanes].astype(F32)
                    o_ref[0, q, t1, rows, lanes] = (gate * (parts[q] + bias * zin)).astype(o_ref.dtype)
        return carry

    lax.fori_loop(0, rows_total // ROW_CHUNK, body, 0)


def _radix_inv(sb, gate_src, gate_block, zin_src, zin_block, bias_row, name):
    p, radix, _, m, ncols = sb.shape
    half = radix // 2
    tt, cb = min(256, m), 256
    goff, zoff = gate_block * (H_WIDTH // cb), zin_block * (H_WIDTH // cb)
    tok = lambda off: pl.BlockSpec((1, 2, half, tt, cb), lambda pi, ti, ci: (pi, 0, 0, ti, off + ci))
    return pl.pallas_call(
        _radix_inv_kernel,
        grid=(p, m // tt, ncols // cb),
        in_specs=[pl.BlockSpec((1, radix, 2, tt, cb), lambda pi, ti, ci: (pi, 0, 0, ti, ci)),
                  tok(goff), tok(zoff),
                  pl.BlockSpec((1, cb), lambda pi, ti, ci: (0, ci))],
        out_specs=tok(0),
        out_shape=jax.ShapeDtypeStruct((p, 2, half, m, ncols), BF16),
        compiler_params=_params("parallel", "parallel", "parallel"),
        name=name,
    )(sb, gate_src, zin_src, bias_row)


def _dft_seed_kernel(cb_ref, sb_ref, ca_ref, sa_ref, *, n):
    m = cb_ref.shape[0]
    r = lax.broadcasted_iota(jnp.int32, (m, m), 0)
    c = lax.broadcasted_iota(jnp.int32, (m, m), 1)
    beta = ((r * c) & (m - 1)).astype(F32) * F32(2.0 * math.pi / m)
    cb_ref[...] = jnp.cos(beta)
    sb_ref[...] = jnp.sin(beta)
    k1 = lax.broadcasted_iota(jnp.int32, ca_ref.shape, 0)
    t2 = lax.broadcasted_iota(jnp.int32, ca_ref.shape, 1)
    alpha = (k1 * t2).astype(F32) * F32(2.0 * math.pi / n)
    ca_ref[...] = jnp.cos(alpha)
    sa_ref[...] = jnp.sin(alpha)


def _dft_table_kernel(cb_ref, sb_ref, car_ref, sar_ref, cac_ref, sac_ref, efwd_ref, einv_ref):
    m = cb_ref.shape[0]
    cb, sb = cb_ref[...], sb_ref[...]
    car, sar = car_ref[0], sar_ref[0]
    cos_f = (car * cb - sar * sb).astype(BF16)
    sin_f = (sar * cb + car * sb).astype(BF16)
    efwd_ref[0, :m, :m] = cos_f
    efwd_ref[0, :m, m:] = sin_f
    efwd_ref[0, m:, :m] = -sin_f
    efwd_ref[0, m:, m:] = cos_f
    cac, sac = cac_ref[0], sac_ref[0]
    cos_i = (cac * cb - sac * sb).astype(BF16)
    sin_i = (sac * cb + cac * sb).astype(BF16)
    einv_ref[0, :m, :m] = cos_i
    einv_ref[0, :m, m:] = -sin_i
    einv_ref[0, m:, :m] = sin_i
    einv_ref[0, m:, m:] = cos_i


def _dft_tables(n):
    m = n // FFT_RADIX
    cb, sb, ca, sa = pl.pallas_call(
        functools.partial(_dft_seed_kernel, n=n),
        out_shape=[jax.ShapeDtypeStruct((m, m), F32)] * 2
        + [jax.ShapeDtypeStruct((FFT_RADIX, m), F32)] * 2,
        compiler_params=pltpu.CompilerParams(vmem_limit_bytes=VMEM_LIMIT),
        name="dft_seed",
    )()
    full = pl.BlockSpec((m, m), lambda i: (0, 0))
    rowspec = pl.BlockSpec((1, 1, m), lambda i: (i, 0, 0))
    colspec = pl.BlockSpec((1, m, 1), lambda i: (i, 0, 0))
    tab = pl.BlockSpec((1, 2 * m, 2 * m), lambda i: (i, 0, 0))
    return pl.pallas_call(
        _dft_table_kernel,
        grid=(FFT_RADIX,),
        in_specs=[full, full, rowspec, rowspec, colspec, colspec],
        out_specs=[tab, tab],
        out_shape=[jax.ShapeDtypeStruct((FFT_RADIX, 2 * m, 2 * m), BF16)] * 2,
        compiler_params=_params("parallel"),
        name="dft_tables",
    )(cb, sb, ca.reshape(FFT_RADIX, 1, m), sa.reshape(FFT_RADIX, 1, m),
      ca.reshape(FFT_RADIX, m, 1), sa.reshape(FFT_RADIX, m, 1))


def _filter_spectrum_kernel(efwd_ref, hf_ref, hb_ref, kr_ref, ki_ref, *, n):
    m = kr_ref.shape[1]
    xf = _dot(efwd_ref[0], hf_ref[0, 0])
    xb = _dot(efwd_ref[0], hb_ref[0, 0])
    kr_ref[0] = (xf[:m] + xb[:m]) * F32(1.0 / n)
    ki_ref[0] = (xf[m:] - xb[m:]) * F32(1.0 / n)


def _filter_spectrum(efwd, sah):
    radix, m2, _ = efwd.shape
    n = radix * m2 // 2
    orders = sah.shape[3] // (2 * H_WIDTH)
    slab = lambda off: pl.BlockSpec((1, 1, m2, H_WIDTH), lambda k1, o: (0, k1, 0, off + o))
    kspec = pl.BlockSpec((1, m2 // 2, H_WIDTH), lambda k1, o: (k1, 0, o))
    return pl.pallas_call(
        functools.partial(_filter_spectrum_kernel, n=n),
        grid=(radix, orders),
        in_specs=[pl.BlockSpec((1, m2, m2), lambda k1, o: (k1, 0, 0)), slab(0), slab(orders)],
        out_specs=[kspec, kspec],
        out_shape=[jax.ShapeDtypeStruct((radix, m2 // 2, orders * H_WIDTH), F32)] * 2,
        compiler_params=_params("parallel", "arbitrary"),
        name="filter_spectrum",
    )(efwd, sah, sah)


def _spectral_kernel(efwd_ref, einv_ref, sa_ref, kr_ref, ki_ref, sb_ref):
    slabs, m = kr_ref.shape[0], kr_ref.shape[1]
    width = sa_ref.shape[3]
    cw = min(512, width)
    for s in range(slabs):
        for j in range(width // cw):
            cols = slice(j * cw, (j + 1) * cw)
            x = _dot(efwd_ref[s], sa_ref[0, s, :, cols])
            xr, xi = x[:m], x[m:]
            kr, ki = kr_ref[s, :, cols], ki_ref[s, :, cols]
            y = jnp.concatenate([xr * kr - xi * ki, xr * ki + xi * kr], axis=0).astype(BF16)
            sb_ref[0, s, :, cols] = _dot(einv_ref[s], y).astype(BF16)


def _spectral(efwd, einv, sa, kr, ki, order):
    p, radix, m2, ncols = sa.shape
    ns = SPECTRAL_SLABS
    slab = pl.BlockSpec((1, ns, m2, ncols), lambda k1, pi: (pi, k1, 0, 0))
    tab = pl.BlockSpec((ns, m2, m2), lambda k1, pi: (k1, 0, 0))
    kspec = pl.BlockSpec((ns, m2 // 2, ncols), lambda k1, pi: (k1, 0, order))
    return pl.pallas_call(
        _spectral_kernel,
        grid=(radix // ns, p),
        in_specs=[tab, tab, slab, kspec, kspec],
        out_specs=slab,
        out_shape=jax.ShapeDtypeStruct(sa.shape, BF16),
        compiler_params=_params("parallel", "arbitrary"),
        name="hyena_spectral",
    )(efwd, einv, sa, kr, ki)


def _output_kernel(yf_ref, yb_ref, z_ref, hg_ref, gm_ref, gh_ref, yh_ref, x_ref, gate_ref, mnw_ref,
                   fnw_ref, wm_ref, wh_ref, wo_ref, o_ref):
    f32 = lambda ref: ref[0].astype(F32)
    g = (f32(yf_ref) + f32(yb_ref)) * _silu(f32(z_ref))
    gw = D_MODEL // M_GROUPS
    parts = []
    for i in range(M_GROUPS):
        gi = g[:, i * gw:(i + 1) * gw]
        parts.append(gi * lax.rsqrt(jnp.mean(gi * gi, axis=-1, keepdims=True) + NORM_EPS))
    ym = (jnp.concatenate(parts, axis=1) * mnw_ref[...]).astype(BF16)
    yh = (f32(yh_ref) * _silu(f32(hg_ref))).astype(BF16)
    merged = (jax.nn.sigmoid(f32(gm_ref)) * _dot(ym, wm_ref[...])
              + jax.nn.sigmoid(f32(gh_ref)) * _dot(yh, wh_ref[...]))
    out = _dot(merged.astype(BF16), wo_ref[...])
    xn = x_ref[0] + gate_ref[0] * out
    ms = jnp.mean(xn * xn, axis=-1, keepdims=True)
    o_ref[0] = xn * lax.rsqrt(ms + NORM_EPS) * fnw_ref[...]


def _mixer_output(y_f, y_b, cols, yh, x, gate, m_norm_w, final_norm_w, wm, wh, wo):
    b, l, d = x.shape
    tm = min(512, l)
    tok = lambda blk: pl.BlockSpec((1, tm, d), lambda bi, i: (bi, i, blk))
    wspec = pl.BlockSpec((d, d), lambda bi, i: (0, 0))
    rowspec = pl.BlockSpec((1, d), lambda bi, i: (0, 0))
    return pl.pallas_call(
        _output_kernel,
        grid=(b, l // tm),
        in_specs=[tok(0), tok(0),
                  tok(COL_Z), tok(COL_HGATE), tok(COL_GM), tok(COL_GH), tok(0), tok(0),
                  pl.BlockSpec((1, 1, d), lambda bi, i: (bi, 0, 0)),
                  rowspec, rowspec, wspec, wspec, wspec],
        out_specs=tok(0),
        out_shape=jax.ShapeDtypeStruct((b, l, d), F32),
        compiler_params=_params("parallel", "parallel"),
        name="mixer_output",
    )(y_f, y_b, cols, cols, cols, cols, yh, x, gate, m_norm_w.reshape(1, d),
      final_norm_w.reshape(1, d), wm, wh, wo)


def _layer(x, ctx, c, c_ctx, ada_w, ada_b, norm_w, w_in, m_conv_w, m_conv_b, m_dt_bias, m_a_log,
           m_d, m_norm_w, h_conv_w, h_conv_b, h_w1, h_b1, h_w2, h_b2, h_w3, h_b3, h_freq,
           h_w_out, h_bias, w_branch_m, w_branch_h, w_out, final_norm_w):
    b, l, d = x.shape
    lc = ctx.shape[1]
    ndt = 2 * M_HEADS

    pad_rows = -(b + 1) % SUBLANES
    c_rows = jnp.concatenate([c, c_ctx[None], jnp.zeros((pad_rows, d), F32)], axis=0)
    mod = _adaln(c_rows, ada_w, ada_b)
    shift, scale, gate = (mod[:b, i * d:(i + 1) * d].reshape(b, 1, d) for i in range(3))
    shift_c = jnp.broadcast_to(mod[b, :d], (b, 1, d))
    scale_c = jnp.broadcast_to(mod[b, d:2 * d], (b, 1, d))

    o_z, o_xbc, o_dt = 0, d, d + M_XBC
    o_hg = o_dt + ndt
    o_hp = o_hg + H_WIDTH
    o_gm = o_hp + 3 * H_WIDTH
    w_xbc = w_in[:, o_xbc:o_dt]
    w_main = jnp.concatenate([w_xbc, w_in[:, o_z:o_xbc], w_in[:, o_hg:]], axis=1).astype(BF16)
    w_dt_pad = jnp.zeros((d, DT_PAD), F32).at[:, :ndt].set(w_in[:, o_dt:o_hg])
    nw = norm_w.reshape(1, d)

    n_main = w_main.shape[1]
    conv_w = (jnp.zeros((3, n_main), F32).at[:, :M_XBC].set(m_conv_w)
              .at[:, COL_HPROJ * 1024:(COL_HPROJ + 3) * 1024].set(h_conv_w))
    conv_b = (jnp.zeros((1, n_main), F32).at[0, :M_XBC].set(m_conv_b)
              .at[0, COL_HPROJ * 1024:(COL_HPROJ + 3) * 1024].set(h_conv_b))
    silu_tiles, conv_tiles = (0, M_XBC // 1024), (COL_HPROJ, COL_HPROJ + 3)

    bias_r = jnp.zeros((1, DT_PAD), F32).at[0, :ndt].set(m_dt_bias.reshape(ndt))
    cols, dt_l = _inproj(x, shift, scale, nw, w_main, conv_w, conv_b, silu_tiles, conv_tiles,
                         w_dt_pad, bias_r, min(1024, l))
    cols_c, dt_c = _inproj(ctx, shift_c, scale_c, nw, w_xbc.astype(BF16), conv_w[:, :M_XBC],
                           conv_b[:, :M_XBC], silu_tiles, (0, 0), w_dt_pad, bias_r, min(1024, lc))

    a_r = jnp.zeros((1, DT_PAD), F32).at[0, :ndt].set(m_a_log.reshape(ndt))
    dskip_row = jnp.repeat(m_d, M_HEADDIM).reshape(1, d)
    y_f, y_b = _ssd(cols, cols_c, dt_l, dt_c, a_r, dskip_row)

    assert b % 2 == 0, "batch rows are packed in pairs as complex signals"
    half = FFT_RADIX // 2
    m = l // half
    pairs = b // 2
    u3 = cols.reshape(pairs, 2, half, m, n_main)
    filt = _hyena_filters(l, h_w1, h_b1, h_w2, h_b2, h_w3, h_b3, h_freq, h_w_out)
    efwd, einv = _dft_tables(2 * l)
    sah = _radix_fwd(filt.reshape(1, 1, half, m, filt.shape[1]), 0, filt.shape[1], True,
                     "filter_radix")
    kr, ki = _filter_spectrum(efwd, sah.reshape(1, FFT_RADIX, 2 * m, filt.shape[1]))
    z, z_block = u3, COL_HPROJ
    for order in range(2):
        sa = _radix_fwd(z, z_block, H_WIDTH, False, "hyena_radix_fwd")
        sb = _spectral(efwd, einv, sa.reshape(pairs, FFT_RADIX, 2 * m, H_WIDTH), kr, ki, order)
        z = _radix_inv(sb.reshape(pairs, FFT_RADIX, 2, m, H_WIDTH), u3, COL_HPROJ + 1 + order,
                       z, z_block, h_bias[order].reshape(1, H_WIDTH), "hyena_radix_inv")
        z_block = 0
    y_h = z.reshape(b, l, H_WIDTH)

    return _mixer_output(y_f, y_b, cols, y_h, x, gate, m_norm_w, final_norm_w,
                         w_branch_m.astype(BF16), w_branch_h.astype(BF16), w_out.astype(BF16))


def kernel(x, c, ctx, c_ctx, ada_w, ada_b, norm_w, w_in, m_conv_w, m_conv_b, m_dt_bias, m_a_log,
           m_d, m_norm_w, h_conv_w, h_conv_b, h_w1, h_b1, h_w2, h_b2, h_w3, h_b3, h_freq,
           h_w_out, h_bias, w_branch_m, w_branch_h, w_out, final_norm_w):
    assert ada_w.shape[0] == 1, "one trunk layer"
    return _layer(x, ctx, c, c_ctx, ada_w[0], ada_b[0], norm_w[0], w_in[0], m_conv_w[0],
                  m_conv_b[0], m_dt_bias[0], m_a_log[0], m_d[0], m_norm_w[0], h_conv_w[0],
                  h_conv_b[0], h_w1[0], h_b1[0], h_w2[0], h_b2[0], h_w3[0], h_b3[0], h_freq[0],
                  h_w_out[0], h_bias[0], w_branch_m[0], w_branch_h[0], w_out[0], final_norm_w)
```

```python
import functools
import math

import jax
import jax.numpy as jnp
from jax import lax
from jax.experimental import pallas as pl
from jax.experimental.pallas import tpu as pltpu

F32 = jnp.float32
BF16 = jnp.bfloat16
HIGHEST = lax.Precision.HIGHEST

LANES = 128
SUBLANES = 8
VMEM_LIMIT = 56 * 1024 * 1024

NORM_EPS = 1e-6
D_MODEL = 1024
M_HEADDIM = 64
M_HEADS = 16
M_GROUPS = 4
M_HPG = M_HEADS // M_GROUPS
M_STATE = 128
M_XBC = D_MODEL + 2 * M_GROUPS * M_STATE
M_CHUNK = 256
HEADDIM_SHIFT = M_HEADDIM.bit_length() - 1
CHUNK_SHIFT = M_CHUNK.bit_length() - 1
H_WIDTH = 1024
H_EMB = 33
H_HID = 64
H_DECAY_TARGET = 1e-2
H_FAST_DECAY_PCT = 0.3
H_SLOW_DECAY_PCT = 1.5
DT_PAD = LANES
FFT_RADIX = 32
ROW_CHUNK = 16
SPECTRAL_SLABS = 2
PROJ_SLAB = 256

COL_XBC, COL_Z, COL_HGATE, COL_HPROJ, COL_GM, COL_GH = 0, 2, 3, 4, 7, 8
N_MAIN = 9 * 1024


def _dot(a, b, precision=None):
    return jnp.dot(a, b, preferred_element_type=F32, precision=precision)


def _dot_nt(a, b, precision=None):
    return lax.dot_general(a, b, (((1,), (1,)), ((), ())), preferred_element_type=F32,
                           precision=precision)


def _dot_tn(a, b, precision=None):
    return lax.dot_general(a, b, (((0,), (0,)), ((), ())), preferred_element_type=F32,
                           precision=precision)


def _params(*sem):
    return pltpu.CompilerParams(dimension_semantics=sem, vmem_limit_bytes=VMEM_LIMIT)


def _silu(v):
    return v * jax.nn.sigmoid(v)


def _adaln_kernel(c_ref, w_ref, b_ref, o_ref):
    o_ref[...] = _dot(_silu(c_ref[...]), w_ref[...], HIGHEST) + b_ref[...]


def _adaln(c_rows, ada_w, ada_b):
    rows, d = c_rows.shape
    n = ada_w.shape[1]
    tn = 1024
    return pl.pallas_call(
        _adaln_kernel,
        grid=(n // tn,),
        in_specs=[pl.BlockSpec((rows, d), lambda j: (0, 0)),
                  pl.BlockSpec((d, tn), lambda j: (0, j)),
                  pl.BlockSpec((1, tn), lambda j: (0, j))],
        out_specs=pl.BlockSpec((rows, tn), lambda j: (0, j)),
        out_shape=jax.ShapeDtypeStruct((rows, n), F32),
        compiler_params=_params("parallel"),
        name="adaln",
    )(c_rows, ada_w, ada_b.reshape(1, n))


def _split_bf16(v):
    hi = v.astype(BF16)
    return hi, (v - hi.astype(F32)).astype(BF16)


def _softplus(v):
    return jnp.maximum(v, 0.0) + jnp.log1p(jnp.exp(-jnp.abs(v)))


def _conv_rows(p_ext, tm, w, bias, has_prev, has_next, apply_silu):
    u = p_ext[0:tm, :]
    slab = lax.broadcasted_iota(jnp.int32, (SUBLANES, 1), 0)
    prev_row = jnp.where(has_prev, p_ext[tm + SUBLANES - 1:tm + SUBLANES, :], 0.0)
    next_row = jnp.where(has_next, p_ext[tm + SUBLANES:tm + SUBLANES + 1, :], 0.0)
    up = pltpu.roll(u, 1, 0)
    un = pltpu.roll(u, tm - 1, 0)
    up = jnp.concatenate([jnp.where(slab == 0, prev_row, up[:SUBLANES]), up[SUBLANES:]], axis=0)
    un = jnp.concatenate([un[:-SUBLANES],
                          jnp.where(slab == SUBLANES - 1, next_row, un[-SUBLANES:])], axis=0)
    y = w[0:1, :] * up + w[1:2, :] * u + w[2:3, :] * un + bias
    return _silu(y) if apply_silu else y


def _inproj_kernel(x_ref, xprev_ref, xnext_ref, shift_ref, scale_ref, nw_ref, w_ref, cw_ref,
                   cb_ref, wdt_hi_ref, wdt_lo_ref, dtb_ref, o_ref, dt_ref, h_scr,
                   *, silu_tiles, conv_tiles):
    i = pl.program_id(1)
    j = pl.program_id(2)
    tm = x_ref.shape[1]

    def modulated(xt):
        ms = jnp.mean(xt * xt, axis=-1, keepdims=True)
        hn = xt * lax.rsqrt(ms + NORM_EPS) * nw_ref[...]
        return hn * (1.0 + scale_ref[0]) + shift_ref[0]

    @pl.when(j == 0)
    def _():
        hi, lo = _split_bf16(modulated(x_ref[0]))
        h_scr[0:tm, :] = hi
        halo = jnp.concatenate([xprev_ref[0], xnext_ref[0]], axis=0)
        h_scr[tm:, :] = modulated(halo).astype(BF16)
        dt_raw = (_dot(hi, wdt_hi_ref[...]) + _dot(lo, wdt_hi_ref[...])
                  + _dot(hi, wdt_lo_ref[...]))
        dt_ref[0] = _softplus(dt_raw + dtb_ref[...])

    in_range = lambda rng: jnp.logical_and(j >= rng[0], j < rng[1])
    is_silu, is_conv = in_range(silu_tiles), in_range(conv_tiles)
    has_prev, has_next = i > 0, i < pl.num_programs(1) - 1

    def project(mode):
        for c in range(o_ref.shape[2] // PROJ_SLAB):
            cols = slice(c * PROJ_SLAB, (c + 1) * PROJ_SLAB)
            p_ext = _dot(h_scr[...], w_ref[:, cols])
            if mode == "plain":
                y = p_ext[0:tm, :]
            else:
                y = _conv_rows(p_ext, tm, cw_ref[:, cols], cb_ref[:, cols], has_prev, has_next,
                               mode == "silu")
            o_ref[0, :, cols] = y.astype(o_ref.dtype)

    pl.when(jnp.logical_not(jnp.logical_or(is_silu, is_conv)))(lambda: project("plain"))
    pl.when(is_silu)(lambda: project("silu"))
    pl.when(is_conv)(lambda: project("conv"))


def _inproj(x, shift, scale, norm_w, w_main, conv_w, conv_b, silu_tiles, conv_tiles, w_dt,
            dt_bias, tm):
    b, l, d = x.shape
    n = w_main.shape[1]
    tn = 1024
    nsub = tm // SUBLANES
    last = l // SUBLANES - 1
    w_dt_hi, w_dt_lo = _split_bf16(w_dt)
    return pl.pallas_call(
        functools.partial(_inproj_kernel, silu_tiles=silu_tiles, conv_tiles=conv_tiles),
        grid=(b, l // tm, n // tn),
        in_specs=[pl.BlockSpec((1, tm, d), lambda bi, i, j: (bi, i, 0)),
                  pl.BlockSpec((1, SUBLANES, d),
                               lambda bi, i, j: (bi, jnp.maximum(i * nsub - 1, 0), 0)),
                  pl.BlockSpec((1, SUBLANES, d),
                               lambda bi, i, j: (bi, jnp.minimum((i + 1) * nsub, last), 0)),
                  pl.BlockSpec((1, 1, d), lambda bi, i, j: (bi, 0, 0)),
                  pl.BlockSpec((1, 1, d), lambda bi, i, j: (bi, 0, 0)),
                  pl.BlockSpec((1, d), lambda bi, i, j: (0, 0)),
                  pl.BlockSpec((d, tn), lambda bi, i, j: (0, j)),
                  pl.BlockSpec((3, tn), lambda bi, i, j: (0, j)),
                  pl.BlockSpec((1, tn), lambda bi, i, j: (0, j)),
                  pl.BlockSpec((d, DT_PAD), lambda bi, i, j: (0, 0)),
                  pl.BlockSpec((d, DT_PAD), lambda bi, i, j: (0, 0)),
                  pl.BlockSpec((1, DT_PAD), lambda bi, i, j: (0, 0))],
        out_specs=[pl.BlockSpec((1, tm, tn), lambda bi, i, j: (bi, i, j)),
                   pl.BlockSpec((1, tm, DT_PAD), lambda bi, i, j: (bi, i, 0))],
        out_shape=[jax.ShapeDtypeStruct((b, l, n), BF16),
                   jax.ShapeDtypeStruct((b, l, DT_PAD), F32)],
        scratch_shapes=[pltpu.VMEM((tm + 2 * SUBLANES, d), BF16)],
        compiler_params=_params("parallel", "parallel", "arbitrary"),
        name="inproj",
    )(x, x, x, shift, scale, norm_w, w_main, conv_w, conv_b, w_dt_hi, w_dt_lo, dt_bias)


def _cumsum_dot(tri_bf16, v):
    hi = v.astype(BF16)
    rest = v - hi.astype(F32)
    mid = rest.astype(BF16)
    lo = (rest - mid.astype(F32)).astype(BF16)
    return _dot(tri_bf16, hi) + _dot(tri_bf16, mid) + _dot(tri_bf16, lo)


def _expand_heads(v, e_bf16):
    hi = v.astype(BF16)
    lo = (v - hi.astype(F32)).astype(BF16)
    return _dot(hi, e_bf16) + _dot(lo, e_bf16)


def _ssd_chunk(fwd, xbc, dt_all, a_row, dskip_row, y_ref, state):
    q = M_CHUNK
    off = 0 if fwd else M_HEADS
    last = q - 1 if fwd else 0
    x = xbc[:, :D_MODEL].astype(F32)
    bmat = xbc[:, D_MODEL:D_MODEL + M_GROUPS * M_STATE]
    cmat = xbc[:, D_MODEL + M_GROUPS * M_STATE:]

    a_all = dt_all * a_row
    ri = lax.broadcasted_iota(jnp.int32, (q, q), 0)
    ci = lax.broadcasted_iota(jnp.int32, (q, q), 1)
    mask = (ri >= ci) if fwd else (ri <= ci)
    acs_all = _cumsum_dot(jnp.where(mask, 1.0, 0.0).astype(BF16), a_all)
    acs_t = acs_all.T[off:off + M_HEADS, :]
    acs = acs_all[:, off:off + M_HEADS]
    e_atot_c = jnp.exp(acs_t[:, last:last + 1])

    hid = lax.broadcasted_iota(jnp.int32, (M_HEADS, D_MODEL), 0)
    lid = lax.broadcasted_iota(jnp.int32, (M_HEADS, D_MODEL), 1)
    expand = jnp.where((lid >> HEADDIM_SHIFT) == hid, 1.0, 0.0).astype(BF16)
    per_head = jnp.concatenate([dt_all[:, off:off + M_HEADS],
                                jnp.exp(acs[last:last + 1, :] - acs),
                                jnp.exp(acs)], axis=0)
    wide = _dot(per_head.astype(BF16), expand)
    xdt = x * wide[:q]
    xw = (xdt * wide[q:2 * q]).astype(BF16)
    e_acs = wide[2 * q:]

    gw = M_HPG * M_HEADDIM
    rb = lax.broadcasted_iota(jnp.int32, (M_HPG * q, gw), 0) >> CHUNK_SHIFT
    cb = lax.broadcasted_iota(jnp.int32, (M_HPG * q, gw), 1) >> HEADDIM_SHIFT
    blockdiag = rb == cb
    for g in range(M_GROUPS):
        bg = bmat[:, g * M_STATE:(g + 1) * M_STATE]
        cg = cmat[:, g * M_STATE:(g + 1) * M_STATE]
        scores = _dot_nt(cg, bg)
        ms = []
        for r in range(M_HPG):
            h = g * M_HPG + r
            seg = acs[:, h:h + 1] - acs_t[h:h + 1, :]
            decay = jnp.exp(jnp.where(mask, seg, -jnp.inf))
            ms.append((scores * decay).astype(BF16))
        m_cat = jnp.concatenate(ms, axis=1)
        xg = xdt[:, g * gw:(g + 1) * gw]
        x_bd = jnp.where(blockdiag, jnp.concatenate([xg] * M_HPG, axis=0),
                         0.0).astype(BF16)
        y_diag = _dot(m_cat, x_bd)
        s_g = state[g * M_HPG:(g + 1) * M_HPG].reshape(gw, M_STATE).astype(BF16)
        y_g = y_diag + e_acs[:, g * gw:(g + 1) * gw] * _dot_nt(cg, s_g)
        if fwd:
            y_g = y_g + dskip_row[:, g * gw:(g + 1) * gw] * x[:, g * gw:(g + 1) * gw]
        y_ref[0, :, g * gw:(g + 1) * gw] = y_g.astype(y_ref.dtype)
        upd = _dot_tn(xw[:, g * gw:(g + 1) * gw], bg)
        for r in range(M_HPG):
            h = g * M_HPG + r
            state[h] = (state[h] * e_atot_c[h:h + 1, 0:1]
                        + upd[r * M_HEADDIM:(r + 1) * M_HEADDIM, :])


def _ssd_kernel(xlf_ref, xcf_ref, xlb_ref, xcb_ref, dtlf_ref, dtcf_ref, dtlb_ref, dtcb_ref, a_ref,
                dskip_ref, yf_ref, yb_ref, state_f, state_b, *, n_ctx_chunks):
    s = pl.program_id(1)
    is_ctx = s < n_ctx_chunks

    @pl.when(s == 0)
    def _():
        state_f[...] = jnp.zeros_like(state_f)
        state_b[...] = jnp.zeros_like(state_b)

    dt_f = jnp.where(is_ctx, dtcf_ref[0], dtlf_ref[0])
    dt_b = jnp.where(is_ctx, dtcb_ref[0], dtlb_ref[0])
    a_row = -jnp.exp(a_ref[...])
    x_f = jnp.where(is_ctx, xcf_ref[0], xlf_ref[0])
    x_b = jnp.where(is_ctx, xcb_ref[0], xlb_ref[0])
    _ssd_chunk(True, x_f, dt_f, a_row, dskip_ref[...], yf_ref, state_f)
    _ssd_chunk(False, x_b, dt_b, a_row, dskip_ref[...], yb_ref, state_b)


def _ssd(cols, cols_c, dt_l, dt_c, a_r, dskip_row):
    b, l, _ = cols.shape
    lc = cols_c.shape[1]
    q = M_CHUNK
    nl, nc = l // q, lc // q

    lat_f = lambda si: jnp.maximum(si - nc, 0)
    lat_b = lambda si: nl - 1 - lat_f(si)
    ctx_f = lambda si: jnp.minimum(si, nc - 1)
    ctx_b = lambda si: nc - 1 - ctx_f(si)

    small = lambda shape: pl.BlockSpec(shape, lambda bi, si: (0, 0))
    x_spec = lambda fn: pl.BlockSpec((1, q, M_XBC), lambda bi, si: (bi, fn(si), 0))
    dt_spec = lambda fn: pl.BlockSpec((1, q, DT_PAD), lambda bi, si: (bi, fn(si), 0))
    state = pltpu.VMEM((M_HEADS, M_HEADDIM, M_STATE), F32)
    return pl.pallas_call(
        functools.partial(_ssd_kernel, n_ctx_chunks=nc),
        grid=(b, nl + nc),
        in_specs=[x_spec(lat_f), x_spec(ctx_f), x_spec(lat_b), x_spec(ctx_b),
                  dt_spec(lat_f), dt_spec(ctx_f), dt_spec(lat_b), dt_spec(ctx_b),
                  small((1, DT_PAD)), small((1, D_MODEL))],
        out_specs=[pl.BlockSpec((1, q, D_MODEL), lambda bi, si: (bi, lat_f(si), 0)),
                   pl.BlockSpec((1, q, D_MODEL), lambda bi, si: (bi, lat_b(si), 0))],
        out_shape=[jax.ShapeDtypeStruct((b, l, D_MODEL), BF16)] * 2,
        scratch_shapes=[state, state],
        compiler_params=_params("parallel", "arbitrary"),
        name="ssd",
    )(cols, cols_c, cols, cols_c, dt_l, dt_c, dt_l, dt_c, a_r, dskip_row)


def _filter_kernel(fl_ref, w1_ref, b1_ref, w2_ref, b2_ref, w3_ref, b3_ref, freq_ref, wo_hi_ref,
                   wo_lo_ref, deltas_ref, h_ref, *, seqlen):
    i = pl.program_id(0)
    tr = h_ref.shape[0]
    pos = (i * tr + lax.broadcasted_iota(jnp.int32, (tr, 1), 0))
    posf = pos.astype(F32)
    t = posf / F32(seqlen - 1)
    w = F32(2.0 * math.pi / seqlen) * posf
    ang = fl_ref[...] * w
    lane = lax.broadcasted_iota(jnp.int32, (tr, LANES), 1)
    bands = (H_EMB - 1) // 2
    feats = jnp.where(lane == 0, t,
                      jnp.where(lane <= bands, jnp.cos(ang),
                                jnp.where(lane < H_EMB, -jnp.sin(ang), 0.0)))
    freq = freq_ref[...]
    hid = jnp.sin(freq * (_dot(feats, w1_ref[...], HIGHEST) + b1_ref[...]))
    hid = jnp.sin(freq * (_dot(hid, w2_ref[...], HIGHEST) + b2_ref[...]))
    hid = jnp.sin(freq * (_dot(hid, w3_ref[...], HIGHEST) + b3_ref[...]))
    hid_hi, hid_lo = _split_bf16(hid)
    filt = (_dot(hid_hi, wo_hi_ref[...]) + _dot(hid_lo, wo_hi_ref[...])
            + _dot(hid_hi, wo_lo_ref[...]))
    window = jnp.exp(-t * deltas_ref[...])
    hw = H_WIDTH
    for blk in range(filt.shape[1] // hw):
        h_ref[:, blk * hw:(blk + 1) * hw] = filt[:, blk * hw:(blk + 1) * hw] * window


def _hyena_filters(seqlen, w1, b1, w2, b2, w3, b3, freq, w_out):
    tr = min(512, seqlen)
    bands = (H_EMB - 1) // 2
    f = jnp.linspace(1e-4, bands - 1, bands, dtype=F32)
    fl = jnp.zeros((1, LANES), F32).at[0, 1:1 + bands].set(f).at[0, 1 + bands:H_EMB].set(f)
    w1p = jnp.zeros((LANES, H_HID), F32).at[:H_EMB].set(w1)
    min_decay = math.log(H_DECAY_TARGET) / H_SLOW_DECAY_PCT
    max_decay = math.log(H_DECAY_TARGET) / H_FAST_DECAY_PCT
    deltas = jnp.abs(jnp.linspace(min_decay, max_decay, H_WIDTH, dtype=F32)).reshape(1, H_WIDTH)
    row = lambda v: v.reshape(1, -1)
    full = lambda a: pl.BlockSpec(a.shape, lambda i: (0, 0))
    args = (fl, w1p, row(b1), w2, row(b2), w3, row(b3), row(freq), *_split_bf16(w_out), deltas)
    nout = w_out.shape[1]
    return pl.pallas_call(
        functools.partial(_filter_kernel, seqlen=seqlen),
        grid=(seqlen // tr,),
        in_specs=[full(a) for a in args],
        out_specs=pl.BlockSpec((tr, nout), lambda i: (i, 0)),
        out_shape=jax.ShapeDtypeStruct((seqlen, nout), F32),
        compiler_params=_params("parallel"),
        name="hyena_filters",
    )(*args)


def _r_add(a, b):
    if a is None:
        return b
    if b is None:
        return a
    return a + b


def _r_sub(a, b):
    if b is None:
        return a
    if a is None:
        return -b
    return a - b


def _r_scale(a, s):
    if a is None or s == 0.0:
        return None
    if s == 1.0:
        return a
    if s == -1.0:
        return -a
    return a * s


def _c_mul_const(z, w):
    snap = lambda v: float(round(v)) if abs(v - round(v)) < 1e-12 else float(v)
    wr, wi = snap(w.real), snap(w.imag)
    zr, zi = z
    return (_r_sub(_r_scale(zr, wr), _r_scale(zi, wi)),
            _r_add(_r_scale(zr, wi), _r_scale(zi, wr)))


def _fft(xs, sign, first_half_only=False):
    n = len(xs)
    if n == 1:
        return list(xs)
    even, odd = _fft(xs[0::2], sign), _fft(xs[1::2], sign)
    out = [None] * n
    for k in range(n // 2):
        tr, ti = _c_mul_const(odd[k], complex(math.cos(2 * math.pi * k / n),
                                              sign * math.sin(2 * math.pi * k / n)))
        out[k] = (_r_add(even[k][0], tr), _r_add(even[k][1], ti))
        if not first_half_only:
            out[k + n // 2] = (_r_sub(even[k][0], tr), _r_sub(even[k][1], ti))
    return out[:n // 2] if first_half_only else out


def _tile_or_zero(v):
    return jnp.zeros((ROW_CHUNK, LANES), F32) if v is None else v


def _radix_fwd_kernel(z_ref, o_ref, *, real_input):
    _, _, half, rows_total, width = z_ref.shape

    def body(i, carry):
        rows = pl.ds(pl.multiple_of(i * ROW_CHUNK, ROW_CHUNK), ROW_CHUNK)
        for j in range(width // LANES):
            lanes = slice(j * LANES, (j + 1) * LANES)
            xs = []
            for t1 in range(half):
                re = z_ref[0, 0, t1, rows, lanes].astype(F32)
                im = None if real_input else z_ref[0, 1, t1, rows, lanes].astype(F32)
                xs.append((re, im))
            xs += [(None, None)] * half
            for k1, (re, im) in enumerate(_fft(xs, -1.0)):
                o_ref[0, k1, 0, rows, lanes] = _tile_or_zero(re).astype(BF16)
                o_ref[0, k1, 1, rows, lanes] = _tile_or_zero(im).astype(BF16)
        return carry

    lax.fori_loop(0, rows_total // ROW_CHUNK, body, 0)


def _radix_fwd(z, col_block, ncols, real_input, name):
    p, parts, half, m, _ = z.shape
    tt, cb = min(256, m), 256
    off = col_block * (H_WIDTH // cb)
    return pl.pallas_call(
        functools.partial(_radix_fwd_kernel, real_input=real_input),
        grid=(p, m // tt, ncols // cb),
        in_specs=[pl.BlockSpec((1, parts, half, tt, cb), lambda pi, ti, ci: (pi, 0, 0, ti, off + ci))],
        out_specs=pl.BlockSpec((1, 2 * half, 2, tt, cb), lambda pi, ti, ci: (pi, 0, 0, ti, ci)),
        out_shape=jax.ShapeDtypeStruct((p, 2 * half, 2, m, ncols), BF16),
        compiler_params=_params("parallel", "parallel", "parallel"),
        name=name,
    )(z)


def _radix_inv_kernel(sb_ref, gate_ref, zin_ref, bias_ref, o_ref):
    _, radix, _, rows_total, width = sb_ref.shape

    def body(i, carry):
        rows = pl.ds(pl.multiple_of(i * ROW_CHUNK, ROW_CHUNK), ROW_CHUNK)
        for j in range(width // LANES):
            lanes = slice(j * LANES, (j + 1) * LANES)
            ys = [(sb_ref[0, k1, 0, rows, lanes].astype(F32),
                   sb_ref[0, k1, 1, rows, lanes].astype(F32)) for k1 in range(radix)]
            bias = bias_ref[:, lanes]
            for t1, parts in enumerate(_fft(ys, 1.0, first_half_only=True)):
                for q in range(2):
                    zin = zin_ref[0, q, t1, rows, lanes].astype(F32)
                    gate = gate_ref[0, q, t1, rows, lanes].astype(F32)
                    o_ref[0, q, t1, rows, lanes] = (gate * (parts[q] + bias * zin)).astype(o_ref.dtype)
        return carry

    lax.fori_loop(0, rows_total // ROW_CHUNK, body, 0)


def _radix_inv(sb, gate_src, gate_block, zin_src, zin_block, bias_row, name):
    p, radix, _, m, ncols = sb.shape
    half = radix // 2
    tt, cb = min(256, m), 256
    goff, zoff = gate_block * (H_WIDTH // cb), zin_block * (H_WIDTH // cb)
    tok = lambda off: pl.BlockSpec((1, 2, half, tt, cb), lambda pi, ti, ci: (pi, 0, 0, ti, off + ci))
    return pl.pallas_call(
        _radix_inv_kernel,
        grid=(p, m // tt, ncols // cb),
        in_specs=[pl.BlockSpec((1, radix, 2, tt, cb), lambda pi, ti, ci: (pi, 0, 0, ti, ci)),
                  tok(goff), tok(zoff),
                  pl.BlockSpec((1, cb), lambda pi, ti, ci: (0, ci))],
        out_specs=tok(0),
        out_shape=jax.ShapeDtypeStruct((p, 2, half, m, ncols), BF16),
        compiler_params=_params("parallel", "parallel", "parallel"),
        name=name,
    )(sb, gate_src, zin_src, bias_row)


def _dft_seed_kernel(cb_ref, sb_ref, ca_ref, sa_ref, *, n):
    m = cb_ref.shape[0]
    r = lax.broadcasted_iota(jnp.int32, (m, m), 0)
    c = lax.broadcasted_iota(jnp.int32, (m, m), 1)
    beta = ((r * c) & (m - 1)).astype(F32) * F32(2.0 * math.pi / m)
    cb_ref[...] = jnp.cos(beta)
    sb_ref[...] = jnp.sin(beta)
    k1 = lax.broadcasted_iota(jnp.int32, ca_ref.shape, 0)
    t2 = lax.broadcasted_iota(jnp.int32, ca_ref.shape, 1)
    alpha = (k1 * t2).astype(F32) * F32(2.0 * math.pi / n)
    ca_ref[...] = jnp.cos(alpha)
    sa_ref[...] = jnp.sin(alpha)


def _dft_table_kernel(cb_ref, sb_ref, car_ref, sar_ref, cac_ref, sac_ref, efwd_ref, einv_ref):
    m = cb_ref.shape[0]
    cb, sb = cb_ref[...], sb_ref[...]
    car, sar = car_ref[0], sar_ref[0]
    cos_f = (car * cb - sar * sb).astype(BF16)
    sin_f = (sar * cb + car * sb).astype(BF16)
    efwd_ref[0, :m, :m] = cos_f
    efwd_ref[0, :m, m:] = sin_f
    efwd_ref[0, m:, :m] = -sin_f
    efwd_ref[0, m:, m:] = cos_f
    cac, sac = cac_ref[0], sac_ref[0]
    cos_i = (cac * cb - sac * sb).astype(BF16)
    sin_i = (sac * cb + cac * sb).astype(BF16)
    einv_ref[0, :m, :m] = cos_i
    einv_ref[0, :m, m:] = -sin_i
    einv_ref[0, m:, :m] = sin_i
    einv_ref[0, m:, m:] = cos_i


def _dft_tables(n):
    m = n // FFT_RADIX
    cb, sb, ca, sa = pl.pallas_call(
        functools.partial(_dft_seed_kernel, n=n),
        out_shape=[jax.ShapeDtypeStruct((m, m), F32)] * 2
        + [jax.ShapeDtypeStruct((FFT_RADIX, m), F32)] * 2,
        compiler_params=pltpu.CompilerParams(vmem_limit_bytes=VMEM_LIMIT),
        name="dft_seed",
    )()
    full = pl.BlockSpec((m, m), lambda i: (0, 0))
    rowspec = pl.BlockSpec((1, 1, m), lambda i: (i, 0, 0))
    colspec = pl.BlockSpec((1, m, 1), lambda i: (i, 0, 0))
    tab = pl.BlockSpec((1, 2 * m, 2 * m), lambda i: (i, 0, 0))
    return pl.pallas_call(
        _dft_table_kernel,
        grid=(FFT_RADIX,),
        in_specs=[full, full, rowspec, rowspec, colspec, colspec],
        out_specs=[tab, tab],
        out_shape=[jax.ShapeDtypeStruct((FFT_RADIX, 2 * m, 2 * m), BF16)] * 2,
        compiler_params=_params("parallel"),
        name="dft_tables",
    )(cb, sb, ca.reshape(FFT_RADIX, 1, m), sa.reshape(FFT_RADIX, 1, m),
      ca.reshape(FFT_RADIX, m, 1), sa.reshape(FFT_RADIX, m, 1))


def _filter_spectrum_kernel(efwd_ref, hf_ref, hb_ref, kr_ref, ki_ref, *, n):
    m = kr_ref.shape[1]
    xf = _dot(efwd_ref[0], hf_ref[0, 0])
    xb = _dot(efwd_ref[0], hb_ref[0, 0])
    kr_ref[0] = (xf[:m] + xb[:m]) * F32(1.0 / n)
    ki_ref[0] = (xf[m:] - xb[m:]) * F32(1.0 / n)


def _filter_spectrum(efwd, sah):
    radix, m2, _ = efwd.shape
    n = radix * m2 // 2
    orders = sah.shape[3] // (2 * H_WIDTH)
    slab = lambda off: pl.BlockSpec((1, 1, m2, H_WIDTH), lambda k1, o: (0, k1, 0, off + o))
    kspec = pl.BlockSpec((1, m2 // 2, H_WIDTH), lambda k1, o: (k1, 0, o))
    return pl.pallas_call(
        functools.partial(_filter_spectrum_kernel, n=n),
        grid=(radix, orders),
        in_specs=[pl.BlockSpec((1, m2, m2), lambda k1, o: (k1, 0, 0)), slab(0), slab(orders)],
        out_specs=[kspec, kspec],
        out_shape=[jax.ShapeDtypeStruct((radix, m2 // 2, orders * H_WIDTH), F32)] * 2,
        compiler_params=_params("parallel", "arbitrary"),
        name="filter_spectrum",
    )(efwd, sah, sah)


def _spectral_kernel(efwd_ref, einv_ref, sa_ref, kr_ref, ki_ref, sb_ref):
    slabs, m = kr_ref.shape[0], kr_ref.shape[1]
    width = sa_ref.shape[3]
    cw = min(512, width)
    for s in range(slabs):
        for j in range(width // cw):
            cols = slice(j * cw, (j + 1) * cw)
            x = _dot(efwd_ref[s], sa_ref[0, s, :, cols])
            xr, xi = x[:m], x[m:]
            kr, ki = kr_ref[s, :, cols], ki_ref[s, :, cols]
            y = jnp.concatenate([xr * kr - xi * ki, xr * ki + xi * kr], axis=0).astype(BF16)
            sb_ref[0, s, :, cols] = _dot(einv_ref[s], y).astype(BF16)


def _spectral(efwd, einv, sa, kr, ki, order):
    p, radix, m2, ncols = sa.shape
    ns = SPECTRAL_SLABS
    slab = pl.BlockSpec((1, ns, m2, ncols), lambda k1, pi: (pi, k1, 0, 0))
    tab = pl.BlockSpec((ns, m2, m2), lambda k1, pi: (k1, 0, 0))
    kspec = pl.BlockSpec((ns, m2 // 2, ncols), lambda k1, pi: (k1, 0, order))
    return pl.pallas_call(
        _spectral_kernel,
        grid=(radix // ns, p),
        in_specs=[tab, tab, slab, kspec, kspec],
        out_specs=slab,
        out_shape=jax.ShapeDtypeStruct(sa.shape, BF16),
        compiler_params=_params("parallel", "arbitrary"),
        name="hyena_spectral",
    )(efwd, einv, sa, kr, ki)


def _output_kernel(yf_ref, yb_ref, z_ref, hg_ref, gm_ref, gh_ref, yh_ref, x_ref, gate_ref, mnw_ref,
                   fnw_ref, wm_ref, wh_ref, wo_ref, o_ref):
    f32 = lambda ref: ref[0].astype(F32)
    g = (f32(yf_ref) + f32(yb_ref)) * _silu(f32(z_ref))
    gw = D_MODEL // M_GROUPS
    parts = []
    for i in range(M_GROUPS):
        gi = g[:, i * gw:(i + 1) * gw]
        parts.append(gi * lax.rsqrt(jnp.mean(gi * gi, axis=-1, keepdims=True) + NORM_EPS))
    ym = (jnp.concatenate(parts, axis=1) * mnw_ref[...]).astype(BF16)
    yh = (f32(yh_ref) * _silu(f32(hg_ref))).astype(BF16)
    merged = (jax.nn.sigmoid(f32(gm_ref)) * _dot(ym, wm_ref[...])
              + jax.nn.sigmoid(f32(gh_ref)) * _dot(yh, wh_ref[...]))
    out = _dot(merged.astype(BF16), wo_ref[...])
    xn = x_ref[0] + gate_ref[0] * out
    ms = jnp.mean(xn * xn, axis=-1, keepdims=True)
    o_ref[0] = xn * lax.rsqrt(ms + NORM_EPS) * fnw_ref[...]


def _mixer_output(y_f, y_b, cols, yh, x, gate, m_norm_w, final_norm_w, wm, wh, wo):
    b, l, d = x.shape
    tm = min(512, l)
    tok = lambda blk: pl.BlockSpec((1, tm, d), lambda bi, i: (bi, i, blk))
    wspec = pl.BlockSpec((d, d), lambda bi, i: (0, 0))
    rowspec = pl.BlockSpec((1, d), lambda bi, i: (0, 0))
    return pl.pallas_call(
        _output_kernel,
        grid=(b, l // tm),
        in_specs=[tok(0), tok(0),
                  tok(COL_Z), tok(COL_HGATE), tok(COL_GM), tok(COL_GH), tok(0), tok(0),
                  pl.BlockSpec((1, 1, d), lambda bi, i: (bi, 0, 0)),
                  rowspec, rowspec, wspec, wspec, wspec],
        out_specs=tok(0),
        out_shape=jax.ShapeDtypeStruct((b, l, d), F32),
        compiler_params=_params("parallel", "parallel"),
        name="mixer_output",
    )(y_f, y_b, cols, cols, cols, cols, yh, x, gate, m_norm_w.reshape(1, d),
      final_norm_w.reshape(1, d), wm, wh, wo)


def _layer(x, ctx, c, c_ctx, ada_w, ada_b, norm_w, w_in, m_conv_w, m_conv_b, m_dt_bias, m_a_log,
           m_d, m_norm_w, h_conv_w, h_conv_b, h_w1, h_b1, h_w2, h_b2, h_w3, h_b3, h_freq,
           h_w_out, h_bias, w_branch_m, w_branch_h, w_out, final_norm_w):
    b, l, d = x.shape
    lc = ctx.shape[1]
    ndt = 2 * M_HEADS

    pad_rows = -(b + 1) % SUBLANES
    c_rows = jnp.concatenate([c, c_ctx[None], jnp.zeros((pad_rows, d), F32)], axis=0)
    mod = _adaln(c_rows, ada_w, ada_b)
    shift, scale, gate = (mod[:b, i * d:(i + 1) * d].reshape(b, 1, d) for i in range(3))
    shift_c = jnp.broadcast_to(mod[b, :d], (b, 1, d))
    scale_c = jnp.broadcast_to(mod[b, d:2 * d], (b, 1, d))

    o_z, o_xbc, o_dt = 0, d, d + M_XBC
    o_hg = o_dt + ndt
    o_hp = o_hg + H_WIDTH
    o_gm = o_hp + 3 * H_WIDTH
    w_xbc = w_in[:, o_xbc:o_dt]
    w_main = jnp.concatenate([w_xbc, w_in[:, o_z:o_xbc], w_in[:, o_hg:]], axis=1).astype(BF16)
    w_dt_pad = jnp.zeros((d, DT_PAD), F32).at[:, :ndt].set(w_in[:, o_dt:o_hg])
    nw = norm_w.reshape(1, d)

    n_main = w_main.shape[1]
    conv_w = (jnp.zeros((3, n_main), F32).at[:, :M_XBC].set(m_conv_w)
              .at[:, COL_HPROJ * 1024:(COL_HPROJ + 3) * 1024].set(h_conv_w))
    conv_b = (jnp.zeros((1, n_main), F32).at[0, :M_XBC].set(m_conv_b)
              .at[0, COL_HPROJ * 1024:(COL_HPROJ + 3) * 1024].set(h_conv_b))
    silu_tiles, conv_tiles = (0, M_XBC // 1024), (COL_HPROJ, COL_HPROJ + 3)

    bias_r = jnp.zeros((1, DT_PAD), F32).at[0, :ndt].set(m_dt_bias.reshape(ndt))
    cols, dt_l = _inproj(x, shift, scale, nw, w_main, conv_w, conv_b, silu_tiles, conv_tiles,
                         w_dt_pad, bias_r, min(1024, l))
    cols_c, dt_c = _inproj(ctx, shift_c, scale_c, nw, w_xbc.astype(BF16), conv_w[:, :M_XBC],
                           conv_b[:, :M_XBC], silu_tiles, (0, 0), w_dt_pad, bias_r, min(1024, lc))

    a_r = jnp.zeros((1, DT_PAD), F32).at[0, :ndt].set(m_a_log.reshape(ndt))
    dskip_row = jnp.repeat(m_d, M_HEADDIM).reshape(1, d)
    y_f, y_b = _ssd(cols, cols_c, dt_l, dt_c, a_r, dskip_row)

    assert b % 2 == 0, "batch rows are packed in pairs as complex signals"
    half = FFT_RADIX // 2
    m = l // half
    pairs = b // 2
    u3 = cols.reshape(pairs, 2, half, m, n_main)
    filt = _hyena_filters(l, h_w1, h_b1, h_w2, h_b2, h_w3, h_b3, h_freq, h_w_out)
    efwd, einv = _dft_tables(2 * l)
    sah = _radix_fwd(filt.reshape(1, 1, half, m, filt.shape[1]), 0, filt.shape[1], True,
                     "filter_radix")
    kr, ki = _filter_spectrum(efwd, sah.reshape(1, FFT_RADIX, 2 * m, filt.shape[1]))
    z, z_block = u3, COL_HPROJ
    for order in range(2):
        sa = _radix_fwd(z, z_block, H_WIDTH, False, "hyena_radix_fwd")
        sb = _spectral(efwd, einv, sa.reshape(pairs, FFT_RADIX, 2 * m, H_WIDTH), kr, ki, order)
        z = _radix_inv(sb.reshape(pairs, FFT_RADIX, 2, m, H_WIDTH), u3, COL_HPROJ + 1 + order,
                       z, z_block, h_bias[order].reshape(1, H_WIDTH), "hyena_radix_inv")
        z_block = 0
    y_h = z.reshape(b, l, H_WIDTH)

    return _mixer_output(y_f, y_b, cols, y_h, x, gate, m_norm_w, final_norm_w,
                         w_branch_m.astype(BF16), w_branch_h.astype(BF16), w_out.astype(BF16))


def kernel(x, c, ctx, c_ctx, ada_w, ada_b, norm_w, w_in, m_conv_w, m_conv_b, m_dt_bias, m_a_log,
           m_d, m_norm_w, h_conv_w, h_conv_b, h_w1, h_b1, h_w2, h_b2, h_w3, h_b3, h_freq,
           h_w_out, h_bias, w_branch_m, w_branch_h, w_out, final_norm_w):
    assert ada_w.shape[0] == 1, "one trunk layer"
    return _layer(x, ctx, c, c_ctx, ada_w[0], ada_b[0], norm_w[0], w_in[0], m_conv_w[0],
                  m_conv_b[0], m_dt_bias[0], m_a_log[0], m_d[0], m_norm_w[0], h_conv_w[0],
                  h_conv_b[0], h_w1[0], h_b1[0], h_w2[0], h_b2[0], h_w3[0], h_b3[0], h_freq[0],
                  h_w_out[0], h_bias[0], w_branch_m[0], w_branch_h[0], w_out[0], final_norm_w)
```

```python
import functools
import math

import jax
import jax.numpy as jnp
from jax import lax
from jax.experimental import pallas as pl
from jax.experimental.pallas import tpu as pltpu

F32 = jnp.float32
BF16 = jnp.bfloat16
HIGHEST = lax.Precision.HIGHEST

LANES = 128
SUBLANES = 8
VMEM_LIMIT = 56 * 1024 * 1024

NORM_EPS = 1e-6
D_MODEL = 1024
M_HEADDIM = 64
M_HEADS = 16
M_GROUPS = 4
M_HPG = M_HEADS // M_GROUPS
M_STATE = 128
M_XBC = D_MODEL + 2 * M_GROUPS * M_STATE
M_CHUNK = 256
HEADDIM_SHIFT = M_HEADDIM.bit_length() - 1
CHUNK_SHIFT = M_CHUNK.bit_length() - 1
H_WIDTH = 1024
H_EMB = 33
H_HID = 64
H_DECAY_TARGET = 1e-2
H_FAST_DECAY_PCT = 0.3
H_SLOW_DECAY_PCT = 1.5
DT_PAD = LANES
FFT_RADIX = 32
ROW_CHUNK = 16
SPECTRAL_SLABS = 4
PROJ_SLAB = 256

COL_XBC, COL_Z, COL_HGATE, COL_HPROJ, COL_GM, COL_GH = 0, 2, 3, 4, 7, 8
N_MAIN = 9 * 1024


def _dot(a, b, precision=None):
    return jnp.dot(a, b, preferred_element_type=F32, precision=precision)


def _dot_nt(a, b, precision=None):
    return lax.dot_general(a, b, (((1,), (1,)), ((), ())), preferred_element_type=F32,
                           precision=precision)


def _dot_tn(a, b, precision=None):
    return lax.dot_general(a, b, (((0,), (0,)), ((), ())), preferred_element_type=F32,
                           precision=precision)


def _params(*sem):
    return pltpu.CompilerParams(dimension_semantics=sem, vmem_limit_bytes=VMEM_LIMIT)


def _silu(v):
    return v * jax.nn.sigmoid(v)


def _adaln_kernel(c_ref, w_ref, b_ref, o_ref):
    o_ref[...] = _dot(_silu(c_ref[...]), w_ref[...], HIGHEST) + b_ref[...]


def _adaln(c_rows, ada_w, ada_b):
    rows, d = c_rows.shape
    n = ada_w.shape[1]
    tn = 1024
    return pl.pallas_call(
        _adaln_kernel,
        grid=(n // tn,),
        in_specs=[pl.BlockSpec((rows, d), lambda j: (0, 0)),
                  pl.BlockSpec((d, tn), lambda j: (0, j)),
                  pl.BlockSpec((1, tn), lambda j: (0, j))],
        out_specs=pl.BlockSpec((rows, tn), lambda j: (0, j)),
        out_shape=jax.ShapeDtypeStruct((rows, n), F32),
        compiler_params=_params("parallel"),
        name="adaln",
    )(c_rows, ada_w, ada_b.reshape(1, n))


def _split_bf16(v):
    hi = v.astype(BF16)
    return hi, (v - hi.astype(F32)).astype(BF16)


def _softplus(v):
    return jnp.maximum(v, 0.0) + jnp.log1p(jnp.exp(-jnp.abs(v)))


def _conv_rows(p_ext, tm, w, bias, has_prev, has_next, apply_silu, rows_ref):
    u = p_ext[0:tm, :]
    lo = SUBLANES
    rows_ref[lo:lo + tm, :] = u
    rows_ref[lo - 1:lo, :] = jnp.where(has_prev, p_ext[tm + SUBLANES - 1:tm + SUBLANES, :], 0.0)
    rows_ref[lo + tm:lo + tm + 1, :] = jnp.where(has_next,
                                                 p_ext[tm + SUBLANES:tm + SUBLANES + 1, :], 0.0)
    up = rows_ref[lo - 1:lo - 1 + tm, :]
    un = rows_ref[lo + 1:lo + 1 + tm, :]
    y = w[0:1, :] * up + w[1:2, :] * u + w[2:3, :] * un + bias
    return _silu(y) if apply_silu else y


def _inproj_kernel(x_ref, xprev_ref, xnext_ref, shift_ref, scale_ref, nw_ref, w_ref, cw_ref,
                   cb_ref, wdt_hi_ref, wdt_lo_ref, dtb_ref, o_ref, dt_ref, h_scr, rows_scr,
                   *, silu_tiles, conv_tiles):
    i = pl.program_id(1)
    j = pl.program_id(2)
    tm = x_ref.shape[1]

    def modulated(xt):
        ms = jnp.mean(xt * xt, axis=-1, keepdims=True)
        hn = xt * lax.rsqrt(ms + NORM_EPS) * nw_ref[...]
        return hn * (1.0 + scale_ref[0]) + shift_ref[0]

    @pl.when(j == 0)
    def _():
        hi, lo = _split_bf16(modulated(x_ref[0]))
        h_scr[0:tm, :] = hi
        halo = jnp.concatenate([xprev_ref[0], xnext_ref[0]], axis=0)
        h_scr[tm:, :] = modulated(halo).astype(BF16)
        dt_raw = (_dot(hi, wdt_hi_ref[...]) + _dot(lo, wdt_hi_ref[...])
                  + _dot(hi, wdt_lo_ref[...]))
        dt_ref[0] = _softplus(dt_raw + dtb_ref[...])

    in_range = lambda rng: jnp.logical_and(j >= rng[0], j < rng[1])
    is_silu, is_conv = in_range(silu_tiles), in_range(conv_tiles)
    has_prev, has_next = i > 0, i < pl.num_programs(1) - 1

    def project(mode):
        for c in range(o_ref.shape[2] // PROJ_SLAB):
            cols = slice(c * PROJ_SLAB, (c + 1) * PROJ_SLAB)
            p_ext = _dot(h_scr[...], w_ref[:, cols])
            if mode == "plain":
                y = p_ext[0:tm, :]
            else:
                y = _conv_rows(p_ext, tm, cw_ref[:, cols], cb_ref[:, cols], has_prev, has_next,
                               mode == "silu", rows_scr.at[c % 2])
            o_ref[0, :, cols] = y.astype(o_ref.dtype)

    pl.when(jnp.logical_not(jnp.logical_or(is_silu, is_conv)))(lambda: project("plain"))
    pl.when(is_silu)(lambda: project("silu"))
    pl.when(is_conv)(lambda: project("conv"))


def _inproj(x, shift, scale, norm_w, w_main, conv_w, conv_b, silu_tiles, conv_tiles, w_dt,
            dt_bias, tm):
    b, l, d = x.shape
    n = w_main.shape[1]
    tn = 1024
    nsub = tm // SUBLANES
    last = l // SUBLANES - 1
    w_dt_hi, w_dt_lo = _split_bf16(w_dt)
    return pl.pallas_call(
        functools.partial(_inproj_kernel, silu_tiles=silu_tiles, conv_tiles=conv_tiles),
        grid=(b, l // tm, n // tn),
        in_specs=[pl.BlockSpec((1, tm, d), lambda bi, i, j: (bi, i, 0)),
                  pl.BlockSpec((1, SUBLANES, d),
                               lambda bi, i, j: (bi, jnp.maximum(i * nsub - 1, 0), 0)),
                  pl.BlockSpec((1, SUBLANES, d),
                               lambda bi, i, j: (bi, jnp.minimum((i + 1) * nsub, last), 0)),
                  pl.BlockSpec((1, 1, d), lambda bi, i, j: (bi, 0, 0)),
                  pl.BlockSpec((1, 1, d), lambda bi, i, j: (bi, 0, 0)),
                  pl.BlockSpec((1, d), lambda bi, i, j: (0, 0)),
                  pl.BlockSpec((d, tn), lambda bi, i, j: (0, j)),
                  pl.BlockSpec((3, tn), lambda bi, i, j: (0, j)),
                  pl.BlockSpec((1, tn), lambda bi, i, j: (0, j)),
                  pl.BlockSpec((d, DT_PAD), lambda bi, i, j: (0, 0)),
                  pl.BlockSpec((d, DT_PAD), lambda bi, i, j: (0, 0)),
                  pl.BlockSpec((1, DT_PAD), lambda bi, i, j: (0, 0))],
        out_specs=[pl.BlockSpec((1, tm, tn), lambda bi, i, j: (bi, i, j)),
                   pl.BlockSpec((1, tm, DT_PAD), lambda bi, i, j: (bi, i, 0))],
        out_shape=[jax.ShapeDtypeStruct((b, l, n), BF16),
                   jax.ShapeDtypeStruct((b, l, DT_PAD), F32)],
        scratch_shapes=[pltpu.VMEM((tm + 2 * SUBLANES, d), BF16),
                        pltpu.VMEM((2, tm + 2 * SUBLANES, PROJ_SLAB), F32)],
        compiler_params=_params("parallel", "parallel", "arbitrary"),
        name="inproj",
    )(x, x, x, shift, scale, norm_w, w_main, conv_w, conv_b, w_dt_hi, w_dt_lo, dt_bias)


def _cumsum_dot(tri_bf16, v):
    hi = v.astype(BF16)
    rest = v - hi.astype(F32)
    mid = rest.astype(BF16)
    lo = (rest - mid.astype(F32)).astype(BF16)
    return _dot(tri_bf16, hi) + _dot(tri_bf16, mid) + _dot(tri_bf16, lo)


def _expand_heads(v, e_bf16):
    hi = v.astype(BF16)
    lo = (v - hi.astype(F32)).astype(BF16)
    return _dot(hi, e_bf16) + _dot(lo, e_bf16)


def _ssd_chunk(fwd, xbc, dt_all, a_row, dskip_row, y_ref, state):
    q = M_CHUNK
    off = 0 if fwd else M_HEADS
    last = q - 1 if fwd else 0
    x = xbc[:, :D_MODEL].astype(F32)
    bmat = xbc[:, D_MODEL:D_MODEL + M_GROUPS * M_STATE]
    cmat = xbc[:, D_MODEL + M_GROUPS * M_STATE:]

    a_all = dt_all * a_row
    ri = lax.broadcasted_iota(jnp.int32, (q, q), 0)
    ci = lax.broadcasted_iota(jnp.int32, (q, q), 1)
    mask = (ri >= ci) if fwd else (ri <= ci)
    acs_all = _cumsum_dot(jnp.where(mask, 1.0, 0.0).astype(BF16), a_all)
    acs_t = acs_all.T[off:off + M_HEADS, :]
    acs = acs_all[:, off:off + M_HEADS]
    e_atot_c = jnp.exp(acs_t[:, last:last + 1])

    hid = lax.broadcasted_iota(jnp.int32, (M_HEADS, D_MODEL), 0)
    lid = lax.broadcasted_iota(jnp.int32, (M_HEADS, D_MODEL), 1)
    expand = jnp.where((lid >> HEADDIM_SHIFT) == hid, 1.0, 0.0).astype(BF16)
    per_head = jnp.concatenate([dt_all[:, off:off + M_HEADS],
                                jnp.exp(acs[last:last + 1, :] - acs),
                                jnp.exp(acs)], axis=0)
    wide = _dot(per_head.astype(BF16), expand)
    xdt = x * wide[:q]
    xw = (xdt * wide[q:2 * q]).astype(BF16)
    e_acs = wide[2 * q:]

    gw = M_HPG * M_HEADDIM
    hb = q // 2
    halves = (slice(0, hb), slice(hb, q))
    far_rows, far_cols = (halves[1], halves[0]) if fwd else (halves[0], halves[1])
    tri = mask[:hb, :hb]
    rb = lax.broadcasted_iota(jnp.int32, (M_HPG * hb, gw), 0) >> (CHUNK_SHIFT - 1)
    cb = lax.broadcasted_iota(jnp.int32, (M_HPG * hb, gw), 1) >> HEADDIM_SHIFT
    blockdiag = rb == cb
    for g in range(M_GROUPS):
        bg = bmat[:, g * M_STATE:(g + 1) * M_STATE]
        cg = cmat[:, g * M_STATE:(g + 1) * M_STATE]
        scores = _dot_nt(cg, bg)
        near, far = ([], []), []
        for r in range(M_HPG):
            h = g * M_HPG + r
            col, row = acs[:, h:h + 1], acs_t[h:h + 1, :]
            for k, sl in enumerate(halves):
                decay = jnp.exp(jnp.where(tri, col[sl] - row[:, sl], -jnp.inf))
                near[k].append((scores[sl, sl] * decay).astype(BF16))
            decay = jnp.exp(col[far_rows] - row[:, far_cols])
            far.append((scores[far_rows, far_cols] * decay).astype(BF16))
        xg = xdt[:, g * gw:(g + 1) * gw]
        x_bd = [jnp.where(blockdiag, jnp.concatenate([xg[sl]] * M_HPG, axis=0), 0.0).astype(BF16)
                for sl in halves]
        x_all = jnp.concatenate(x_bd, axis=0)
        if fwd:
            y_top = _dot(jnp.concatenate(near[0], axis=1), x_bd[0])
            y_bot = _dot(jnp.concatenate(far + near[1], axis=1), x_all)
        else:
            y_top = _dot(jnp.concatenate(near[0] + far, axis=1), x_all)
            y_bot = _dot(jnp.concatenate(near[1], axis=1), x_bd[1])
        y_diag = jnp.concatenate([y_top, y_bot], axis=0)
        s_g = state[g * M_HPG:(g + 1) * M_HPG].reshape(gw, M_STATE).astype(BF16)
        y_g = y_diag + e_acs[:, g * gw:(g + 1) * gw] * _dot_nt(cg, s_g)
        if fwd:
            y_g = y_g + dskip_row[:, g * gw:(g + 1) * gw] * x[:, g * gw:(g + 1) * gw]
        y_ref[0, :, g * gw:(g + 1) * gw] = y_g.astype(y_ref.dtype)
        upd = _dot_tn(xw[:, g * gw:(g + 1) * gw], bg)
        for r in range(M_HPG):
            h = g * M_HPG + r
            state[h] = (state[h] * e_atot_c[h:h + 1, 0:1]
                        + upd[r * M_HEADDIM:(r + 1) * M_HEADDIM, :])


def _ssd_kernel(xlf_ref, xcf_ref, xlb_ref, xcb_ref, dtlf_ref, dtcf_ref, dtlb_ref, dtcb_ref, a_ref,
                dskip_ref, yf_ref, yb_ref, state_f, state_b, *, n_ctx_chunks):
    s = pl.program_id(1)
    is_ctx = s < n_ctx_chunks

    @pl.when(s == 0)
    def _():
        state_f[...] = jnp.zeros_like(state_f)
        state_b[...] = jnp.zeros_like(state_b)

    dt_f = jnp.where(is_ctx, dtcf_ref[0], dtlf_ref[0])
    dt_b = jnp.where(is_ctx, dtcb_ref[0], dtlb_ref[0])
    a_row = -jnp.exp(a_ref[...])
    x_f = jnp.where(is_ctx, xcf_ref[0], xlf_ref[0])
    x_b = jnp.where(is_ctx, xcb_ref[0], xlb_ref[0])
    _ssd_chunk(True, x_f, dt_f, a_row, dskip_ref[...], yf_ref, state_f)
    _ssd_chunk(False, x_b, dt_b, a_row, dskip_ref[...], yb_ref, state_b)


def _ssd(cols, cols_c, dt_l, dt_c, a_r, dskip_row):
    b, l, _ = cols.shape
    lc = cols_c.shape[1]
    q = M_CHUNK
    nl, nc = l // q, lc // q

    lat_f = lambda si: jnp.maximum(si - nc, 0)
    lat_b = lambda si: nl - 1 - lat_f(si)
    ctx_f = lambda si: jnp.minimum(si, nc - 1)
    ctx_b = lambda si: nc - 1 - ctx_f(si)

    small = lambda shape: pl.BlockSpec(shape, lambda bi, si: (0, 0))
    x_spec = lambda fn: pl.BlockSpec((1, q, M_XBC), lambda bi, si: (bi, fn(si), 0))
    dt_spec = lambda fn: pl.BlockSpec((1, q, DT_PAD), lambda bi, si: (bi, fn(si), 0))
    state = pltpu.VMEM((M_HEADS, M_HEADDIM, M_STATE), F32)
    return pl.pallas_call(
        functools.partial(_ssd_kernel, n_ctx_chunks=nc),
        grid=(b, nl + nc),
        in_specs=[x_spec(lat_f), x_spec(ctx_f), x_spec(lat_b), x_spec(ctx_b),
                  dt_spec(lat_f), dt_spec(ctx_f), dt_spec(lat_b), dt_spec(ctx_b),
                  small((1, DT_PAD)), small((1, D_MODEL))],
        out_specs=[pl.BlockSpec((1, q, D_MODEL), lambda bi, si: (bi, lat_f(si), 0)),
                   pl.BlockSpec((1, q, D_MODEL), lambda bi, si: (bi, lat_b(si), 0))],
        out_shape=[jax.ShapeDtypeStruct((b, l, D_MODEL), BF16)] * 2,
        scratch_shapes=[state, state],
        compiler_params=_params("parallel", "arbitrary"),
        name="ssd",
    )(cols, cols_c, cols, cols_c, dt_l, dt_c, dt_l, dt_c, a_r, dskip_row)


def _filter_kernel(fl_ref, w1_ref, b1_ref, w2_ref, b2_ref, w3_ref, b3_ref, freq_ref, wo_hi_ref,
                   wo_lo_ref, deltas_ref, h_ref, *, seqlen):
    i = pl.program_id(0)
    tr = h_ref.shape[0]
    pos = (i * tr + lax.broadcasted_iota(jnp.int32, (tr, 1), 0))
    posf = pos.astype(F32)
    t = posf / F32(seqlen - 1)
    w = F32(2.0 * math.pi / seqlen) * posf
    ang = fl_ref[...] * w
    lane = lax.broadcasted_iota(jnp.int32, (tr, LANES), 1)
    bands = (H_EMB - 1) // 2
    feats = jnp.where(lane == 0, t,
                      jnp.where(lane <= bands, jnp.cos(ang),
                                jnp.where(lane < H_EMB, -jnp.sin(ang), 0.0)))
    freq = freq_ref[...]
    hid = jnp.sin(freq * (_dot(feats, w1_ref[...], HIGHEST) + b1_ref[...]))
    hid = jnp.sin(freq * (_dot(hid, w2_ref[...], HIGHEST) + b2_ref[...]))
    hid = jnp.sin(freq * (_dot(hid, w3_ref[...], HIGHEST) + b3_ref[...]))
    hid_hi, hid_lo = _split_bf16(hid)
    filt = (_dot(hid_hi, wo_hi_ref[...]) + _dot(hid_lo, wo_hi_ref[...])
            + _dot(hid_hi, wo_lo_ref[...]))
    window = jnp.exp(-t * deltas_ref[...])
    hw = H_WIDTH
    for blk in range(filt.shape[1] // hw):
        h_ref[:, blk * hw:(blk + 1) * hw] = filt[:, blk * hw:(blk + 1) * hw] * window


def _hyena_filters(seqlen, w1, b1, w2, b2, w3, b3, freq, w_out):
    tr = min(512, seqlen)
    bands = (H_EMB - 1) // 2
    f = jnp.linspace(1e-4, bands - 1, bands, dtype=F32)
    fl = jnp.zeros((1, LANES), F32).at[0, 1:1 + bands].set(f).at[0, 1 + bands:H_EMB].set(f)
    w1p = jnp.zeros((LANES, H_HID), F32).at[:H_EMB].set(w1)
    min_decay = math.log(H_DECAY_TARGET) / H_SLOW_DECAY_PCT
    max_decay = math.log(H_DECAY_TARGET) / H_FAST_DECAY_PCT
    deltas = jnp.abs(jnp.linspace(min_decay, max_decay, H_WIDTH, dtype=F32)).reshape(1, H_WIDTH)
    row = lambda v: v.reshape(1, -1)
    full = lambda a: pl.BlockSpec(a.shape, lambda i: (0, 0))
    args = (fl, w1p, row(b1), w2, row(b2), w3, row(b3), row(freq), *_split_bf16(w_out), deltas)
    nout = w_out.shape[1]
    return pl.pallas_call(
        functools.partial(_filter_kernel, seqlen=seqlen),
        grid=(seqlen // tr,),
        in_specs=[full(a) for a in args],
        out_specs=pl.BlockSpec((tr, nout), lambda i: (i, 0)),
        out_shape=jax.ShapeDtypeStruct((seqlen, nout), F32),
        compiler_params=_params("parallel"),
        name="hyena_filters",
    )(*args)


class _Neg:
    def __init__(self, arr):
        self.arr = arr


def _r_neg(a):
    if a is None:
        return None
    return a.arr if isinstance(a, _Neg) else _Neg(a)


def _r_add(a, b):
    if a is None:
        return b
    if b is None:
        return a
    if isinstance(a, _Neg) and isinstance(b, _Neg):
        return _Neg(a.arr + b.arr)
    if isinstance(b, _Neg):
        return a - b.arr
    if isinstance(a, _Neg):
        return b - a.arr
    return a + b


def _r_sub(a, b):
    return _r_add(a, _r_neg(b))


def _r_scale(a, s):
    if a is None or s == 0.0:
        return None
    if isinstance(a, _Neg):
        a, s = a.arr, -s
    if s == 1.0:
        return a
    if s == -1.0:
        return _Neg(a)
    return a * s


def _r_value(a):
    return -a.arr if isinstance(a, _Neg) else a


def _c_mul_const(z, w):
    snap = lambda v: float(round(v)) if abs(v - round(v)) < 1e-12 else float(v)
    wr, wi = snap(w.real), snap(w.imag)
    zr, zi = z
    if wr != 0.0 and abs(abs(wr) - abs(wi)) < 1e-12:
        sr, si, c = math.copysign(1.0, wr), math.copysign(1.0, wi), abs(wr)
        return (_r_scale(_r_sub(_r_scale(zr, sr), _r_scale(zi, si)), c),
                _r_scale(_r_add(_r_scale(zr, si), _r_scale(zi, sr)), c))
    return (_r_sub(_r_scale(zr, wr), _r_scale(zi, wi)),
            _r_add(_r_scale(zr, wi), _r_scale(zi, wr)))


def _fft(xs, sign, first_half_only=False):
    n = len(xs)
    if n == 1:
        return list(xs)
    even, odd = _fft(xs[0::2], sign), _fft(xs[1::2], sign)
    out = [None] * n
    for k in range(n // 2):
        tr, ti = _c_mul_const(odd[k], complex(math.cos(2 * math.pi * k / n),
                                              sign * math.sin(2 * math.pi * k / n)))
        out[k] = (_r_add(even[k][0], tr), _r_add(even[k][1], ti))
        if not first_half_only:
            out[k + n // 2] = (_r_sub(even[k][0], tr), _r_sub(even[k][1], ti))
    return out[:n // 2] if first_half_only else out


def _tile_or_zero(v):
    return jnp.zeros((ROW_CHUNK, LANES), F32) if v is None else _r_value(v)


def _radix_fwd_kernel(z_ref, o_ref, *, real_input):
    _, _, half, rows_total, width = z_ref.shape

    def body(i, carry):
        rows = pl.ds(pl.multiple_of(i * ROW_CHUNK, ROW_CHUNK), ROW_CHUNK)
        for j in range(width // LANES):
            lanes = slice(j * LANES, (j + 1) * LANES)
            xs = []
            for t1 in range(half):
                re = z_ref[0, 0, t1, rows, lanes].astype(F32)
                im = None if real_input else z_ref[0, 1, t1, rows, lanes].astype(F32)
                xs.append((re, im))
            xs += [(None, None)] * half
            for k1, (re, im) in enumerate(_fft(xs, -1.0)):
                o_ref[0, k1, 0, rows, lanes] = _tile_or_zero(re).astype(BF16)
                o_ref[0, k1, 1, rows, lanes] = _tile_or_zero(im).astype(BF16)
        return carry

    lax.fori_loop(0, rows_total // ROW_CHUNK, body, 0)


def _radix_fwd(z, col_block, ncols, real_input, name):
    p, parts, half, m, _ = z.shape
    tt, cb = min(256, m), 256
    off = col_block * (H_WIDTH // cb)
    return pl.pallas_call(
        functools.partial(_radix_fwd_kernel, real_input=real_input),
        grid=(p, m // tt, ncols // cb),
        in_specs=[pl.BlockSpec((1, parts, half, tt, cb), lambda pi, ti, ci: (pi, 0, 0, ti, off + ci))],
        out_specs=pl.BlockSpec((1, 2 * half, 2, tt, cb), lambda pi, ti, ci: (pi, 0, 0, ti, ci)),
        out_shape=jax.ShapeDtypeStruct((p, 2 * half, 2, m, ncols), BF16),
        compiler_params=_params("parallel", "parallel", "parallel"),
        name=name,
    )(z)


def _radix_inv_kernel(sb_ref, gate_ref, zin_ref, bias_ref, o_ref):
    _, radix, _, rows_total, width = sb_ref.shape

    def body(i, carry):
        rows = pl.ds(pl.multiple_of(i * ROW_CHUNK, ROW_CHUNK), ROW_CHUNK)
        for j in range(width // LANES):
            lanes = slice(j * LANES, (j + 1) * LANES)
            ys = [(sb_ref[0, k1, 0, rows, lanes].astype(F32),
                   sb_ref[0, k1, 1, rows, lanes].astype(F32)) for k1 in range(radix)]
            bias = bias_ref[:, lanes]
            for t1, parts in enumerate(_fft(ys, 1.0, first_half_only=True)):
                for q in range(2):
                    zin = zin_ref[0, q, t1, rows, lanes].astype(F32)
                    gate = gate_ref[0, q, t1, rows, lanes].astype(F32)
                    o_ref[0, q, t1, rows, lanes] = (
                        gate * _r_add(bias * zin, parts[q])).astype(o_ref.dtype)
        return carry

    lax.fori_loop(0, rows_total // ROW_CHUNK, body, 0)


def _radix_inv(sb, gate_src, gate_block, zin_src, zin_block, bias_row, name):
    p, radix, _, m, ncols = sb.shape
    half = radix // 2
    tt, cb = min(256, m), 256
    goff, zoff = gate_block * (H_WIDTH // cb), zin_block * (H_WIDTH // cb)
    tok = lambda off: pl.BlockSpec((1, 2, half, tt, cb), lambda pi, ti, ci: (pi, 0, 0, ti, off + ci))
    return pl.pallas_call(
        _radix_inv_kernel,
        grid=(p, m // tt, ncols // cb),
        in_specs=[pl.BlockSpec((1, radix, 2, tt, cb), lambda pi, ti, ci: (pi, 0, 0, ti, ci)),
                  tok(goff), tok(zoff),
                  pl.BlockSpec((1, cb), lambda pi, ti, ci: (0, ci))],
        out_specs=tok(0),
        out_shape=jax.ShapeDtypeStruct((p, 2, half, m, ncols), BF16),
        compiler_params=_params("parallel", "parallel", "parallel"),
        name=name,
    )(sb, gate_src, zin_src, bias_row)


def _dft_seed_kernel(cb_ref, sb_ref, ca_ref, sa_ref, *, n):
    m = cb_ref.shape[0]
    r = lax.broadcasted_iota(jnp.int32, (m, m), 0)
    c = lax.broadcasted_iota(jnp.int32, (m, m), 1)
    beta = ((r * c) & (m - 1)).astype(F32) * F32(2.0 * math.pi / m)
    cb_ref[...] = jnp.cos(beta)
    sb_ref[...] = jnp.sin(beta)
    k1 = lax.broadcasted_iota(jnp.int32, ca_ref.shape, 0)
    t2 = lax.broadcasted_iota(jnp.int32, ca_ref.shape, 1)
    alpha = (k1 * t2).astype(F32) * F32(2.0 * math.pi / n)
    ca_ref[...] = jnp.cos(alpha)
    sa_ref[...] = jnp.sin(alpha)


def _dft_table_kernel(cb_ref, sb_ref, car_ref, sar_ref, cac_ref, sac_ref, efwd_ref, einv_ref):
    m = cb_ref.shape[0]
    cb, sb = cb_ref[...], sb_ref[...]
    car, sar = car_ref[0], sar_ref[0]
    cos_f = (car * cb - sar * sb).astype(BF16)
    sin_f = (sar * cb + car * sb).astype(BF16)
    efwd_ref[0, :m, :m] = cos_f
    efwd_ref[0, :m, m:] = sin_f
    efwd_ref[0, m:, :m] = -sin_f
    efwd_ref[0, m:, m:] = cos_f
    cac, sac = cac_ref[0], sac_ref[0]
    cos_i = (cac * cb - sac * sb).astype(BF16)
    sin_i = (sac * cb + cac * sb).astype(BF16)
    einv_ref[0, :m, :m] = cos_i
    einv_ref[0, :m, m:] = -sin_i
    einv_ref[0, m:, :m] = sin_i
    einv_ref[0, m:, m:] = cos_i


def _dft_tables(n):
    m = n // FFT_RADIX
    cb, sb, ca, sa = pl.pallas_call(
        functools.partial(_dft_seed_kernel, n=n),
        out_shape=[jax.ShapeDtypeStruct((m, m), F32)] * 2
        + [jax.ShapeDtypeStruct((FFT_RADIX, m), F32)] * 2,
        compiler_params=pltpu.CompilerParams(vmem_limit_bytes=VMEM_LIMIT),
        name="dft_seed",
    )()
    full = pl.BlockSpec((m, m), lambda i: (0, 0))
    rowspec = pl.BlockSpec((1, 1, m), lambda i: (i, 0, 0))
    colspec = pl.BlockSpec((1, m, 1), lambda i: (i, 0, 0))
    tab = pl.BlockSpec((1, 2 * m, 2 * m), lambda i: (i, 0, 0))
    return pl.pallas_call(
        _dft_table_kernel,
        grid=(FFT_RADIX,),
        in_specs=[full, full, rowspec, rowspec, colspec, colspec],
        out_specs=[tab, tab],
        out_shape=[jax.ShapeDtypeStruct((FFT_RADIX, 2 * m, 2 * m), BF16)] * 2,
        compiler_params=_params("parallel"),
        name="dft_tables",
    )(cb, sb, ca.reshape(FFT_RADIX, 1, m), sa.reshape(FFT_RADIX, 1, m),
      ca.reshape(FFT_RADIX, m, 1), sa.reshape(FFT_RADIX, m, 1))


def _filter_spectrum_kernel(efwd_ref, hf_ref, hb_ref, kr_ref, ki_ref, *, n):
    m = kr_ref.shape[1]
    xf = _dot(efwd_ref[0], hf_ref[0, 0])
    xb = _dot(efwd_ref[0], hb_ref[0, 0])
    kr_ref[0] = (xf[:m] + xb[:m]) * F32(1.0 / n)
    ki_ref[0] = (xf[m:] - xb[m:]) * F32(1.0 / n)


def _filter_spectrum(efwd, sah):
    radix, m2, _ = efwd.shape
    n = radix * m2 // 2
    orders = sah.shape[3] // (2 * H_WIDTH)
    slab = lambda off: pl.BlockSpec((1, 1, m2, H_WIDTH), lambda k1, o: (0, k1, 0, off + o))
    kspec = pl.BlockSpec((1, m2 // 2, H_WIDTH), lambda k1, o: (k1, 0, o))
    return pl.pallas_call(
        functools.partial(_filter_spectrum_kernel, n=n),
        grid=(radix, orders),
        in_specs=[pl.BlockSpec((1, m2, m2), lambda k1, o: (k1, 0, 0)), slab(0), slab(orders)],
        out_specs=[kspec, kspec],
        out_shape=[jax.ShapeDtypeStruct((radix, m2 // 2, orders * H_WIDTH), F32)] * 2,
        compiler_params=_params("parallel", "arbitrary"),
        name="filter_spectrum",
    )(efwd, sah, sah)


def _spectral_kernel(efwd_ref, einv_ref, sa_ref, kr_ref, ki_ref, sb_ref):
    slabs, m = kr_ref.shape[0], kr_ref.shape[1]
    width = sa_ref.shape[3]
    cw = min(512, width)
    for s in range(slabs):
        for j in range(width // cw):
            cols = slice(j * cw, (j + 1) * cw)
            x = _dot(efwd_ref[s], sa_ref[0, s, :, cols])
            xr, xi = x[:m], x[m:]
            kr, ki = kr_ref[s, :, cols], ki_ref[s, :, cols]
            y = jnp.concatenate([xr * kr - xi * ki, xr * ki + xi * kr], axis=0).astype(BF16)
            sb_ref[0, s, :, cols] = _dot(einv_ref[s], y).astype(BF16)


def _spectral(efwd, einv, sa, kr, ki, order):
    p, radix, m2, ncols = sa.shape
    ns = SPECTRAL_SLABS
    slab = pl.BlockSpec((1, ns, m2, ncols), lambda k1, pi: (pi, k1, 0, 0))
    tab = pl.BlockSpec((ns, m2, m2), lambda k1, pi: (k1, 0, 0))
    kspec = pl.BlockSpec((ns, m2 // 2, ncols), lambda k1, pi: (k1, 0, order))
    return pl.pallas_call(
        _spectral_kernel,
        grid=(radix // ns, p),
        in_specs=[tab, tab, slab, kspec, kspec],
        out_specs=slab,
        out_shape=jax.ShapeDtypeStruct(sa.shape, BF16),
        compiler_params=_params("parallel", "arbitrary"),
        name="hyena_spectral",
    )(efwd, einv, sa, kr, ki)


def _output_kernel(yf_ref, yb_ref, z_ref, hg_ref, gm_ref, gh_ref, yh_ref, x_ref, gate_ref, mnw_ref,
                   fnw_ref, wm_ref, wh_ref, wo_ref, o_ref):
    f32 = lambda ref: ref[0].astype(F32)
    g = (f32(yf_ref) + f32(yb_ref)) * _silu(f32(z_ref))
    gw = D_MODEL // M_GROUPS
    parts = []
    for i in range(M_GROUPS):
        gi = g[:, i * gw:(i + 1) * gw]
        parts.append(gi * lax.rsqrt(jnp.mean(gi * gi, axis=-1, keepdims=True) + NORM_EPS))
    ym = (jnp.concatenate(parts, axis=1) * mnw_ref[...]).astype(BF16)
    yh = (f32(yh_ref) * _silu(f32(hg_ref))).astype(BF16)
    merged = (jax.nn.sigmoid(f32(gm_ref)) * _dot(ym, wm_ref[...])
              + jax.nn.sigmoid(f32(gh_ref)) * _dot(yh, wh_ref[...]))
    out = _dot(merged.astype(BF16), wo_ref[...])
    xn = x_ref[0] + gate_ref[0] * out
    ms = jnp.mean(xn * xn, axis=-1, keepdims=True)
    o_ref[0] = xn * lax.rsqrt(ms + NORM_EPS) * fnw_ref[...]


def _mixer_output(y_f, y_b, cols, yh, x, gate, m_norm_w, final_norm_w, wm, wh, wo):
    b, l, d = x.shape
    tm = min(512, l)
    tok = lambda blk: pl.BlockSpec((1, tm, d), lambda bi, i: (bi, i, blk))
    wspec = pl.BlockSpec((d, d), lambda bi, i: (0, 0))
    rowspec = pl.BlockSpec((1, d), lambda bi, i: (0, 0))
    return pl.pallas_call(
        _output_kernel,
        grid=(b, l // tm),
        in_specs=[tok(0), tok(0),
                  tok(COL_Z), tok(COL_HGATE), tok(COL_GM), tok(COL_GH), tok(0), tok(0),
                  pl.BlockSpec((1, 1, d), lambda bi, i: (bi, 0, 0)),
                  rowspec, rowspec, wspec, wspec, wspec],
        out_specs=tok(0),
        out_shape=jax.ShapeDtypeStruct((b, l, d), F32),
        compiler_params=_params("parallel", "parallel"),
        name="mixer_output",
    )(y_f, y_b, cols, cols, cols, cols, yh, x, gate, m_norm_w.reshape(1, d),
      final_norm_w.reshape(1, d), wm, wh, wo)


def _layer(x, ctx, c, c_ctx, ada_w, ada_b, norm_w, w_in, m_conv_w, m_conv_b, m_dt_bias, m_a_log,
           m_d, m_norm_w, h_conv_w, h_conv_b, h_w1, h_b1, h_w2, h_b2, h_w3, h_b3, h_freq,
           h_w_out, h_bias, w_branch_m, w_branch_h, w_out, final_norm_w):
    b, l, d = x.shape
    lc = ctx.shape[1]
    ndt = 2 * M_HEADS

    pad_rows = -(b + 1) % SUBLANES
    c_rows = jnp.concatenate([c, c_ctx[None], jnp.zeros((pad_rows, d), F32)], axis=0)
    mod = _adaln(c_rows, ada_w, ada_b)
    shift, scale, gate = (mod[:b, i * d:(i + 1) * d].reshape(b, 1, d) for i in range(3))
    shift_c = jnp.broadcast_to(mod[b, :d], (b, 1, d))
    scale_c = jnp.broadcast_to(mod[b, d:2 * d], (b, 1, d))

    o_z, o_xbc, o_dt = 0, d, d + M_XBC
    o_hg = o_dt + ndt
    o_hp = o_hg + H_WIDTH
    o_gm = o_hp + 3 * H_WIDTH
    w_xbc = w_in[:, o_xbc:o_dt]
    w_main = jnp.concatenate([w_xbc, w_in[:, o_z:o_xbc], w_in[:, o_hg:]], axis=1).astype(BF16)
    w_dt_pad = jnp.zeros((d, DT_PAD), F32).at[:, :ndt].set(w_in[:, o_dt:o_hg])
    nw = norm_w.reshape(1, d)

    n_main = w_main.shape[1]
    conv_w = (jnp.zeros((3, n_main), F32).at[:, :M_XBC].set(m_conv_w)
              .at[:, COL_HPROJ * 1024:(COL_HPROJ + 3) * 1024].set(h_conv_w))
    conv_b = (jnp.zeros((1, n_main), F32).at[0, :M_XBC].set(m_conv_b)
              .at[0, COL_HPROJ * 1024:(COL_HPROJ + 3) * 1024].set(h_conv_b))
    silu_tiles, conv_tiles = (0, M_XBC // 1024), (COL_HPROJ, COL_HPROJ + 3)

    bias_r = jnp.zeros((1, DT_PAD), F32).at[0, :ndt].set(m_dt_bias.reshape(ndt))
    cols, dt_l = _inproj(x, shift, scale, nw, w_main, conv_w, conv_b, silu_tiles, conv_tiles,
                         w_dt_pad, bias_r, min(1024, l))
    cols_c, dt_c = _inproj(ctx, shift_c, scale_c, nw, w_xbc.astype(BF16), conv_w[:, :M_XBC],
                           conv_b[:, :M_XBC], silu_tiles, (0, 0), w_dt_pad, bias_r, min(1024, lc))

    a_r = jnp.zeros((1, DT_PAD), F32).at[0, :ndt].set(m_a_log.reshape(ndt))
    dskip_row = jnp.repeat(m_d, M_HEADDIM).reshape(1, d)
    y_f, y_b = _ssd(cols, cols_c, dt_l, dt_c, a_r, dskip_row)

    assert b % 2 == 0, "batch rows are packed in pairs as complex signals"
    half = FFT_RADIX // 2
    m = l // half
    pairs = b // 2
    u3 = cols.reshape(pairs, 2, half, m, n_main)
    filt = _hyena_filters(l, h_w1, h_b1, h_w2, h_b2, h_w3, h_b3, h_freq, h_w_out)
    efwd, einv = _dft_tables(2 * l)
    sah = _radix_fwd(filt.reshape(1, 1, half, m, filt.shape[1]), 0, filt.shape[1], True,
                     "filter_radix")
    kr, ki = _filter_spectrum(efwd, sah.reshape(1, FFT_RADIX, 2 * m, filt.shape[1]))
    z, z_block = u3, COL_HPROJ
    for order in range(2):
        sa = _radix_fwd(z, z_block, H_WIDTH, False, "hyena_radix_fwd")
        sb = _spectral(efwd, einv, sa.reshape(pairs, FFT_RADIX, 2 * m, H_WIDTH), kr, ki, order)
        z = _radix_inv(sb.reshape(pairs, FFT_RADIX, 2, m, H_WIDTH), u3, COL_HPROJ + 1 + order,
                       z, z_block, h_bias[order].reshape(1, H_WIDTH), "hyena_radix_inv")
        z_block = 0
    y_h = z.reshape(b, l, H_WIDTH)

    return _mixer_output(y_f, y_b, cols, y_h, x, gate, m_norm_w, final_norm_w,
                         w_branch_m.astype(BF16), w_branch_h.astype(BF16), w_out.astype(BF16))


def kernel(x, c, ctx, c_ctx, ada_w, ada_b, norm_w, w_in, m_conv_w, m_conv_b, m_dt_bias, m_a_log,
           m_d, m_norm_w, h_conv_w, h_conv_b, h_w1, h_b1, h_w2, h_b2, h_w3, h_b3, h_freq,
           h_w_out, h_bias, w_branch_m, w_branch_h, w_out, final_norm_w):
    assert ada_w.shape[0] == 1, "one trunk layer"
    return _layer(x, ctx, c, c_ctx, ada_w[0], ada_b[0], norm_w[0], w_in[0], m_conv_w[0],
                  m_conv_b[0], m_dt_bias[0], m_a_log[0], m_d[0], m_norm_w[0], h_conv_w[0],
                  h_conv_b[0], h_w1[0], h_b1[0], h_w2[0], h_b2[0], h_w3[0], h_b3[0], h_freq[0],
                  h_w_out[0], h_bias[0], w_branch_m[0], w_branch_h[0], w_out[0], final_norm_w)
```

```python
import functools
import math

import jax
import jax.numpy as jnp
from jax import lax
from jax.experimental import pallas as pl
from jax.experimental.pallas import tpu as pltpu

F32 = jnp.float32
BF16 = jnp.bfloat16
HIGHEST = lax.Precision.HIGHEST

LANES = 128
SUBLANES = 8
VMEM_LIMIT = 56 * 1024 * 1024

NORM_EPS = 1e-6
D_MODEL = 1024
M_HEADDIM = 64
M_HEADS = 16
M_GROUPS = 4
M_HPG = M_HEADS // M_GROUPS
M_STATE = 128
M_XBC = D_MODEL + 2 * M_GROUPS * M_STATE
M_CHUNK = 256
HEADDIM_SHIFT = M_HEADDIM.bit_length() - 1
CHUNK_SHIFT = M_CHUNK.bit_length() - 1
H_WIDTH = 1024
H_EMB = 33
H_HID = 64
H_DECAY_TARGET = 1e-2
H_FAST_DECAY_PCT = 0.3
H_SLOW_DECAY_PCT = 1.5
DT_PAD = LANES
FFT_RADIX = 32
ROW_CHUNK = 16
SPECTRAL_SLABS = 4
PROJ_SLAB = 256

COL_HPROJ = 2
COL_Z, COL_HGATE, COL_GM, COL_GH = 0, 1, 2, 3
CONV_TN = 1024
PLAIN_TN = 2048


def _dot(a, b, precision=None):
    return jnp.dot(a, b, preferred_element_type=F32, precision=precision)


def _dot_nt(a, b, precision=None):
    return lax.dot_general(a, b, (((1,), (1,)), ((), ())), preferred_element_type=F32,
                           precision=precision)


def _dot_tn(a, b, precision=None):
    return lax.dot_general(a, b, (((0,), (0,)), ((), ())), preferred_element_type=F32,
                           precision=precision)


def _params(*sem):
    return pltpu.CompilerParams(dimension_semantics=sem, vmem_limit_bytes=VMEM_LIMIT)


def _silu(v):
    return v * jax.nn.sigmoid(v)


def _adaln_kernel(c_ref, w_ref, b_ref, o_ref):
    o_ref[...] = _dot(_silu(c_ref[...]), w_ref[...], HIGHEST) + b_ref[...]


def _adaln(c_rows, ada_w, ada_b):
    rows, d = c_rows.shape
    n = ada_w.shape[1]
    tn = 1024
    return pl.pallas_call(
        _adaln_kernel,
        grid=(n // tn,),
        in_specs=[pl.BlockSpec((rows, d), lambda j: (0, 0)),
                  pl.BlockSpec((d, tn), lambda j: (0, j)),
                  pl.BlockSpec((1, tn), lambda j: (0, j))],
        out_specs=pl.BlockSpec((rows, tn), lambda j: (0, j)),
        out_shape=jax.ShapeDtypeStruct((rows, n), F32),
        compiler_params=_params("parallel"),
        name="adaln",
    )(c_rows, ada_w, ada_b.reshape(1, n))


def _split_bf16(v):
    hi = v.astype(BF16)
    return hi, (v - hi.astype(F32)).astype(BF16)


def _softplus(v):
    return jnp.maximum(v, 0.0) + jnp.log1p(jnp.exp(-jnp.abs(v)))


def _conv_rows(p_ext, tm, w, bias, has_prev, has_next, apply_silu, rows_ref):
    u = p_ext[0:tm, :]
    lo = SUBLANES
    rows_ref[lo:lo + tm, :] = u
    rows_ref[lo - 1:lo, :] = jnp.where(has_prev, p_ext[tm + SUBLANES - 1:tm + SUBLANES, :], 0.0)
    rows_ref[lo + tm:lo + tm + 1, :] = jnp.where(has_next,
                                                 p_ext[tm + SUBLANES:tm + SUBLANES + 1, :], 0.0)
    up = rows_ref[lo - 1:lo - 1 + tm, :]
    un = rows_ref[lo + 1:lo + 1 + tm, :]
    y = w[0:1, :] * up + w[1:2, :] * u + w[2:3, :] * un + bias
    return _silu(y) if apply_silu else y


def _inproj_kernel(*refs, n_silu, n_conv, has_plain):
    if has_plain:
        (x_ref, xprev_ref, xnext_ref, shift_ref, scale_ref, nw_ref, wc_ref, cw_ref, cb_ref, wp_ref,
         wdt_hi_ref, wdt_lo_ref, dtb_ref, oc_ref, op_ref, dt_ref, h_scr, rows_scr) = refs
    else:
        (x_ref, xprev_ref, xnext_ref, shift_ref, scale_ref, nw_ref, wc_ref, cw_ref, cb_ref,
         wdt_hi_ref, wdt_lo_ref, dtb_ref, oc_ref, dt_ref, h_scr, rows_scr) = refs
    i = pl.program_id(1)
    j = pl.program_id(2)
    tm = x_ref.shape[1]

    def modulated(xt):
        ms = jnp.mean(xt * xt, axis=-1, keepdims=True)
        hn = xt * lax.rsqrt(ms + NORM_EPS) * nw_ref[...]
        return hn * (1.0 + scale_ref[0]) + shift_ref[0]

    @pl.when(j == 0)
    def _():
        hi, lo = _split_bf16(modulated(x_ref[0]))
        h_scr[0:tm, :] = hi
        halo = jnp.concatenate([xprev_ref[0], xnext_ref[0]], axis=0)
        h_scr[tm:, :] = modulated(halo).astype(BF16)
        dt_raw = (_dot(hi, wdt_hi_ref[...]) + _dot(lo, wdt_hi_ref[...])
                  + _dot(hi, wdt_lo_ref[...]))
        dt_ref[0] = _softplus(dt_raw + dtb_ref[...])

    has_prev, has_next = i > 0, i < pl.num_programs(1) - 1

    def project(mode, w_ref, o_ref):
        for c in range(o_ref.shape[2] // PROJ_SLAB):
            cols = slice(c * PROJ_SLAB, (c + 1) * PROJ_SLAB)
            p_ext = _dot(h_scr[...], w_ref[:, cols])
            if mode == "plain":
                y = p_ext[0:tm, :]
            else:
                y = _conv_rows(p_ext, tm, cw_ref[:, cols], cb_ref[:, cols], has_prev, has_next,
                               mode == "silu", rows_scr.at[c % 2])
            o_ref[0, :, cols] = y.astype(o_ref.dtype)

    pl.when(j < n_silu)(lambda: project("silu", wc_ref, oc_ref))
    if n_conv > n_silu:
        pl.when(jnp.logical_and(j >= n_silu, j < n_conv))(lambda: project("conv", wc_ref, oc_ref))
    if has_plain:
        pl.when(j >= n_conv)(lambda: project("plain", wp_ref, op_ref))


def _inproj(x, shift, scale, norm_w, w_conv, conv_w, conv_b, n_silu, w_plain, w_dt, dt_bias, tm):
    b, l, d = x.shape
    nct = w_conv.shape[1] // CONV_TN
    npt = 0 if w_plain is None else w_plain.shape[1] // PLAIN_TN
    nsub = tm // SUBLANES
    last = l // SUBLANES - 1
    w_dt_hi, w_dt_lo = _split_bf16(w_dt)
    conv_j = lambda j: jnp.minimum(j, nct - 1)
    plain_j = lambda j: jnp.maximum(j - nct, 0)
    const = lambda shape: pl.BlockSpec(shape, lambda bi, i, j: tuple(0 for _ in shape))
    in_specs = [pl.BlockSpec((1, tm, d), lambda bi, i, j: (bi, i, 0)),
                pl.BlockSpec((1, SUBLANES, d),
                             lambda bi, i, j: (bi, jnp.maximum(i * nsub - 1, 0), 0)),
                pl.BlockSpec((1, SUBLANES, d),
                             lambda bi, i, j: (bi, jnp.minimum((i + 1) * nsub, last), 0)),
                pl.BlockSpec((1, 1, d), lambda bi, i, j: (bi, 0, 0)),
                pl.BlockSpec((1, 1, d), lambda bi, i, j: (bi, 0, 0)),
                const((1, d)),
                pl.BlockSpec((d, CONV_TN), lambda bi, i, j: (0, conv_j(j))),
                pl.BlockSpec((3, CONV_TN), lambda bi, i, j: (0, conv_j(j))),
                pl.BlockSpec((1, CONV_TN), lambda bi, i, j: (0, conv_j(j)))]
    args = [x, x, x, shift, scale, norm_w, w_conv, conv_w, conv_b]
    out_specs = [pl.BlockSpec((1, tm, CONV_TN), lambda bi, i, j: (bi, i, conv_j(j)))]
    out_shape = [jax.ShapeDtypeStruct((b, l, w_conv.shape[1]), BF16)]
    if npt:
        in_specs.append(pl.BlockSpec((d, PLAIN_TN), lambda bi, i, j: (0, plain_j(j))))
        args.append(w_plain)
        out_specs.append(pl.BlockSpec((1, tm, PLAIN_TN), lambda bi, i, j: (bi, i, plain_j(j))))
        out_shape.append(jax.ShapeDtypeStruct((b, l, w_plain.shape[1]), BF16))
    in_specs += [const((d, DT_PAD)), const((d, DT_PAD)), const((1, DT_PAD))]
    args += [w_dt_hi, w_dt_lo, dt_bias]
    out_specs.append(pl.BlockSpec((1, tm, DT_PAD), lambda bi, i, j: (bi, i, 0)))
    out_shape.append(jax.ShapeDtypeStruct((b, l, DT_PAD), F32))
    outs = pl.pallas_call(
        functools.partial(_inproj_kernel, n_silu=n_silu, n_conv=nct, has_plain=npt > 0),
        grid=(b, l // tm, nct + npt),
        in_specs=in_specs,
        out_specs=out_specs,
        out_shape=out_shape,
        scratch_shapes=[pltpu.VMEM((tm + 2 * SUBLANES, d), BF16),
                        pltpu.VMEM((2, tm + 2 * SUBLANES, PROJ_SLAB), F32)],
        compiler_params=_params("parallel", "parallel", "arbitrary"),
        name="inproj",
    )(*args)
    return (outs[0], outs[1], outs[2]) if npt else (outs[0], None, outs[1])


def _cumsum_dot(tri_bf16, v):
    hi = v.astype(BF16)
    rest = v - hi.astype(F32)
    mid = rest.astype(BF16)
    lo = (rest - mid.astype(F32)).astype(BF16)
    return _dot(tri_bf16, hi) + _dot(tri_bf16, mid) + _dot(tri_bf16, lo)


def _expand_heads(v, e_bf16):
    hi = v.astype(BF16)
    lo = (v - hi.astype(F32)).astype(BF16)
    return _dot(hi, e_bf16) + _dot(lo, e_bf16)


def _ssd_chunk(fwd, xbc, dt_all, a_row, dskip_row, y_ref, state):
    q = M_CHUNK
    off = 0 if fwd else M_HEADS
    last = q - 1 if fwd else 0
    x = xbc[:, :D_MODEL].astype(F32)
    bmat = xbc[:, D_MODEL:D_MODEL + M_GROUPS * M_STATE]
    cmat = xbc[:, D_MODEL + M_GROUPS * M_STATE:]

    a_all = dt_all * a_row
    ri = lax.broadcasted_iota(jnp.int32, (q, q), 0)
    ci = lax.broadcasted_iota(jnp.int32, (q, q), 1)
    mask = (ri >= ci) if fwd else (ri <= ci)
    acs_all = _cumsum_dot(jnp.where(mask, 1.0, 0.0).astype(BF16), a_all)
    acs_t = acs_all.T[off:off + M_HEADS, :]
    acs = acs_all[:, off:off + M_HEADS]
    e_atot_c = jnp.exp(acs_t[:, last:last + 1])

    hid = lax.broadcasted_iota(jnp.int32, (M_HEADS, D_MODEL), 0)
    lid = lax.broadcasted_iota(jnp.int32, (M_HEADS, D_MODEL), 1)
    expand = jnp.where((lid >> HEADDIM_SHIFT) == hid, 1.0, 0.0).astype(BF16)
    per_head = jnp.concatenate([dt_all[:, off:off + M_HEADS],
                                jnp.exp(acs[last:last + 1, :] - acs),
                                jnp.exp(acs)], axis=0)
    wide = _dot(per_head.astype(BF16), expand)
    xdt = x * wide[:q]
    xw = (xdt * wide[q:2 * q]).astype(BF16)
    e_acs = wide[2 * q:]

    gw = M_HPG * M_HEADDIM
    hb = q // 2
    halves = (slice(0, hb), slice(hb, q))
    far_rows, far_cols = (halves[1], halves[0]) if fwd else (halves[0], halves[1])
    tri = mask[:hb, :hb]
    rb = lax.broadcasted_iota(jnp.int32, (M_HPG * hb, gw), 0) >> (CHUNK_SHIFT - 1)
    cb = lax.broadcasted_iota(jnp.int32, (M_HPG * hb, gw), 1) >> HEADDIM_SHIFT
    blockdiag = rb == cb
    for g in range(M_GROUPS):
        bg = bmat[:, g * M_STATE:(g + 1) * M_STATE]
        cg = cmat[:, g * M_STATE:(g + 1) * M_STATE]
        scores = _dot_nt(cg, bg)
        near, far = ([], []), []
        for r in range(M_HPG):
            h = g * M_HPG + r
            col, row = acs[:, h:h + 1], acs_t[h:h + 1, :]
            for k, sl in enumerate(halves):
                decay = jnp.exp(jnp.where(tri, col[sl] - row[:, sl], -jnp.inf))
                near[k].append((scores[sl, sl] * decay).astype(BF16))
            decay = jnp.exp(col[far_rows] - row[:, far_cols])
            far.append((scores[far_rows, far_cols] * decay).astype(BF16))
        xg = xdt[:, g * gw:(g + 1) * gw]
        x_bd = [jnp.where(blockdiag, jnp.concatenate([xg[sl]] * M_HPG, axis=0), 0.0).astype(BF16)
                for sl in halves]
        x_all = jnp.concatenate(x_bd, axis=0)
        if fwd:
            y_top = _dot(jnp.concatenate(near[0], axis=1), x_bd[0])
            y_bot = _dot(jnp.concatenate(far + near[1], axis=1), x_all)
        else:
            y_top = _dot(jnp.concatenate(near[0] + far, axis=1), x_all)
            y_bot = _dot(jnp.concatenate(near[1], axis=1), x_bd[1])
        y_diag = jnp.concatenate([y_top, y_bot], axis=0)
        s_g = state[g * M_HPG:(g + 1) * M_HPG].reshape(gw, M_STATE).astype(BF16)
        y_g = y_diag + e_acs[:, g * gw:(g + 1) * gw] * _dot_nt(cg, s_g)
        if fwd:
            y_g = y_g + dskip_row[:, g * gw:(g + 1) * gw] * x[:, g * gw:(g + 1) * gw]
        y_ref[0, :, g * gw:(g + 1) * gw] = y_g.astype(y_ref.dtype)
        upd = _dot_tn(xw[:, g * gw:(g + 1) * gw], bg)
        for r in range(M_HPG):
            h = g * M_HPG + r
            state[h] = (state[h] * e_atot_c[h:h + 1, 0:1]
                        + upd[r * M_HEADDIM:(r + 1) * M_HEADDIM, :])


def _ssd_kernel(xlf_ref, xcf_ref, xlb_ref, xcb_ref, dtlf_ref, dtcf_ref, dtlb_ref, dtcb_ref, a_ref,
                dskip_ref, yf_ref, yb_ref, state_f, state_b, *, n_ctx_chunks):
    s = pl.program_id(1)
    is_ctx = s < n_ctx_chunks

    @pl.when(s == 0)
    def _():
        state_f[...] = jnp.zeros_like(state_f)
        state_b[...] = jnp.zeros_like(state_b)

    dt_f = jnp.where(is_ctx, dtcf_ref[0], dtlf_ref[0])
    dt_b = jnp.where(is_ctx, dtcb_ref[0], dtlb_ref[0])
    a_row = -jnp.exp(a_ref[...])
    x_f = jnp.where(is_ctx, xcf_ref[0], xlf_ref[0])
    x_b = jnp.where(is_ctx, xcb_ref[0], xlb_ref[0])
    _ssd_chunk(True, x_f, dt_f, a_row, dskip_ref[...], yf_ref, state_f)
    _ssd_chunk(False, x_b, dt_b, a_row, dskip_ref[...], yb_ref, state_b)


def _ssd(cols, cols_c, dt_l, dt_c, a_r, dskip_row):
    b, l, _ = cols.shape
    lc = cols_c.shape[1]
    q = M_CHUNK
    nl, nc = l // q, lc // q

    lat_f = lambda si: jnp.maximum(si - nc, 0)
    lat_b = lambda si: nl - 1 - lat_f(si)
    ctx_f = lambda si: jnp.minimum(si, nc - 1)
    ctx_b = lambda si: nc - 1 - ctx_f(si)

    small = lambda shape: pl.BlockSpec(shape, lambda bi, si: (0, 0))
    x_spec = lambda fn: pl.BlockSpec((1, q, M_XBC), lambda bi, si: (bi, fn(si), 0))
    dt_spec = lambda fn: pl.BlockSpec((1, q, DT_PAD), lambda bi, si: (bi, fn(si), 0))
    state = pltpu.VMEM((M_HEADS, M_HEADDIM, M_STATE), F32)
    return pl.pallas_call(
        functools.partial(_ssd_kernel, n_ctx_chunks=nc),
        grid=(b, nl + nc),
        in_specs=[x_spec(lat_f), x_spec(ctx_f), x_spec(lat_b), x_spec(ctx_b),
                  dt_spec(lat_f), dt_spec(ctx_f), dt_spec(lat_b), dt_spec(ctx_b),
                  small((1, DT_PAD)), small((1, D_MODEL))],
        out_specs=[pl.BlockSpec((1, q, D_MODEL), lambda bi, si: (bi, lat_f(si), 0)),
                   pl.BlockSpec((1, q, D_MODEL), lambda bi, si: (bi, lat_b(si), 0))],
        out_shape=[jax.ShapeDtypeStruct((b, l, D_MODEL), BF16)] * 2,
        scratch_shapes=[state, state],
        compiler_params=_params("parallel", "arbitrary"),
        name="ssd",
    )(cols, cols_c, cols, cols_c, dt_l, dt_c, dt_l, dt_c, a_r, dskip_row)


def _filter_kernel(fl_ref, w1_ref, b1_ref, w2_ref, b2_ref, w3_ref, b3_ref, freq_ref, wo_hi_ref,
                   wo_lo_ref, deltas_ref, h_ref, *, seqlen):
    i = pl.program_id(0)
    tr = h_ref.shape[0]
    pos = (i * tr + lax.broadcasted_iota(jnp.int32, (tr, 1), 0))
    posf = pos.astype(F32)
    t = posf / F32(seqlen - 1)
    w = F32(2.0 * math.pi / seqlen) * posf
    ang = fl_ref[...] * w
    lane = lax.broadcasted_iota(jnp.int32, (tr, LANES), 1)
    bands = (H_EMB - 1) // 2
    feats = jnp.where(lane == 0, t,
                      jnp.where(lane <= bands, jnp.cos(ang),
                                jnp.where(lane < H_EMB, -jnp.sin(ang), 0.0)))
    freq = freq_ref[...]
    hid = jnp.sin(freq * (_dot(feats, w1_ref[...], HIGHEST) + b1_ref[...]))
    hid = jnp.sin(freq * (_dot(hid, w2_ref[...], HIGHEST) + b2_ref[...]))
    hid = jnp.sin(freq * (_dot(hid, w3_ref[...], HIGHEST) + b3_ref[...]))
    hid_hi, hid_lo = _split_bf16(hid)
    filt = (_dot(hid_hi, wo_hi_ref[...]) + _dot(hid_lo, wo_hi_ref[...])
            + _dot(hid_hi, wo_lo_ref[...]))
    window = jnp.exp(-t * deltas_ref[...])
    hw = H_WIDTH
    for blk in range(filt.shape[1] // hw):
        h_ref[:, blk * hw:(blk + 1) * hw] = filt[:, blk * hw:(blk + 1) * hw] * window


def _hyena_filters(seqlen, w1, b1, w2, b2, w3, b3, freq, w_out):
    tr = min(512, seqlen)
    bands = (H_EMB - 1) // 2
    f = jnp.linspace(1e-4, bands - 1, bands, dtype=F32)
    fl = jnp.zeros((1, LANES), F32).at[0, 1:1 + bands].set(f).at[0, 1 + bands:H_EMB].set(f)
    w1p = jnp.zeros((LANES, H_HID), F32).at[:H_EMB].set(w1)
    min_decay = math.log(H_DECAY_TARGET) / H_SLOW_DECAY_PCT
    max_decay = math.log(H_DECAY_TARGET) / H_FAST_DECAY_PCT
    deltas = jnp.abs(jnp.linspace(min_decay, max_decay, H_WIDTH, dtype=F32)).reshape(1, H_WIDTH)
    row = lambda v: v.reshape(1, -1)
    full = lambda a: pl.BlockSpec(a.shape, lambda i: (0, 0))
    args = (fl, w1p, row(b1), w2, row(b2), w3, row(b3), row(freq), *_split_bf16(w_out), deltas)
    nout = w_out.shape[1]
    return pl.pallas_call(
        functools.partial(_filter_kernel, seqlen=seqlen),
        grid=(seqlen // tr,),
        in_specs=[full(a) for a in args],
        out_specs=pl.BlockSpec((tr, nout), lambda i: (i, 0)),
        out_shape=jax.ShapeDtypeStruct((seqlen, nout), F32),
        compiler_params=_params("parallel"),
        name="hyena_filters",
    )(*args)


class _Neg:
    def __init__(self, arr):
        self.arr = arr


def _r_neg(a):
    if a is None:
        return None
    return a.arr if isinstance(a, _Neg) else _Neg(a)


def _r_add(a, b):
    if a is None:
        return b
    if b is None:
        return a
    if isinstance(a, _Neg) and isinstance(b, _Neg):
        return _Neg(a.arr + b.arr)
    if isinstance(b, _Neg):
        return a - b.arr
    if isinstance(a, _Neg):
        return b - a.arr
    return a + b


def _r_sub(a, b):
    return _r_add(a, _r_neg(b))


def _r_scale(a, s):
    if a is None or s == 0.0:
        return None
    if isinstance(a, _Neg):
        a, s = a.arr, -s
    if s == 1.0:
        return a
    if s == -1.0:
        return _Neg(a)
    return a * s


def _r_value(a):
    return -a.arr if isinstance(a, _Neg) else a


def _c_mul_const(z, w):
    snap = lambda v: float(round(v)) if abs(v - round(v)) < 1e-12 else float(v)
    wr, wi = snap(w.real), snap(w.imag)
    zr, zi = z
    if wr != 0.0 and abs(abs(wr) - abs(wi)) < 1e-12:
        sr, si, c = math.copysign(1.0, wr), math.copysign(1.0, wi), abs(wr)
        return (_r_scale(_r_sub(_r_scale(zr, sr), _r_scale(zi, si)), c),
                _r_scale(_r_add(_r_scale(zr, si), _r_scale(zi, sr)), c))
    return (_r_sub(_r_scale(zr, wr), _r_scale(zi, wi)),
            _r_add(_r_scale(zr, wi), _r_scale(zi, wr)))


def _fft(xs, sign, first_half_only=False):
    n = len(xs)
    if n == 1:
        return list(xs)
    even, odd = _fft(xs[0::2], sign), _fft(xs[1::2], sign)
    out = [None] * n
    for k in range(n // 2):
        tr, ti = _c_mul_const(odd[k], complex(math.cos(2 * math.pi * k / n),
                                              sign * math.sin(2 * math.pi * k / n)))
        out[k] = (_r_add(even[k][0], tr), _r_add(even[k][1], ti))
        if not first_half_only:
            out[k + n // 2] = (_r_sub(even[k][0], tr), _r_sub(even[k][1], ti))
    return out[:n // 2] if first_half_only else out


def _tile_or_zero(v):
    return jnp.zeros((ROW_CHUNK, LANES), F32) if v is None else _r_value(v)


def _radix_fwd_kernel(z_ref, o_ref, *, real_input):
    _, _, half, rows_total, width = z_ref.shape

    def body(i, carry):
        rows = pl.ds(pl.multiple_of(i * ROW_CHUNK, ROW_CHUNK), ROW_CHUNK)
        for j in range(width // LANES):
            lanes = slice(j * LANES, (j + 1) * LANES)
            xs = []
            for t1 in range(half):
                re = z_ref[0, 0, t1, rows, lanes].astype(F32)
                im = None if real_input else z_ref[0, 1, t1, rows, lanes].astype(F32)
                xs.append((re, im))
            xs += [(None, None)] * half
            for k1, (re, im) in enumerate(_fft(xs, -1.0)):
                o_ref[0, k1, 0, rows, lanes] = _tile_or_zero(re).astype(BF16)
                o_ref[0, k1, 1, rows, lanes] = _tile_or_zero(im).astype(BF16)
        return carry

    lax.fori_loop(0, rows_total // ROW_CHUNK, body, 0)


def _radix_fwd(z, col_block, ncols, real_input, name):
    p, parts, half, m, _ = z.shape
    tt, cb = min(256, m), 256
    off = col_block * (H_WIDTH // cb)
    return pl.pallas_call(
        functools.partial(_radix_fwd_kernel, real_input=real_input),
        grid=(p, m // tt, ncols // cb),
        in_specs=[pl.BlockSpec((1, parts, half, tt, cb), lambda pi, ti, ci: (pi, 0, 0, ti, off + ci))],
        out_specs=pl.BlockSpec((1, 2 * half, 2, tt, cb), lambda pi, ti, ci: (pi, 0, 0, ti, ci)),
        out_shape=jax.ShapeDtypeStruct((p, 2 * half, 2, m, ncols), BF16),
        compiler_params=_params("parallel", "parallel", "parallel"),
        name=name,
    )(z)


def _radix_inv_kernel(sb_ref, gate_ref, zin_ref, bias_ref, o_ref):
    _, radix, _, rows_total, width = sb_ref.shape

    def body(i, carry):
        rows = pl.ds(pl.multiple_of(i * ROW_CHUNK, ROW_CHUNK), ROW_CHUNK)
        for j in range(width // LANES):
            lanes = slice(j * LANES, (j + 1) * LANES)
            ys = [(sb_ref[0, k1, 0, rows, lanes].astype(F32),
                   sb_ref[0, k1, 1, rows, lanes].astype(F32)) for k1 in range(radix)]
            bias = bias_ref[:, lanes]
            for t1, parts in enumerate(_fft(ys, 1.0, first_half_only=True)):
                for q in range(2):
                    zin = zin_ref[0, q, t1, rows, lanes].astype(F32)
                    gate = gate_ref[0, q, t1, rows, lanes].astype(F32)
                    o_ref[0, q, t1, rows, lanes] = (
                        gate * _r_add(bias * zin, parts[q])).astype(o_ref.dtype)
        return carry

    lax.fori_loop(0, rows_total // ROW_CHUNK, body, 0)


def _radix_inv(sb, gate_src, gate_block, zin_src, zin_block, bias_row, name):
    p, radix, _, m, ncols = sb.shape
    half = radix // 2
    tt, cb = min(256, m), 256
    goff, zoff = gate_block * (H_WIDTH // cb), zin_block * (H_WIDTH // cb)
    tok = lambda off: pl.BlockSpec((1, 2, half, tt, cb), lambda pi, ti, ci: (pi, 0, 0, ti, off + ci))
    return pl.pallas_call(
        _radix_inv_kernel,
        grid=(p, m // tt, ncols // cb),
        in_specs=[pl.BlockSpec((1, radix, 2, tt, cb), lambda pi, ti, ci: (pi, 0, 0, ti, ci)),
                  tok(goff), tok(zoff),
                  pl.BlockSpec((1, cb), lambda pi, ti, ci: (0, ci))],
        out_specs=tok(0),
        out_shape=jax.ShapeDtypeStruct((p, 2, half, m, ncols), BF16),
        compiler_params=_params("parallel", "parallel", "parallel"),
        name=name,
    )(sb, gate_src, zin_src, bias_row)


def _dft_seed_kernel(cb_ref, sb_ref, ca_ref, sa_ref, *, n):
    m = cb_ref.shape[0]
    r = lax.broadcasted_iota(jnp.int32, (m, m), 0)
    c = lax.broadcasted_iota(jnp.int32, (m, m), 1)
    beta = ((r * c) & (m - 1)).astype(F32) * F32(2.0 * math.pi / m)
    cb_ref[...] = jnp.cos(beta)
    sb_ref[...] = jnp.sin(beta)
    k1 = lax.broadcasted_iota(jnp.int32, ca_ref.shape, 0)
    t2 = lax.broadcasted_iota(jnp.int32, ca_ref.shape, 1)
    alpha = (k1 * t2).astype(F32) * F32(2.0 * math.pi / n)
    ca_ref[...] = jnp.cos(alpha)
    sa_ref[...] = jnp.sin(alpha)


def _dft_table_kernel(cb_ref, sb_ref, car_ref, sar_ref, cac_ref, sac_ref, efwd_ref, einv_ref):
    m = cb_ref.shape[0]
    cb, sb = cb_ref[...], sb_ref[...]
    car, sar = car_ref[0], sar_ref[0]
    cos_f = (car * cb - sar * sb).astype(BF16)
    sin_f = (sar * cb + car * sb).astype(BF16)
    efwd_ref[0, :m, :m] = cos_f
    efwd_ref[0, :m, m:] = sin_f
    efwd_ref[0, m:, :m] = -sin_f
    efwd_ref[0, m:, m:] = cos_f
    cac, sac = cac_ref[0], sac_ref[0]
    cos_i = (cac * cb - sac * sb).astype(BF16)
    sin_i = (sac * cb + cac * sb).astype(BF16)
    einv_ref[0, :m, :m] = cos_i
    einv_ref[0, :m, m:] = -sin_i
    einv_ref[0, m:, :m] = sin_i
    einv_ref[0, m:, m:] = cos_i


def _dft_tables(n):
    m = n // FFT_RADIX
    cb, sb, ca, sa = pl.pallas_call(
        functools.partial(_dft_seed_kernel, n=n),
        out_shape=[jax.ShapeDtypeStruct((m, m), F32)] * 2
        + [jax.ShapeDtypeStruct((FFT_RADIX, m), F32)] * 2,
        compiler_params=pltpu.CompilerParams(vmem_limit_bytes=VMEM_LIMIT),
        name="dft_seed",
    )()
    full = pl.BlockSpec((m, m), lambda i: (0, 0))
    rowspec = pl.BlockSpec((1, 1, m), lambda i: (i, 0, 0))
    colspec = pl.BlockSpec((1, m, 1), lambda i: (i, 0, 0))
    tab = pl.BlockSpec((1, 2 * m, 2 * m), lambda i: (i, 0, 0))
    return pl.pallas_call(
        _dft_table_kernel,
        grid=(FFT_RADIX,),
        in_specs=[full, full, rowspec, rowspec, colspec, colspec],
        out_specs=[tab, tab],
        out_shape=[jax.ShapeDtypeStruct((FFT_RADIX, 2 * m, 2 * m), BF16)] * 2,
        compiler_params=_params("parallel"),
        name="dft_tables",
    )(cb, sb, ca.reshape(FFT_RADIX, 1, m), sa.reshape(FFT_RADIX, 1, m),
      ca.reshape(FFT_RADIX, m, 1), sa.reshape(FFT_RADIX, m, 1))


def _filter_spectrum_kernel(efwd_ref, hf_ref, hb_ref, kr_ref, ki_ref, *, n):
    m = kr_ref.shape[1]
    xf = _dot(efwd_ref[0], hf_ref[0, 0])
    xb = _dot(efwd_ref[0], hb_ref[0, 0])
    kr_ref[0] = (xf[:m] + xb[:m]) * F32(1.0 / n)
    ki_ref[0] = (xf[m:] - xb[m:]) * F32(1.0 / n)


def _filter_spectrum(efwd, sah):
    radix, m2, _ = efwd.shape
    n = radix * m2 // 2
    orders = sah.shape[3] // (2 * H_WIDTH)
    slab = lambda off: pl.BlockSpec((1, 1, m2, H_WIDTH), lambda k1, o: (0, k1, 0, off + o))
    kspec = pl.BlockSpec((1, m2 // 2, H_WIDTH), lambda k1, o: (k1, 0, o))
    return pl.pallas_call(
        functools.partial(_filter_spectrum_kernel, n=n),
        grid=(radix, orders),
        in_specs=[pl.BlockSpec((1, m2, m2), lambda k1, o: (k1, 0, 0)), slab(0), slab(orders)],
        out_specs=[kspec, kspec],
        out_shape=[jax.ShapeDtypeStruct((radix, m2 // 2, orders * H_WIDTH), F32)] * 2,
        compiler_params=_params("parallel", "arbitrary"),
        name="filter_spectrum",
    )(efwd, sah, sah)


def _spectral_kernel(efwd_ref, einv_ref, sa_ref, kr_ref, ki_ref, sb_ref):
    slabs, m = kr_ref.shape[0], kr_ref.shape[1]
    width = sa_ref.shape[3]
    cw = min(512, width)
    for s in range(slabs):
        for j in range(width // cw):
            cols = slice(j * cw, (j + 1) * cw)
            x = _dot(efwd_ref[s], sa_ref[0, s, :, cols])
            xr, xi = x[:m], x[m:]
            kr, ki = kr_ref[s, :, cols], ki_ref[s, :, cols]
            y = jnp.concatenate([xr * kr - xi * ki, xr * ki + xi * kr], axis=0).astype(BF16)
            sb_ref[0, s, :, cols] = _dot(einv_ref[s], y).astype(BF16)


def _spectral(efwd, einv, sa, kr, ki, order):
    p, radix, m2, ncols = sa.shape
    ns = SPECTRAL_SLABS
    slab = pl.BlockSpec((1, ns, m2, ncols), lambda k1, pi: (pi, k1, 0, 0))
    tab = pl.BlockSpec((ns, m2, m2), lambda k1, pi: (k1, 0, 0))
    kspec = pl.BlockSpec((ns, m2 // 2, ncols), lambda k1, pi: (k1, 0, order))
    return pl.pallas_call(
        _spectral_kernel,
        grid=(radix // ns, p),
        in_specs=[tab, tab, slab, kspec, kspec],
        out_specs=slab,
        out_shape=jax.ShapeDtypeStruct(sa.shape, BF16),
        compiler_params=_params("parallel", "arbitrary"),
        name="hyena_spectral",
    )(efwd, einv, sa, kr, ki)


def _output_kernel(yf_ref, yb_ref, z_ref, hg_ref, gm_ref, gh_ref, yh_ref, x_ref, gate_ref, mnw_ref,
                   fnw_ref, wm_ref, wh_ref, wo_ref, o_ref):
    f32 = lambda ref: ref[0].astype(F32)
    g = (f32(yf_ref) + f32(yb_ref)) * _silu(f32(z_ref))
    gw = D_MODEL // M_GROUPS
    parts = []
    for i in range(M_GROUPS):
        gi = g[:, i * gw:(i + 1) * gw]
        parts.append(gi * lax.rsqrt(jnp.mean(gi * gi, axis=-1, keepdims=True) + NORM_EPS))
    ym = (jnp.concatenate(parts, axis=1) * mnw_ref[...]).astype(BF16)
    yh = (f32(yh_ref) * _silu(f32(hg_ref))).astype(BF16)
    merged = (jax.nn.sigmoid(f32(gm_ref)) * _dot(ym, wm_ref[...])
              + jax.nn.sigmoid(f32(gh_ref)) * _dot(yh, wh_ref[...]))
    out = _dot(merged.astype(BF16), wo_ref[...])
    xn = x_ref[0] + gate_ref[0] * out
    ms = jnp.mean(xn * xn, axis=-1, keepdims=True)
    o_ref[0] = xn * lax.rsqrt(ms + NORM_EPS) * fnw_ref[...]


def _mixer_output(y_f, y_b, cols, yh, x, gate, m_norm_w, final_norm_w, wm, wh, wo):
    b, l, d = x.shape
    tm = min(512, l)
    tok = lambda blk: pl.BlockSpec((1, tm, d), lambda bi, i: (bi, i, blk))
    wspec = pl.BlockSpec((d, d), lambda bi, i: (0, 0))
    rowspec = pl.BlockSpec((1, d), lambda bi, i: (0, 0))
    return pl.pallas_call(
        _output_kernel,
        grid=(b, l // tm),
        in_specs=[tok(0), tok(0),
                  tok(COL_Z), tok(COL_HGATE), tok(COL_GM), tok(COL_GH), tok(0), tok(0),
                  pl.BlockSpec((1, 1, d), lambda bi, i: (bi, 0, 0)),
                  rowspec, rowspec, wspec, wspec, wspec],
        out_specs=tok(0),
        out_shape=jax.ShapeDtypeStruct((b, l, d), F32),
        compiler_params=_params("parallel", "parallel"),
        name="mixer_output",
    )(y_f, y_b, cols, cols, cols, cols, yh, x, gate, m_norm_w.reshape(1, d),
      final_norm_w.reshape(1, d), wm, wh, wo)


def _layer(x, ctx, c, c_ctx, ada_w, ada_b, norm_w, w_in, m_conv_w, m_conv_b, m_dt_bias, m_a_log,
           m_d, m_norm_w, h_conv_w, h_conv_b, h_w1, h_b1, h_w2, h_b2, h_w3, h_b3, h_freq,
           h_w_out, h_bias, w_branch_m, w_branch_h, w_out, final_norm_w):
    b, l, d = x.shape
    lc = ctx.shape[1]
    ndt = 2 * M_HEADS

    pad_rows = -(b + 1) % SUBLANES
    c_rows = jnp.concatenate([c, c_ctx[None], jnp.zeros((pad_rows, d), F32)], axis=0)
    mod = _adaln(c_rows, ada_w, ada_b)
    shift, scale, gate = (mod[:b, i * d:(i + 1) * d].reshape(b, 1, d) for i in range(3))
    shift_c = jnp.broadcast_to(mod[b, :d], (b, 1, d))
    scale_c = jnp.broadcast_to(mod[b, d:2 * d], (b, 1, d))

    o_z, o_xbc, o_dt = 0, d, d + M_XBC
    o_hg = o_dt + ndt
    o_hp = o_hg + H_WIDTH
    o_gm = o_hp + 3 * H_WIDTH
    w_xbc = w_in[:, o_xbc:o_dt].astype(BF16)
    w_conv = jnp.concatenate([w_xbc, w_in[:, o_hp:o_gm].astype(BF16)], axis=1)
    w_plain = jnp.concatenate([w_in[:, o_z:o_xbc], w_in[:, o_hg:o_hp], w_in[:, o_gm:]],
                              axis=1).astype(BF16)
    w_dt_pad = jnp.zeros((d, DT_PAD), F32).at[:, :ndt].set(w_in[:, o_dt:o_hg])
    nw = norm_w.reshape(1, d)
    conv_w = jnp.concatenate([m_conv_w, h_conv_w], axis=1)
    conv_b = jnp.concatenate([m_conv_b, h_conv_b]).reshape(1, -1)
    n_silu = M_XBC // CONV_TN

    bias_r = jnp.zeros((1, DT_PAD), F32).at[0, :ndt].set(m_dt_bias.reshape(ndt))
    cols, cols_p, dt_l = _inproj(x, shift, scale, nw, w_conv, conv_w, conv_b, n_silu, w_plain,
                                 w_dt_pad, bias_r, min(1024, l))
    cols_c, _, dt_c = _inproj(ctx, shift_c, scale_c, nw, w_xbc, conv_w[:, :M_XBC],
                              conv_b[:, :M_XBC], n_silu, None, w_dt_pad, bias_r, min(1024, lc))

    a_r = jnp.zeros((1, DT_PAD), F32).at[0, :ndt].set(m_a_log.reshape(ndt))
    dskip_row = jnp.repeat(m_d, M_HEADDIM).reshape(1, d)
    y_f, y_b = _ssd(cols, cols_c, dt_l, dt_c, a_r, dskip_row)

    assert b % 2 == 0, "batch rows are packed in pairs as complex signals"
    half = FFT_RADIX // 2
    m = l // half
    pairs = b // 2
    u3 = cols.reshape(pairs, 2, half, m, cols.shape[2])
    filt = _hyena_filters(l, h_w1, h_b1, h_w2, h_b2, h_w3, h_b3, h_freq, h_w_out)
    efwd, einv = _dft_tables(2 * l)
    sah = _radix_fwd(filt.reshape(1, 1, half, m, filt.shape[1]), 0, filt.shape[1], True,
                     "filter_radix")
    kr, ki = _filter_spectrum(efwd, sah.reshape(1, FFT_RADIX, 2 * m, filt.shape[1]))
    z, z_block = u3, COL_HPROJ
    for order in range(2):
        sa = _radix_fwd(z, z_block, H_WIDTH, False, "hyena_radix_fwd")
        sb = _spectral(efwd, einv, sa.reshape(pairs, FFT_RADIX, 2 * m, H_WIDTH), kr, ki, order)
        z = _radix_inv(sb.reshape(pairs, FFT_RADIX, 2, m, H_WIDTH), u3, COL_HPROJ + 1 + order,
                       z, z_block, h_bias[order].reshape(1, H_WIDTH), "hyena_radix_inv")
        z_block = 0
    y_h = z.reshape(b, l, H_WIDTH)

    return _mixer_output(y_f, y_b, cols_p, y_h, x, gate, m_norm_w, final_norm_w,
                         w_branch_m.astype(BF16), w_branch_h.astype(BF16), w_out.astype(BF16))


def kernel(x, c, ctx, c_ctx, ada_w, ada_b, norm_w, w_in, m_conv_w, m_conv_b, m_dt_bias, m_a_log,
           m_d, m_norm_w, h_conv_w, h_conv_b, h_w1, h_b1, h_w2, h_b2, h_w3, h_b3, h_freq,
           h_w_out, h_bias, w_branch_m, w_branch_h, w_out, final_norm_w):
    assert ada_w.shape[0] == 1, "one trunk layer"
    return _layer(x, ctx, c, c_ctx, ada_w[0], ada_b[0], norm_w[0], w_in[0], m_conv_w[0],
                  m_conv_b[0], m_dt_bias[0], m_a_log[0], m_d[0], m_norm_w[0], h_conv_w[0],
                  h_conv_b[0], h_w1[0], h_b1[0], h_w2[0], h_b2[0], h_w3[0], h_b3[0], h_freq[0],
                  h_w_out[0], h_bias[0], w_branch_m[0], w_branch_h[0], w_out[0], final_norm_w)
```

```python
import functools
import math

import jax
import jax.numpy as jnp
from jax import lax
from jax.experimental import pallas as pl
from jax.experimental.pallas import tpu as pltpu

F32 = jnp.float32
BF16 = jnp.bfloat16
HIGHEST = lax.Precision.HIGHEST

LANES = 128
SUBLANES = 8
VMEM_LIMIT = 56 * 1024 * 1024

NORM_EPS = 1e-6
D_MODEL = 1024
M_HEADDIM = 64
M_HEADS = 16
M_GROUPS = 4
M_HPG = M_HEADS // M_GROUPS
M_STATE = 128
M_XBC = D_MODEL + 2 * M_GROUPS * M_STATE
M_CHUNK = 256
HEADDIM_SHIFT = M_HEADDIM.bit_length() - 1
CHUNK_SHIFT = M_CHUNK.bit_length() - 1
H_WIDTH = 1024
H_EMB = 33
H_HID = 64
H_DECAY_TARGET = 1e-2
H_FAST_DECAY_PCT = 0.3
H_SLOW_DECAY_PCT = 1.5
DT_PAD = LANES
FFT_RADIX = 32
ROW_CHUNK = 16
SPECTRAL_SLABS = 4
PROJ_SLAB = 256

COL_HPROJ = 2
COL_Z, COL_HGATE, COL_GM, COL_GH = 0, 1, 2, 3
CONV_TN = 1024
PLAIN_TN = 2048


def _dot(a, b, precision=None):
    return jnp.dot(a, b, preferred_element_type=F32, precision=precision)


def _dot_nt(a, b, precision=None):
    return lax.dot_general(a, b, (((1,), (1,)), ((), ())), preferred_element_type=F32,
                           precision=precision)


def _dot_tn(a, b, precision=None):
    return lax.dot_general(a, b, (((0,), (0,)), ((), ())), preferred_element_type=F32,
                           precision=precision)


def _params(*sem):
    return pltpu.CompilerParams(dimension_semantics=sem, vmem_limit_bytes=VMEM_LIMIT)


def _silu(v):
    return v * jax.nn.sigmoid(v)


def _adaln_kernel(c_ref, w_ref, b_ref, o_ref):
    o_ref[...] = _dot(_silu(c_ref[...]), w_ref[...], HIGHEST) + b_ref[...]


def _adaln(c_rows, ada_w, ada_b):
    rows, d = c_rows.shape
    n = ada_w.shape[1]
    tn = 1024
    return pl.pallas_call(
        _adaln_kernel,
        grid=(n // tn,),
        in_specs=[pl.BlockSpec((rows, d), lambda j: (0, 0)),
                  pl.BlockSpec((d, tn), lambda j: (0, j)),
                  pl.BlockSpec((1, tn), lambda j: (0, j))],
        out_specs=pl.BlockSpec((rows, tn), lambda j: (0, j)),
        out_shape=jax.ShapeDtypeStruct((rows, n), F32),
        compiler_params=_params("parallel"),
        name="adaln",
    )(c_rows, ada_w, ada_b.reshape(1, n))


def _split_bf16(v):
    hi = v.astype(BF16)
    return hi, (v - hi.astype(F32)).astype(BF16)


def _softplus(v):
    return jnp.maximum(v, 0.0) + jnp.log1p(jnp.exp(-jnp.abs(v)))


def _conv_rows(p_ext, tm, w, bias, has_prev, has_next, apply_silu, rows_ref):
    u = p_ext[0:tm, :]
    lo = SUBLANES
    rows_ref[lo:lo + tm, :] = u
    rows_ref[lo - 1:lo, :] = jnp.where(has_prev, p_ext[tm + SUBLANES - 1:tm + SUBLANES, :], 0.0)
    rows_ref[lo + tm:lo + tm + 1, :] = jnp.where(has_next,
                                                 p_ext[tm + SUBLANES:tm + SUBLANES + 1, :], 0.0)
    up = rows_ref[lo - 1:lo - 1 + tm, :]
    un = rows_ref[lo + 1:lo + 1 + tm, :]
    y = w[0:1, :] * up + w[1:2, :] * u + w[2:3, :] * un + bias
    return _silu(y) if apply_silu else y


def _inproj_kernel(*refs, n_silu, n_conv, has_plain):
    if has_plain:
        (x_ref, xprev_ref, xnext_ref, shift_ref, scale_ref, nw_ref, wc_ref, cw_ref, cb_ref, wp_ref,
         wdt_hi_ref, wdt_lo_ref, dtb_ref, oc_ref, op_ref, dt_ref, h_scr, rows_scr) = refs
    else:
        (x_ref, xprev_ref, xnext_ref, shift_ref, scale_ref, nw_ref, wc_ref, cw_ref, cb_ref,
         wdt_hi_ref, wdt_lo_ref, dtb_ref, oc_ref, dt_ref, h_scr, rows_scr) = refs
    i = pl.program_id(1)
    j = pl.program_id(2)
    tm = x_ref.shape[1]

    def modulated(xt):
        ms = jnp.mean(xt * xt, axis=-1, keepdims=True)
        hn = xt * lax.rsqrt(ms + NORM_EPS) * nw_ref[...]
        return hn * (1.0 + scale_ref[0]) + shift_ref[0]

    @pl.when(j == 0)
    def _():
        hi, lo = _split_bf16(modulated(x_ref[0]))
        h_scr[0:tm, :] = hi
        halo = jnp.concatenate([xprev_ref[0], xnext_ref[0]], axis=0)
        h_scr[tm:, :] = modulated(halo).astype(BF16)
        dt_raw = (_dot(hi, wdt_hi_ref[...]) + _dot(lo, wdt_hi_ref[...])
                  + _dot(hi, wdt_lo_ref[...]))
        dt_ref[0] = _softplus(dt_raw + dtb_ref[...])

    has_prev, has_next = i > 0, i < pl.num_programs(1) - 1

    def project(mode, w_ref, o_ref):
        for c in range(o_ref.shape[2] // PROJ_SLAB):
            cols = slice(c * PROJ_SLAB, (c + 1) * PROJ_SLAB)
            p_ext = _dot(h_scr[...], w_ref[:, cols])
            if mode == "plain":
                y = p_ext[0:tm, :]
            else:
                y = _conv_rows(p_ext, tm, cw_ref[:, cols], cb_ref[:, cols], has_prev, has_next,
                               mode == "silu", rows_scr.at[c % 2])
            o_ref[0, :, cols] = y.astype(o_ref.dtype)

    pl.when(j < n_silu)(lambda: project("silu", wc_ref, oc_ref))
    if n_conv > n_silu:
        pl.when(jnp.logical_and(j >= n_silu, j < n_conv))(lambda: project("conv", wc_ref, oc_ref))
    if has_plain:
        pl.when(j >= n_conv)(lambda: project("plain", wp_ref, op_ref))


def _inproj(x, shift, scale, norm_w, w_conv, conv_w, conv_b, n_silu, w_plain, w_dt, dt_bias, tm):
    b, l, d = x.shape
    nct = w_conv.shape[1] // CONV_TN
    npt = 0 if w_plain is None else w_plain.shape[1] // PLAIN_TN
    nsub = tm // SUBLANES
    last = l // SUBLANES - 1
    w_dt_hi, w_dt_lo = _split_bf16(w_dt)
    conv_j = lambda j: jnp.minimum(j, nct - 1)
    plain_j = lambda j: jnp.maximum(j - nct, 0)
    const = lambda shape: pl.BlockSpec(shape, lambda bi, i, j: tuple(0 for _ in shape))
    in_specs = [pl.BlockSpec((1, tm, d), lambda bi, i, j: (bi, i, 0)),
                pl.BlockSpec((1, SUBLANES, d),
                             lambda bi, i, j: (bi, jnp.maximum(i * nsub - 1, 0), 0)),
                pl.BlockSpec((1, SUBLANES, d),
                             lambda bi, i, j: (bi, jnp.minimum((i + 1) * nsub, last), 0)),
                pl.BlockSpec((1, 1, d), lambda bi, i, j: (bi, 0, 0)),
                pl.BlockSpec((1, 1, d), lambda bi, i, j: (bi, 0, 0)),
                const((1, d)),
                pl.BlockSpec((d, CONV_TN), lambda bi, i, j: (0, conv_j(j))),
                pl.BlockSpec((3, CONV_TN), lambda bi, i, j: (0, conv_j(j))),
                pl.BlockSpec((1, CONV_TN), lambda bi, i, j: (0, conv_j(j)))]
    args = [x, x, x, shift, scale, norm_w, w_conv, conv_w, conv_b]
    out_specs = [pl.BlockSpec((1, tm, CONV_TN), lambda bi, i, j: (bi, i, conv_j(j)))]
    out_shape = [jax.ShapeDtypeStruct((b, l, w_conv.shape[1]), BF16)]
    if npt:
        in_specs.append(pl.BlockSpec((d, PLAIN_TN), lambda bi, i, j: (0, plain_j(j))))
        args.append(w_plain)
        out_specs.append(pl.BlockSpec((1, tm, PLAIN_TN), lambda bi, i, j: (bi, i, plain_j(j))))
        out_shape.append(jax.ShapeDtypeStruct((b, l, w_plain.shape[1]), BF16))
    in_specs += [const((d, DT_PAD)), const((d, DT_PAD)), const((1, DT_PAD))]
    args += [w_dt_hi, w_dt_lo, dt_bias]
    out_specs.append(pl.BlockSpec((1, tm, DT_PAD), lambda bi, i, j: (bi, i, 0)))
    out_shape.append(jax.ShapeDtypeStruct((b, l, DT_PAD), F32))
    outs = pl.pallas_call(
        functools.partial(_inproj_kernel, n_silu=n_silu, n_conv=nct, has_plain=npt > 0),
        grid=(b, l // tm, nct + npt),
        in_specs=in_specs,
        out_specs=out_specs,
        out_shape=out_shape,
        scratch_shapes=[pltpu.VMEM((tm + 2 * SUBLANES, d), BF16),
                        pltpu.VMEM((2, tm + 2 * SUBLANES, PROJ_SLAB), F32)],
        compiler_params=_params("parallel", "parallel", "arbitrary"),
        name="inproj",
    )(*args)
    return (outs[0], outs[1], outs[2]) if npt else (outs[0], None, outs[1])


def _cumsum_dot(tri_bf16, v):
    hi = v.astype(BF16)
    rest = v - hi.astype(F32)
    mid = rest.astype(BF16)
    lo = (rest - mid.astype(F32)).astype(BF16)
    return _dot(tri_bf16, hi) + _dot(tri_bf16, mid) + _dot(tri_bf16, lo)


def _expand_heads(v, e_bf16):
    hi = v.astype(BF16)
    lo = (v - hi.astype(F32)).astype(BF16)
    return _dot(hi, e_bf16) + _dot(lo, e_bf16)


def _ssd_chunk(fwd, xbc, dt_all, a_row, dskip_row, y_ref, state):
    q = M_CHUNK
    off = 0 if fwd else M_HEADS
    last = q - 1 if fwd else 0
    x = xbc[:, :D_MODEL].astype(F32)
    bmat = xbc[:, D_MODEL:D_MODEL + M_GROUPS * M_STATE]
    cmat = xbc[:, D_MODEL + M_GROUPS * M_STATE:]

    a_all = dt_all * a_row
    ri = lax.broadcasted_iota(jnp.int32, (q, q), 0)
    ci = lax.broadcasted_iota(jnp.int32, (q, q), 1)
    mask = (ri >= ci) if fwd else (ri <= ci)
    acs_all = _cumsum_dot(jnp.where(mask, 1.0, 0.0).astype(BF16), a_all)
    acs_t = acs_all.T[off:off + M_HEADS, :]
    acs = acs_all[:, off:off + M_HEADS]
    e_atot_c = jnp.exp(acs_t[:, last:last + 1])

    hid = lax.broadcasted_iota(jnp.int32, (M_HEADS, D_MODEL), 0)
    lid = lax.broadcasted_iota(jnp.int32, (M_HEADS, D_MODEL), 1)
    expand = jnp.where((lid >> HEADDIM_SHIFT) == hid, 1.0, 0.0).astype(BF16)
    per_head = jnp.concatenate([dt_all[:, off:off + M_HEADS],
                                jnp.exp(acs[last:last + 1, :] - acs),
                                jnp.exp(acs)], axis=0)
    wide = _dot(per_head.astype(BF16), expand)
    xdt = x * wide[:q]
    xw = (xdt * wide[q:2 * q]).astype(BF16)
    e_acs = wide[2 * q:]

    gw = M_HPG * M_HEADDIM
    hb = q // 2
    halves = (slice(0, hb), slice(hb, q))
    far_rows, far_cols = (halves[1], halves[0]) if fwd else (halves[0], halves[1])
    tri = mask[:hb, :hb]
    rb = lax.broadcasted_iota(jnp.int32, (M_HPG * hb, gw), 0) >> (CHUNK_SHIFT - 1)
    cb = lax.broadcasted_iota(jnp.int32, (M_HPG * hb, gw), 1) >> HEADDIM_SHIFT
    blockdiag = rb == cb
    for g in range(M_GROUPS):
        bg = bmat[:, g * M_STATE:(g + 1) * M_STATE]
        cg = cmat[:, g * M_STATE:(g + 1) * M_STATE]
        scores = _dot_nt(cg, bg)
        near, far = ([], []), []
        for r in range(M_HPG):
            h = g * M_HPG + r
            col, row = acs[:, h:h + 1], acs_t[h:h + 1, :]
            for k, sl in enumerate(halves):
                decay = jnp.exp(jnp.where(tri, col[sl] - row[:, sl], -jnp.inf))
                near[k].append((scores[sl, sl] * decay).astype(BF16))
            decay = jnp.exp(col[far_rows] - row[:, far_cols])
            far.append((scores[far_rows, far_cols] * decay).astype(BF16))
        xg = xdt[:, g * gw:(g + 1) * gw]
        x_bd = [jnp.where(blockdiag, jnp.concatenate([xg[sl]] * M_HPG, axis=0), 0.0).astype(BF16)
                for sl in halves]
        x_all = jnp.concatenate(x_bd, axis=0)
        if fwd:
            y_top = _dot(jnp.concatenate(near[0], axis=1), x_bd[0])
            y_bot = _dot(jnp.concatenate(far + near[1], axis=1), x_all)
        else:
            y_top = _dot(jnp.concatenate(near[0] + far, axis=1), x_all)
            y_bot = _dot(jnp.concatenate(near[1], axis=1), x_bd[1])
        y_diag = jnp.concatenate([y_top, y_bot], axis=0)
        s_g = state[g * M_HPG:(g + 1) * M_HPG].reshape(gw, M_STATE).astype(BF16)
        y_g = y_diag + e_acs[:, g * gw:(g + 1) * gw] * _dot_nt(cg, s_g)
        if fwd:
            y_g = y_g + dskip_row[:, g * gw:(g + 1) * gw] * x[:, g * gw:(g + 1) * gw]
        y_ref[0, :, g * gw:(g + 1) * gw] = y_g.astype(y_ref.dtype)
        upd = _dot_tn(xw[:, g * gw:(g + 1) * gw], bg)
        for r in range(M_HPG):
            h = g * M_HPG + r
            state[h] = (state[h] * e_atot_c[h:h + 1, 0:1]
                        + upd[r * M_HEADDIM:(r + 1) * M_HEADDIM, :])


def _ssd_kernel(xlf_ref, xcf_ref, xlb_ref, xcb_ref, dtlf_ref, dtcf_ref, dtlb_ref, dtcb_ref, a_ref,
                dskip_ref, yf_ref, yb_ref, state_f, state_b, *, n_ctx_chunks):
    s = pl.program_id(1)
    is_ctx = s < n_ctx_chunks

    @pl.when(s == 0)
    def _():
        state_f[...] = jnp.zeros_like(state_f)
        state_b[...] = jnp.zeros_like(state_b)

    dt_f = jnp.where(is_ctx, dtcf_ref[0], dtlf_ref[0])
    dt_b = jnp.where(is_ctx, dtcb_ref[0], dtlb_ref[0])
    a_row = -jnp.exp(a_ref[...])
    x_f = jnp.where(is_ctx, xcf_ref[0], xlf_ref[0])
    x_b = jnp.where(is_ctx, xcb_ref[0], xlb_ref[0])
    _ssd_chunk(True, x_f, dt_f, a_row, dskip_ref[...], yf_ref, state_f)
    _ssd_chunk(False, x_b, dt_b, a_row, dskip_ref[...], yb_ref, state_b)


def _ssd(cols, cols_c, dt_l, dt_c, a_r, dskip_row):
    b, l, _ = cols.shape
    lc = cols_c.shape[1]
    q = M_CHUNK
    nl, nc = l // q, lc // q

    lat_f = lambda si: jnp.maximum(si - nc, 0)
    lat_b = lambda si: nl - 1 - lat_f(si)
    ctx_f = lambda si: jnp.minimum(si, nc - 1)
    ctx_b = lambda si: nc - 1 - ctx_f(si)

    small = lambda shape: pl.BlockSpec(shape, lambda bi, si: (0, 0))
    x_spec = lambda fn: pl.BlockSpec((1, q, M_XBC), lambda bi, si: (bi, fn(si), 0))
    dt_spec = lambda fn: pl.BlockSpec((1, q, DT_PAD), lambda bi, si: (bi, fn(si), 0))
    state = pltpu.VMEM((M_HEADS, M_HEADDIM, M_STATE), F32)
    return pl.pallas_call(
        functools.partial(_ssd_kernel, n_ctx_chunks=nc),
        grid=(b, nl + nc),
        in_specs=[x_spec(lat_f), x_spec(ctx_f), x_spec(lat_b), x_spec(ctx_b),
                  dt_spec(lat_f), dt_spec(ctx_f), dt_spec(lat_b), dt_spec(ctx_b),
                  small((1, DT_PAD)), small((1, D_MODEL))],
        out_specs=[pl.BlockSpec((1, q, D_MODEL), lambda bi, si: (bi, lat_f(si), 0)),
                   pl.BlockSpec((1, q, D_MODEL), lambda bi, si: (bi, lat_b(si), 0))],
        out_shape=[jax.ShapeDtypeStruct((b, l, D_MODEL), BF16)] * 2,
        scratch_shapes=[state, state],
        compiler_params=_params("parallel", "arbitrary"),
        name="ssd",
    )(cols, cols_c, cols, cols_c, dt_l, dt_c, dt_l, dt_c, a_r, dskip_row)


def _filter_kernel(fl_ref, w1_ref, b1_ref, w2_ref, b2_ref, w3_ref, b3_ref, freq_ref, wo_hi_ref,
                   wo_lo_ref, deltas_ref, h_ref, *, seqlen):
    i = pl.program_id(0)
    tr = h_ref.shape[0]
    pos = (i * tr + lax.broadcasted_iota(jnp.int32, (tr, 1), 0))
    posf = pos.astype(F32)
    t = posf / F32(seqlen - 1)
    w = F32(2.0 * math.pi / seqlen) * posf
    ang = fl_ref[...] * w
    lane = lax.broadcasted_iota(jnp.int32, (tr, LANES), 1)
    bands = (H_EMB - 1) // 2
    feats = jnp.where(lane == 0, t,
                      jnp.where(lane <= bands, jnp.cos(ang),
                                jnp.where(lane < H_EMB, -jnp.sin(ang), 0.0)))
    freq = freq_ref[...]
    hid = jnp.sin(freq * (_dot(feats, w1_ref[...], HIGHEST) + b1_ref[...]))
    hid = jnp.sin(freq * (_dot(hid, w2_ref[...], HIGHEST) + b2_ref[...]))
    hid = jnp.sin(freq * (_dot(hid, w3_ref[...], HIGHEST) + b3_ref[...]))
    hid_hi, hid_lo = _split_bf16(hid)
    filt = (_dot(hid_hi, wo_hi_ref[...]) + _dot(hid_lo, wo_hi_ref[...])
            + _dot(hid_hi, wo_lo_ref[...]))
    window = jnp.exp(-t * deltas_ref[...])
    hw = H_WIDTH
    orders = filt.shape[1] // (2 * hw)
    for o in range(orders):
        h_fwd = filt[:, o * hw:(o + 1) * hw] * window
        h_bwd = filt[:, (orders + o) * hw:(orders + o + 1) * hw] * window
        h_ref[:, o * hw:(o + 1) * hw] = h_fwd + h_bwd
        h_ref[:, (orders + o) * hw:(orders + o + 1) * hw] = h_fwd - h_bwd


def _hyena_filters(seqlen, w1, b1, w2, b2, w3, b3, freq, w_out):
    tr = min(512, seqlen)
    bands = (H_EMB - 1) // 2
    f = jnp.linspace(1e-4, bands - 1, bands, dtype=F32)
    fl = jnp.zeros((1, LANES), F32).at[0, 1:1 + bands].set(f).at[0, 1 + bands:H_EMB].set(f)
    w1p = jnp.zeros((LANES, H_HID), F32).at[:H_EMB].set(w1)
    min_decay = math.log(H_DECAY_TARGET) / H_SLOW_DECAY_PCT
    max_decay = math.log(H_DECAY_TARGET) / H_FAST_DECAY_PCT
    deltas = jnp.abs(jnp.linspace(min_decay, max_decay, H_WIDTH, dtype=F32)).reshape(1, H_WIDTH)
    row = lambda v: v.reshape(1, -1)
    full = lambda a: pl.BlockSpec(a.shape, lambda i: (0, 0))
    args = (fl, w1p, row(b1), w2, row(b2), w3, row(b3), row(freq), *_split_bf16(w_out), deltas)
    nout = w_out.shape[1]
    return pl.pallas_call(
        functools.partial(_filter_kernel, seqlen=seqlen),
        grid=(seqlen // tr,),
        in_specs=[full(a) for a in args],
        out_specs=pl.BlockSpec((tr, nout), lambda i: (i, 0)),
        out_shape=jax.ShapeDtypeStruct((seqlen, nout), F32),
        compiler_params=_params("parallel"),
        name="hyena_filters",
    )(*args)


class _Neg:
    def __init__(self, arr):
        self.arr = arr


def _r_neg(a):
    if a is None:
        return None
    return a.arr if isinstance(a, _Neg) else _Neg(a)


def _r_add(a, b):
    if a is None:
        return b
    if b is None:
        return a
    if isinstance(a, _Neg) and isinstance(b, _Neg):
        return _Neg(a.arr + b.arr)
    if isinstance(b, _Neg):
        return a - b.arr
    if isinstance(a, _Neg):
        return b - a.arr
    return a + b


def _r_sub(a, b):
    return _r_add(a, _r_neg(b))


def _r_scale(a, s):
    if a is None or s == 0.0:
        return None
    if isinstance(a, _Neg):
        a, s = a.arr, -s
    if s == 1.0:
        return a
    if s == -1.0:
        return _Neg(a)
    return a * s


def _r_value(a):
    return -a.arr if isinstance(a, _Neg) else a


def _c_mul_const(z, w):
    snap = lambda v: float(round(v)) if abs(v - round(v)) < 1e-12 else float(v)
    wr, wi = snap(w.real), snap(w.imag)
    zr, zi = z
    if wr != 0.0 and abs(abs(wr) - abs(wi)) < 1e-12:
        sr, si, c = math.copysign(1.0, wr), math.copysign(1.0, wi), abs(wr)
        return (_r_scale(_r_sub(_r_scale(zr, sr), _r_scale(zi, si)), c),
                _r_scale(_r_add(_r_scale(zr, si), _r_scale(zi, sr)), c))
    return (_r_sub(_r_scale(zr, wr), _r_scale(zi, wi)),
            _r_add(_r_scale(zr, wi), _r_scale(zi, wr)))


def _fft(xs, sign, first_half_only=False):
    n = len(xs)
    if n == 1:
        return list(xs)
    even, odd = _fft(xs[0::2], sign), _fft(xs[1::2], sign)
    out = [None] * n
    for k in range(n // 2):
        tr, ti = _c_mul_const(odd[k], complex(math.cos(2 * math.pi * k / n),
                                              sign * math.sin(2 * math.pi * k / n)))
        out[k] = (_r_add(even[k][0], tr), _r_add(even[k][1], ti))
        if not first_half_only:
            out[k + n // 2] = (_r_sub(even[k][0], tr), _r_sub(even[k][1], ti))
    return out[:n // 2] if first_half_only else out


def _tile_or_zero(v):
    return jnp.zeros((ROW_CHUNK, LANES), F32) if v is None else _r_value(v)


def _radix_fwd_kernel(z_ref, o_ref, *, real_input):
    _, _, half, rows_total, width = z_ref.shape

    def body(i, carry):
        rows = pl.ds(pl.multiple_of(i * ROW_CHUNK, ROW_CHUNK), ROW_CHUNK)
        for j in range(width // LANES):
            lanes = slice(j * LANES, (j + 1) * LANES)
            xs = []
            for t1 in range(half):
                re = z_ref[0, 0, t1, rows, lanes].astype(F32)
                im = None if real_input else z_ref[0, 1, t1, rows, lanes].astype(F32)
                xs.append((re, im))
            xs += [(None, None)] * half
            for k1, (re, im) in enumerate(_fft(xs, -1.0)):
                o_ref[0, k1, 0, rows, lanes] = _tile_or_zero(re).astype(BF16)
                o_ref[0, k1, 1, rows, lanes] = _tile_or_zero(im).astype(BF16)
        return carry

    lax.fori_loop(0, rows_total // ROW_CHUNK, body, 0)


def _radix_fwd(z, col_block, ncols, real_input, name):
    p, parts, half, m, _ = z.shape
    tt, cb = min(256, m), 256
    off = col_block * (H_WIDTH // cb)
    return pl.pallas_call(
        functools.partial(_radix_fwd_kernel, real_input=real_input),
        grid=(p, m // tt, ncols // cb),
        in_specs=[pl.BlockSpec((1, parts, half, tt, cb), lambda pi, ti, ci: (pi, 0, 0, ti, off + ci))],
        out_specs=pl.BlockSpec((1, 2 * half, 2, tt, cb), lambda pi, ti, ci: (pi, 0, 0, ti, ci)),
        out_shape=jax.ShapeDtypeStruct((p, 2 * half, 2, m, ncols), BF16),
        compiler_params=_params("parallel", "parallel", "parallel"),
        name=name,
    )(z)


def _radix_inv_kernel(sb_ref, gate_ref, zin_ref, bias_ref, o_ref):
    _, radix, _, rows_total, width = sb_ref.shape

    def body(i, carry):
        rows = pl.ds(pl.multiple_of(i * ROW_CHUNK, ROW_CHUNK), ROW_CHUNK)
        for j in range(width // LANES):
            lanes = slice(j * LANES, (j + 1) * LANES)
            ys = [(sb_ref[0, k1, 0, rows, lanes].astype(F32),
                   sb_ref[0, k1, 1, rows, lanes].astype(F32)) for k1 in range(radix)]
            bias = bias_ref[:, lanes]
            for t1, parts in enumerate(_fft(ys, 1.0, first_half_only=True)):
                for q in range(2):
                    zin = zin_ref[0, q, t1, rows, lanes].astype(F32)
                    gate = gate_ref[0, q, t1, rows, lanes].astype(F32)
                    o_ref[0, q, t1, rows, lanes] = (
                        gate * _r_add(bias * zin, parts[q])).astype(o_ref.dtype)
        return carry

    lax.fori_loop(0, rows_total // ROW_CHUNK, body, 0)


def _radix_inv(sb, gate_src, gate_block, zin_src, zin_block, bias_row, name):
    p, radix, _, m, ncols = sb.shape
    half = radix // 2
    tt, cb = min(256, m), 256
    goff, zoff = gate_block * (H_WIDTH // cb), zin_block * (H_WIDTH // cb)
    tok = lambda off: pl.BlockSpec((1, 2, half, tt, cb), lambda pi, ti, ci: (pi, 0, 0, ti, off + ci))
    return pl.pallas_call(
        _radix_inv_kernel,
        grid=(p, m // tt, ncols // cb),
        in_specs=[pl.BlockSpec((1, radix, 2, tt, cb), lambda pi, ti, ci: (pi, 0, 0, ti, ci)),
                  tok(goff), tok(zoff),
                  pl.BlockSpec((1, cb), lambda pi, ti, ci: (0, ci))],
        out_specs=tok(0),
        out_shape=jax.ShapeDtypeStruct((p, 2, half, m, ncols), BF16),
        compiler_params=_params("parallel", "parallel", "parallel"),
        name=name,
    )(sb, gate_src, zin_src, bias_row)


def _dft_seed_kernel(cb_ref, sb_ref, ca_ref, sa_ref, *, n):
    m = cb_ref.shape[0]
    r = lax.broadcasted_iota(jnp.int32, (m, m), 0)
    c = lax.broadcasted_iota(jnp.int32, (m, m), 1)
    beta = ((r * c) & (m - 1)).astype(F32) * F32(2.0 * math.pi / m)
    cb_ref[...] = jnp.cos(beta)
    sb_ref[...] = jnp.sin(beta)
    k1 = lax.broadcasted_iota(jnp.int32, ca_ref.shape, 0)
    t2 = lax.broadcasted_iota(jnp.int32, ca_ref.shape, 1)
    alpha = (k1 * t2).astype(F32) * F32(2.0 * math.pi / n)
    ca_ref[...] = jnp.cos(alpha)
    sa_ref[...] = jnp.sin(alpha)


def _dft_table_kernel(cb_ref, sb_ref, car_ref, sar_ref, cac_ref, sac_ref, efwd_ref, einv_ref):
    m = cb_ref.shape[0]
    cb, sb = cb_ref[...], sb_ref[...]
    car, sar = car_ref[0], sar_ref[0]
    cos_f = (car * cb - sar * sb).astype(BF16)
    sin_f = (sar * cb + car * sb).astype(BF16)
    efwd_ref[0, :m, :m] = cos_f
    efwd_ref[0, :m, m:] = sin_f
    efwd_ref[0, m:, :m] = -sin_f
    efwd_ref[0, m:, m:] = cos_f
    cac, sac = cac_ref[0], sac_ref[0]
    cos_i = (cac * cb - sac * sb).astype(BF16)
    sin_i = (sac * cb + cac * sb).astype(BF16)
    einv_ref[0, :m, :m] = cos_i
    einv_ref[0, :m, m:] = -sin_i
    einv_ref[0, m:, :m] = sin_i
    einv_ref[0, m:, m:] = cos_i


def _dft_tables(n):
    m = n // FFT_RADIX
    cb, sb, ca, sa = pl.pallas_call(
        functools.partial(_dft_seed_kernel, n=n),
        out_shape=[jax.ShapeDtypeStruct((m, m), F32)] * 2
        + [jax.ShapeDtypeStruct((FFT_RADIX, m), F32)] * 2,
        compiler_params=pltpu.CompilerParams(vmem_limit_bytes=VMEM_LIMIT),
        name="dft_seed",
    )()
    full = pl.BlockSpec((m, m), lambda i: (0, 0))
    rowspec = pl.BlockSpec((1, 1, m), lambda i: (i, 0, 0))
    colspec = pl.BlockSpec((1, m, 1), lambda i: (i, 0, 0))
    tab = pl.BlockSpec((1, 2 * m, 2 * m), lambda i: (i, 0, 0))
    return pl.pallas_call(
        _dft_table_kernel,
        grid=(FFT_RADIX,),
        in_specs=[full, full, rowspec, rowspec, colspec, colspec],
        out_specs=[tab, tab],
        out_shape=[jax.ShapeDtypeStruct((FFT_RADIX, 2 * m, 2 * m), BF16)] * 2,
        compiler_params=_params("parallel"),
        name="dft_tables",
    )(cb, sb, ca.reshape(FFT_RADIX, 1, m), sa.reshape(FFT_RADIX, 1, m),
      ca.reshape(FFT_RADIX, m, 1), sa.reshape(FFT_RADIX, m, 1))


def _filter_spectrum_kernel(efwd_ref, hsum_ref, hdiff_ref, kr_ref, ki_ref, *, n):
    m = kr_ref.shape[1]
    kr_ref[0] = _dot(efwd_ref[0, :m, :], hsum_ref[0, 0]) * F32(1.0 / n)
    ki_ref[0] = _dot(efwd_ref[0, m:, :], hdiff_ref[0, 0]) * F32(1.0 / n)


def _filter_spectrum(efwd, sah):
    radix, m2, _ = efwd.shape
    n = radix * m2 // 2
    orders = sah.shape[3] // (2 * H_WIDTH)
    slab = lambda off: pl.BlockSpec((1, 1, m2, H_WIDTH), lambda k1, o: (0, k1, 0, off + o))
    kspec = pl.BlockSpec((1, m2 // 2, H_WIDTH), lambda k1, o: (k1, 0, o))
    return pl.pallas_call(
        functools.partial(_filter_spectrum_kernel, n=n),
        grid=(radix, orders),
        in_specs=[pl.BlockSpec((1, m2, m2), lambda k1, o: (k1, 0, 0)), slab(0), slab(orders)],
        out_specs=[kspec, kspec],
        out_shape=[jax.ShapeDtypeStruct((radix, m2 // 2, orders * H_WIDTH), F32)] * 2,
        compiler_params=_params("parallel", "arbitrary"),
        name="filter_spectrum",
    )(efwd, sah, sah)


def _spectral_kernel(efwd_ref, einv_ref, sa_ref, kr_ref, ki_ref, sb_ref):
    slabs, m = kr_ref.shape[0], kr_ref.shape[1]
    width = sa_ref.shape[3]
    cw = min(512, width)
    for s in range(slabs):
        for j in range(width // cw):
            cols = slice(j * cw, (j + 1) * cw)
            x = _dot(efwd_ref[s], sa_ref[0, s, :, cols])
            xr, xi = x[:m], x[m:]
            kr, ki = kr_ref[s, :, cols], ki_ref[s, :, cols]
            y = jnp.concatenate([xr * kr - xi * ki, xr * ki + xi * kr], axis=0).astype(BF16)
            sb_ref[0, s, :, cols] = _dot(einv_ref[s], y).astype(BF16)


def _spectral(efwd, einv, sa, kr, ki, order):
    p, radix, m2, ncols = sa.shape
    ns = SPECTRAL_SLABS
    slab = pl.BlockSpec((1, ns, m2, ncols), lambda k1, pi: (pi, k1, 0, 0))
    tab = pl.BlockSpec((ns, m2, m2), lambda k1, pi: (k1, 0, 0))
    kspec = pl.BlockSpec((ns, m2 // 2, ncols), lambda k1, pi: (k1, 0, order))
    return pl.pallas_call(
        _spectral_kernel,
        grid=(radix // ns, p),
        in_specs=[tab, tab, slab, kspec, kspec],
        out_specs=slab,
        out_shape=jax.ShapeDtypeStruct(sa.shape, BF16),
        compiler_params=_params("parallel", "arbitrary"),
        name="hyena_spectral",
    )(efwd, einv, sa, kr, ki)


def _output_kernel(yf_ref, yb_ref, z_ref, hg_ref, gm_ref, gh_ref, yh_ref, x_ref, gate_ref, mnw_ref,
                   fnw_ref, wm_ref, wh_ref, wo_ref, o_ref):
    f32 = lambda ref: ref[0].astype(F32)
    g = (f32(yf_ref) + f32(yb_ref)) * _silu(f32(z_ref))
    gw = D_MODEL // M_GROUPS
    parts = []
    for i in range(M_GROUPS):
        gi = g[:, i * gw:(i + 1) * gw]
        parts.append(gi * lax.rsqrt(jnp.mean(gi * gi, axis=-1, keepdims=True) + NORM_EPS))
    ym = (jnp.concatenate(parts, axis=1) * mnw_ref[...]).astype(BF16)
    yh = (f32(yh_ref) * _silu(f32(hg_ref))).astype(BF16)
    merged = (jax.nn.sigmoid(f32(gm_ref)) * _dot(ym, wm_ref[...])
              + jax.nn.sigmoid(f32(gh_ref)) * _dot(yh, wh_ref[...]))
    out = _dot(merged.astype(BF16), wo_ref[...])
    xn = x_ref[0] + gate_ref[0] * out
    ms = jnp.mean(xn * xn, axis=-1, keepdims=True)
    o_ref[0] = xn * lax.rsqrt(ms + NORM_EPS) * fnw_ref[...]


def _mixer_output(y_f, y_b, cols, yh, x, gate, m_norm_w, final_norm_w, wm, wh, wo):
    b, l, d = x.shape
    tm = min(512, l)
    tok = lambda blk: pl.BlockSpec((1, tm, d), lambda bi, i: (bi, i, blk))
    wspec = pl.BlockSpec((d, d), lambda bi, i: (0, 0))
    rowspec = pl.BlockSpec((1, d), lambda bi, i: (0, 0))
    return pl.pallas_call(
        _output_kernel,
        grid=(b, l // tm),
        in_specs=[tok(0), tok(0),
                  tok(COL_Z), tok(COL_HGATE), tok(COL_GM), tok(COL_GH), tok(0), tok(0),
                  pl.BlockSpec((1, 1, d), lambda bi, i: (bi, 0, 0)),
                  rowspec, rowspec, wspec, wspec, wspec],
        out_specs=tok(0),
        out_shape=jax.ShapeDtypeStruct((b, l, d), F32),
        compiler_params=_params("parallel", "parallel"),
        name="mixer_output",
    )(y_f, y_b, cols, cols, cols, cols, yh, x, gate, m_norm_w.reshape(1, d),
      final_norm_w.reshape(1, d), wm, wh, wo)


def _layer(x, ctx, c, c_ctx, ada_w, ada_b, norm_w, w_in, m_conv_w, m_conv_b, m_dt_bias, m_a_log,
           m_d, m_norm_w, h_conv_w, h_conv_b, h_w1, h_b1, h_w2, h_b2, h_w3, h_b3, h_freq,
           h_w_out, h_bias, w_branch_m, w_branch_h, w_out, final_norm_w):
    b, l, d = x.shape
    lc = ctx.shape[1]
    ndt = 2 * M_HEADS

    pad_rows = -(b + 1) % SUBLANES
    c_rows = jnp.concatenate([c, c_ctx[None], jnp.zeros((pad_rows, d), F32)], axis=0)
    mod = _adaln(c_rows, ada_w, ada_b)
    shift, scale, gate = (mod[:b, i * d:(i + 1) * d].reshape(b, 1, d) for i in range(3))
    shift_c = jnp.broadcast_to(mod[b, :d], (b, 1, d))
    scale_c = jnp.broadcast_to(mod[b, d:2 * d], (b, 1, d))

    o_z, o_xbc, o_dt = 0, d, d + M_XBC
    o_hg = o_dt + ndt
    o_hp = o_hg + H_WIDTH
    o_gm = o_hp + 3 * H_WIDTH
    w_xbc = w_in[:, o_xbc:o_dt].astype(BF16)
    w_conv = jnp.concatenate([w_xbc, w_in[:, o_hp:o_gm].astype(BF16)], axis=1)
    w_plain = jnp.concatenate([w_in[:, o_z:o_xbc], w_in[:, o_hg:o_hp], w_in[:, o_gm:]],
                              axis=1).astype(BF16)
    w_dt_pad = jnp.zeros((d, DT_PAD), F32).at[:, :ndt].set(w_in[:, o_dt:o_hg])
    nw = norm_w.reshape(1, d)
    conv_w = jnp.concatenate([m_conv_w, h_conv_w], axis=1)
    conv_b = jnp.concatenate([m_conv_b, h_conv_b]).reshape(1, -1)
    n_silu = M_XBC // CONV_TN

    bias_r = jnp.zeros((1, DT_PAD), F32).at[0, :ndt].set(m_dt_bias.reshape(ndt))
    cols, cols_p, dt_l = _inproj(x, shift, scale, nw, w_conv, conv_w, conv_b, n_silu, w_plain,
                                 w_dt_pad, bias_r, min(1024, l))
    cols_c, _, dt_c = _inproj(ctx, shift_c, scale_c, nw, w_xbc, conv_w[:, :M_XBC],
                              conv_b[:, :M_XBC], n_silu, None, w_dt_pad, bias_r, min(1024, lc))

    a_r = jnp.zeros((1, DT_PAD), F32).at[0, :ndt].set(m_a_log.reshape(ndt))
    dskip_row = jnp.repeat(m_d, M_HEADDIM).reshape(1, d)
    y_f, y_b = _ssd(cols, cols_c, dt_l, dt_c, a_r, dskip_row)

    assert b % 2 == 0, "batch rows are packed in pairs as complex signals"
    half = FFT_RADIX // 2
    m = l // half
    pairs = b // 2
    u3 = cols.reshape(pairs, 2, half, m, cols.shape[2])
    filt = _hyena_filters(l, h_w1, h_b1, h_w2, h_b2, h_w3, h_b3, h_freq, h_w_out)
    efwd, einv = _dft_tables(2 * l)
    sah = _radix_fwd(filt.reshape(1, 1, half, m, filt.shape[1]), 0, filt.shape[1], True,
                     "filter_radix")
    kr, ki = _filter_spectrum(efwd, sah.reshape(1, FFT_RADIX, 2 * m, filt.shape[1]))
    z, z_block = u3, COL_HPROJ
    for order in range(2):
        sa = _radix_fwd(z, z_block, H_WIDTH, False, "hyena_radix_fwd")
        sb = _spectral(efwd, einv, sa.reshape(pairs, FFT_RADIX, 2 * m, H_WIDTH), kr, ki, order)
        z = _radix_inv(sb.reshape(pairs, FFT_RADIX, 2, m, H_WIDTH), u3, COL_HPROJ + 1 + order,
                       z, z_block, h_bias[order].reshape(1, H_WIDTH), "hyena_radix_inv")
        z_block = 0
    y_h = z.reshape(b, l, H_WIDTH)

    return _mixer_output(y_f, y_b, cols_p, y_h, x, gate, m_norm_w, final_norm_w,
                         w_branch_m.astype(BF16), w_branch_h.astype(BF16), w_out.astype(BF16))


def kernel(x, c, ctx, c_ctx, ada_w, ada_b, norm_w, w_in, m_conv_w, m_conv_b, m_dt_bias, m_a_log,
           m_d, m_norm_w, h_conv_w, h_conv_b, h_w1, h_b1, h_w2, h_b2, h_w3, h_b3, h_freq,
           h_w_out, h_bias, w_branch_m, w_branch_h, w_out, final_norm_w):
    assert ada_w.shape[0] == 1, "one trunk layer"
    return _layer(x, ctx, c, c_ctx, ada_w[0], ada_b[0], norm_w[0], w_in[0], m_conv_w[0],
                  m_conv_b[0], m_dt_bias[0], m_a_log[0], m_d[0], m_norm_w[0], h_conv_w[0],
                  h_conv_b[0], h_w1[0], h_b1[0], h_w2[0], h_b2[0], h_w3[0], h_b3[0], h_freq[0],
                  h_w_out[0], h_bias[0], w_branch_m[0], w_branch_h[0], w_out[0], final_norm_w)
```

```python
import functools
import math

import jax
import jax.numpy as jnp
from jax import lax
from jax.experimental import pallas as pl
from jax.experimental.pallas import tpu as pltpu

F32 = jnp.float32
BF16 = jnp.bfloat16
HIGHEST = lax.Precision.HIGHEST

LANES = 128
SUBLANES = 8
VMEM_LIMIT = 56 * 1024 * 1024

NORM_EPS = 1e-6
D_MODEL = 1024
M_HEADDIM = 64
M_HEADS = 16
M_GROUPS = 4
M_HPG = M_HEADS // M_GROUPS
M_STATE = 128
M_XBC = D_MODEL + 2 * M_GROUPS * M_STATE
M_CHUNK = 256
HEADDIM_SHIFT = M_HEADDIM.bit_length() - 1
CHUNK_SHIFT = M_CHUNK.bit_length() - 1
H_WIDTH = 1024
H_EMB = 33
H_HID = 64
H_DECAY_TARGET = 1e-2
H_FAST_DECAY_PCT = 0.3
H_SLOW_DECAY_PCT = 1.5
DT_PAD = LANES
FFT_RADIX = 32
ROW_CHUNK = 16
SPECTRAL_SLABS = 4
PROJ_SLAB = 256

COL_HPROJ = 2
COL_Z, COL_HGATE, COL_GM, COL_GH = 0, 1, 2, 3
CONV_TN = 1024
PLAIN_TN = 2048


def _dot(a, b, precision=None):
    return jnp.dot(a, b, preferred_element_type=F32, precision=precision)


def _dot_nt(a, b, precision=None):
    return lax.dot_general(a, b, (((1,), (1,)), ((), ())), preferred_element_type=F32,
                           precision=precision)


def _dot_tn(a, b, precision=None):
    return lax.dot_general(a, b, (((0,), (0,)), ((), ())), preferred_element_type=F32,
                           precision=precision)


def _params(*sem):
    return pltpu.CompilerParams(dimension_semantics=sem, vmem_limit_bytes=VMEM_LIMIT)


def _silu(v):
    return v * jax.nn.sigmoid(v)


def _adaln_kernel(c_ref, w_ref, b_ref, o_ref):
    o_ref[...] = _dot(_silu(c_ref[...]), w_ref[...], HIGHEST) + b_ref[...]


def _adaln(c_rows, ada_w, ada_b):
    rows, d = c_rows.shape
    n = ada_w.shape[1]
    tn = 1024
    return pl.pallas_call(
        _adaln_kernel,
        grid=(n // tn,),
        in_specs=[pl.BlockSpec((rows, d), lambda j: (0, 0)),
                  pl.BlockSpec((d, tn), lambda j: (0, j)),
                  pl.BlockSpec((1, tn), lambda j: (0, j))],
        out_specs=pl.BlockSpec((rows, tn), lambda j: (0, j)),
        out_shape=jax.ShapeDtypeStruct((rows, n), F32),
        compiler_params=_params("parallel"),
        name="adaln",
    )(c_rows, ada_w, ada_b.reshape(1, n))


def _split_bf16(v):
    hi = v.astype(BF16)
    return hi, (v - hi.astype(F32)).astype(BF16)


def _softplus(v):
    return jnp.maximum(v, 0.0) + jnp.log1p(jnp.exp(-jnp.abs(v)))


def _conv_rows(p_ext, tm, w, bias, has_prev, has_next, apply_silu, rows_ref):
    u = p_ext[0:tm, :]
    lo = SUBLANES
    rows_ref[lo:lo + tm, :] = u
    rows_ref[lo - 1:lo, :] = jnp.where(has_prev, p_ext[tm + SUBLANES - 1:tm + SUBLANES, :], 0.0)
    rows_ref[lo + tm:lo + tm + 1, :] = jnp.where(has_next,
                                                 p_ext[tm + SUBLANES:tm + SUBLANES + 1, :], 0.0)
    up = rows_ref[lo - 1:lo - 1 + tm, :]
    un = rows_ref[lo + 1:lo + 1 + tm, :]
    y = w[0:1, :] * up + w[1:2, :] * u + w[2:3, :] * un + bias
    return _silu(y) if apply_silu else y


def _inproj_kernel(*refs, n_silu, n_conv, has_plain):
    if has_plain:
        (x_ref, xprev_ref, xnext_ref, shift_ref, scale_ref, nw_ref, wc_ref, cw_ref, cb_ref, wp_ref,
         wdt_hi_ref, wdt_lo_ref, dtb_ref, oc_ref, op_ref, dt_ref, h_scr, rows_scr) = refs
    else:
        (x_ref, xprev_ref, xnext_ref, shift_ref, scale_ref, nw_ref, wc_ref, cw_ref, cb_ref,
         wdt_hi_ref, wdt_lo_ref, dtb_ref, oc_ref, dt_ref, h_scr, rows_scr) = refs
    i = pl.program_id(1)
    j = pl.program_id(2)
    tm = x_ref.shape[1]

    def modulated(xt):
        ms = jnp.mean(xt * xt, axis=-1, keepdims=True)
        hn = xt * lax.rsqrt(ms + NORM_EPS) * nw_ref[...]
        return hn * (1.0 + scale_ref[0]) + shift_ref[0]

    @pl.when(j == 0)
    def _():
        hi, lo = _split_bf16(modulated(x_ref[0]))
        h_scr[0:tm, :] = hi
        halo = jnp.concatenate([xprev_ref[0], xnext_ref[0]], axis=0)
        h_scr[tm:, :] = modulated(halo).astype(BF16)
        dt_raw = (_dot(hi, wdt_hi_ref[...]) + _dot(lo, wdt_hi_ref[...])
                  + _dot(hi, wdt_lo_ref[...]))
        dt_ref[0] = _softplus(dt_raw + dtb_ref[...])

    has_prev, has_next = i > 0, i < pl.num_programs(1) - 1

    def project(mode, w_ref, o_ref):
        for c in range(o_ref.shape[2] // PROJ_SLAB):
            cols = slice(c * PROJ_SLAB, (c + 1) * PROJ_SLAB)
            p_ext = _dot(h_scr[...], w_ref[:, cols])
            if mode == "plain":
                y = p_ext[0:tm, :]
            else:
                y = _conv_rows(p_ext, tm, cw_ref[:, cols], cb_ref[:, cols], has_prev, has_next,
                               mode == "silu", rows_scr.at[c % 2])
            o_ref[0, :, cols] = y.astype(o_ref.dtype)

    pl.when(j < n_silu)(lambda: project("silu", wc_ref, oc_ref))
    if n_conv > n_silu:
        pl.when(jnp.logical_and(j >= n_silu, j < n_conv))(lambda: project("conv", wc_ref, oc_ref))
    if has_plain:
        pl.when(j >= n_conv)(lambda: project("plain", wp_ref, op_ref))


def _inproj(x, shift, scale, norm_w, w_conv, conv_w, conv_b, n_silu, w_plain, w_dt, dt_bias, tm):
    b, l, d = x.shape
    nct = w_conv.shape[1] // CONV_TN
    npt = 0 if w_plain is None else w_plain.shape[1] // PLAIN_TN
    nsub = tm // SUBLANES
    last = l // SUBLANES - 1
    w_dt_hi, w_dt_lo = _split_bf16(w_dt)
    conv_j = lambda j: jnp.minimum(j, nct - 1)
    plain_j = lambda j: jnp.maximum(j - nct, 0)
    const = lambda shape: pl.BlockSpec(shape, lambda bi, i, j: tuple(0 for _ in shape))
    in_specs = [pl.BlockSpec((1, tm, d), lambda bi, i, j: (bi, i, 0)),
                pl.BlockSpec((1, SUBLANES, d),
                             lambda bi, i, j: (bi, jnp.maximum(i * nsub - 1, 0), 0)),
                pl.BlockSpec((1, SUBLANES, d),
                             lambda bi, i, j: (bi, jnp.minimum((i + 1) * nsub, last), 0)),
                pl.BlockSpec((1, 1, d), lambda bi, i, j: (bi, 0, 0)),
                pl.BlockSpec((1, 1, d), lambda bi, i, j: (bi, 0, 0)),
                const((1, d)),
                pl.BlockSpec((d, CONV_TN), lambda bi, i, j: (0, conv_j(j))),
                pl.BlockSpec((3, CONV_TN), lambda bi, i, j: (0, conv_j(j))),
                pl.BlockSpec((1, CONV_TN), lambda bi, i, j: (0, conv_j(j)))]
    args = [x, x, x, shift, scale, norm_w, w_conv, conv_w, conv_b]
    out_specs = [pl.BlockSpec((1, tm, CONV_TN), lambda bi, i, j: (bi, i, conv_j(j)))]
    out_shape = [jax.ShapeDtypeStruct((b, l, w_conv.shape[1]), BF16)]
    if npt:
        in_specs.append(pl.BlockSpec((d, PLAIN_TN), lambda bi, i, j: (0, plain_j(j))))
        args.append(w_plain)
        out_specs.append(pl.BlockSpec((1, tm, PLAIN_TN), lambda bi, i, j: (bi, i, plain_j(j))))
        out_shape.append(jax.ShapeDtypeStruct((b, l, w_plain.shape[1]), BF16))
    in_specs += [const((d, DT_PAD)), const((d, DT_PAD)), const((1, DT_PAD))]
    args += [w_dt_hi, w_dt_lo, dt_bias]
    out_specs.append(pl.BlockSpec((1, tm, DT_PAD), lambda bi, i, j: (bi, i, 0)))
    out_shape.append(jax.ShapeDtypeStruct((b, l, DT_PAD), F32))
    outs = pl.pallas_call(
        functools.partial(_inproj_kernel, n_silu=n_silu, n_conv=nct, has_plain=npt > 0),
        grid=(b, l // tm, nct + npt),
        in_specs=in_specs,
        out_specs=out_specs,
        out_shape=out_shape,
        scratch_shapes=[pltpu.VMEM((tm + 2 * SUBLANES, d), BF16),
                        pltpu.VMEM((2, tm + 2 * SUBLANES, PROJ_SLAB), F32)],
        compiler_params=_params("parallel", "parallel", "arbitrary"),
        name="inproj",
    )(*args)
    return (outs[0], outs[1], outs[2]) if npt else (outs[0], None, outs[1])


def _cumsum_dot(tri_bf16, v):
    hi = v.astype(BF16)
    rest = v - hi.astype(F32)
    mid = rest.astype(BF16)
    lo = (rest - mid.astype(F32)).astype(BF16)
    return _dot(tri_bf16, hi) + _dot(tri_bf16, mid) + _dot(tri_bf16, lo)


def _expand_heads(v, e_bf16):
    hi = v.astype(BF16)
    lo = (v - hi.astype(F32)).astype(BF16)
    return _dot(hi, e_bf16) + _dot(lo, e_bf16)


def _ssd_chunk(fwd, xbc, dt_all, a_row, dskip_row, y_ref, state):
    q = M_CHUNK
    off = 0 if fwd else M_HEADS
    last = q - 1 if fwd else 0
    x = xbc[:, :D_MODEL].astype(F32)
    bmat = xbc[:, D_MODEL:D_MODEL + M_GROUPS * M_STATE]
    cmat = xbc[:, D_MODEL + M_GROUPS * M_STATE:]

    a_all = dt_all * a_row
    ri = lax.broadcasted_iota(jnp.int32, (q, q), 0)
    ci = lax.broadcasted_iota(jnp.int32, (q, q), 1)
    mask = (ri >= ci) if fwd else (ri <= ci)
    acs_all = _cumsum_dot(jnp.where(mask, 1.0, 0.0).astype(BF16), a_all)
    acs_t = acs_all.T[off:off + M_HEADS, :]
    acs = acs_all[:, off:off + M_HEADS]
    e_atot_c = jnp.exp(acs_t[:, last:last + 1])

    hid = lax.broadcasted_iota(jnp.int32, (M_HEADS, D_MODEL), 0)
    lid = lax.broadcasted_iota(jnp.int32, (M_HEADS, D_MODEL), 1)
    expand = jnp.where((lid >> HEADDIM_SHIFT) == hid, 1.0, 0.0).astype(BF16)
    per_head = jnp.concatenate([dt_all[:, off:off + M_HEADS],
                                jnp.exp(acs[last:last + 1, :] - acs),
                                jnp.exp(acs)], axis=0)
    wide = _dot(per_head.astype(BF16), expand)
    xdt = x * wide[:q]
    xw = (xdt * wide[q:2 * q]).astype(BF16)
    e_acs = wide[2 * q:]

    gw = M_HPG * M_HEADDIM
    hb = q // 2
    halves = (slice(0, hb), slice(hb, q))
    far_rows, far_cols = (halves[1], halves[0]) if fwd else (halves[0], halves[1])
    tri = mask[:hb, :hb]
    rb = lax.broadcasted_iota(jnp.int32, (M_HPG * hb, gw), 0) >> (CHUNK_SHIFT - 1)
    cb = lax.broadcasted_iota(jnp.int32, (M_HPG * hb, gw), 1) >> HEADDIM_SHIFT
    blockdiag = rb == cb
    for g in range(M_GROUPS):
        bg = bmat[:, g * M_STATE:(g + 1) * M_STATE]
        cg = cmat[:, g * M_STATE:(g + 1) * M_STATE]
        scores = _dot_nt(cg, bg)
        near, far = ([], []), []
        for r in range(M_HPG):
            h = g * M_HPG + r
            col, row = acs[:, h:h + 1], acs_t[h:h + 1, :]
            for k, sl in enumerate(halves):
                decay = jnp.exp(jnp.where(tri, col[sl] - row[:, sl], -jnp.inf))
                near[k].append((scores[sl, sl] * decay).astype(BF16))
            decay = jnp.exp(col[far_rows] - row[:, far_cols])
            far.append((scores[far_rows, far_cols] * decay).astype(BF16))
        xg = xdt[:, g * gw:(g + 1) * gw]
        x_bd = [jnp.where(blockdiag, jnp.concatenate([xg[sl]] * M_HPG, axis=0), 0.0).astype(BF16)
                for sl in halves]
        x_all = jnp.concatenate(x_bd, axis=0)
        if fwd:
            y_top = _dot(jnp.concatenate(near[0], axis=1), x_bd[0])
            y_bot = _dot(jnp.concatenate(far + near[1], axis=1), x_all)
        else:
            y_top = _dot(jnp.concatenate(near[0] + far, axis=1), x_all)
            y_bot = _dot(jnp.concatenate(near[1], axis=1), x_bd[1])
        y_diag = jnp.concatenate([y_top, y_bot], axis=0)
        s_g = state[g * M_HPG:(g + 1) * M_HPG].reshape(gw, M_STATE).astype(BF16)
        y_g = y_diag + e_acs[:, g * gw:(g + 1) * gw] * _dot_nt(cg, s_g)
        if fwd:
            y_g = y_g + dskip_row[:, g * gw:(g + 1) * gw] * x[:, g * gw:(g + 1) * gw]
        y_ref[0, :, g * gw:(g + 1) * gw] = y_g.astype(y_ref.dtype)
        upd = _dot_tn(xw[:, g * gw:(g + 1) * gw], bg)
        for r in range(M_HPG):
            h = g * M_HPG + r
            state[h] = (state[h] * e_atot_c[h:h + 1, 0:1]
                        + upd[r * M_HEADDIM:(r + 1) * M_HEADDIM, :])


def _ssd_kernel(xlf_ref, xcf_ref, xlb_ref, xcb_ref, dtlf_ref, dtcf_ref, dtlb_ref, dtcb_ref, a_ref,
                dskip_ref, yf_ref, yb_ref, state_f, state_b, *, n_ctx_chunks):
    s = pl.program_id(1)
    is_ctx = s < n_ctx_chunks

    @pl.when(s == 0)
    def _():
        state_f[...] = jnp.zeros_like(state_f)
        state_b[...] = jnp.zeros_like(state_b)

    dt_f = jnp.where(is_ctx, dtcf_ref[0], dtlf_ref[0])
    dt_b = jnp.where(is_ctx, dtcb_ref[0], dtlb_ref[0])
    a_row = -jnp.exp(a_ref[...])
    x_f = jnp.where(is_ctx, xcf_ref[0], xlf_ref[0])
    x_b = jnp.where(is_ctx, xcb_ref[0], xlb_ref[0])
    _ssd_chunk(True, x_f, dt_f, a_row, dskip_ref[...], yf_ref, state_f)
    _ssd_chunk(False, x_b, dt_b, a_row, dskip_ref[...], yb_ref, state_b)


def _ssd(cols, cols_c, dt_l, dt_c, a_r, dskip_row):
    b, l, _ = cols.shape
    lc = cols_c.shape[1]
    q = M_CHUNK
    nl, nc = l // q, lc // q

    lat_f = lambda si: jnp.maximum(si - nc, 0)
    lat_b = lambda si: nl - 1 - lat_f(si)
    ctx_f = lambda si: jnp.minimum(si, nc - 1)
    ctx_b = lambda si: nc - 1 - ctx_f(si)

    small = lambda shape: pl.BlockSpec(shape, lambda bi, si: (0, 0))
    x_spec = lambda fn: pl.BlockSpec((1, q, M_XBC), lambda bi, si: (bi, fn(si), 0))
    dt_spec = lambda fn: pl.BlockSpec((1, q, DT_PAD), lambda bi, si: (bi, fn(si), 0))
    state = pltpu.VMEM((M_HEADS, M_HEADDIM, M_STATE), F32)
    return pl.pallas_call(
        functools.partial(_ssd_kernel, n_ctx_chunks=nc),
        grid=(b, nl + nc),
        in_specs=[x_spec(lat_f), x_spec(ctx_f), x_spec(lat_b), x_spec(ctx_b),
                  dt_spec(lat_f), dt_spec(ctx_f), dt_spec(lat_b), dt_spec(ctx_b),
                  small((1, DT_PAD)), small((1, D_MODEL))],
        out_specs=[pl.BlockSpec((1, q, D_MODEL), lambda bi, si: (bi, lat_f(si), 0)),
                   pl.BlockSpec((1, q, D_MODEL), lambda bi, si: (bi, lat_b(si), 0))],
        out_shape=[jax.ShapeDtypeStruct((b, l, D_MODEL), BF16)] * 2,
        scratch_shapes=[state, state],
        compiler_params=_params("parallel", "arbitrary"),
        name="ssd",
    )(cols, cols_c, cols, cols_c, dt_l, dt_c, dt_l, dt_c, a_r, dskip_row)


def _filter_kernel(fl_ref, w1_ref, b1_ref, w2_ref, b2_ref, w3_ref, b3_ref, freq_ref, wo_hi_ref,
                   wo_lo_ref, deltas_ref, h_ref, *, seqlen):
    i = pl.program_id(0)
    tr = h_ref.shape[0]
    pos = (i * tr + lax.broadcasted_iota(jnp.int32, (tr, 1), 0))
    posf = pos.astype(F32)
    t = posf / F32(seqlen - 1)
    w = F32(2.0 * math.pi / seqlen) * posf
    ang = fl_ref[...] * w
    lane = lax.broadcasted_iota(jnp.int32, (tr, LANES), 1)
    bands = (H_EMB - 1) // 2
    feats = jnp.where(lane == 0, t,
                      jnp.where(lane <= bands, jnp.cos(ang),
                                jnp.where(lane < H_EMB, -jnp.sin(ang), 0.0)))
    freq = freq_ref[...]
    hid = jnp.sin(freq * (_dot(feats, w1_ref[...], HIGHEST) + b1_ref[...]))
    hid = jnp.sin(freq * (_dot(hid, w2_ref[...], HIGHEST) + b2_ref[...]))
    hid = jnp.sin(freq * (_dot(hid, w3_ref[...], HIGHEST) + b3_ref[...]))
    hid_hi, hid_lo = _split_bf16(hid)
    filt = (_dot(hid_hi, wo_hi_ref[...]) + _dot(hid_lo, wo_hi_ref[...])
            + _dot(hid_hi, wo_lo_ref[...]))
    window = jnp.exp(-t * deltas_ref[...])
    hw = H_WIDTH
    orders = filt.shape[1] // (2 * hw)
    for o in range(orders):
        h_fwd = filt[:, o * hw:(o + 1) * hw] * window
        h_bwd = filt[:, (orders + o) * hw:(orders + o + 1) * hw] * window
        h_ref[:, o * hw:(o + 1) * hw] = h_fwd + h_bwd
        h_ref[:, (orders + o) * hw:(orders + o + 1) * hw] = h_fwd - h_bwd


def _hyena_filters(seqlen, w1, b1, w2, b2, w3, b3, freq, w_out):
    tr = min(512, seqlen)
    bands = (H_EMB - 1) // 2
    f = jnp.linspace(1e-4, bands - 1, bands, dtype=F32)
    fl = jnp.zeros((1, LANES), F32).at[0, 1:1 + bands].set(f).at[0, 1 + bands:H_EMB].set(f)
    w1p = jnp.zeros((LANES, H_HID), F32).at[:H_EMB].set(w1)
    min_decay = math.log(H_DECAY_TARGET) / H_SLOW_DECAY_PCT
    max_decay = math.log(H_DECAY_TARGET) / H_FAST_DECAY_PCT
    deltas = jnp.abs(jnp.linspace(min_decay, max_decay, H_WIDTH, dtype=F32)).reshape(1, H_WIDTH)
    row = lambda v: v.reshape(1, -1)
    full = lambda a: pl.BlockSpec(a.shape, lambda i: (0, 0))
    args = (fl, w1p, row(b1), w2, row(b2), w3, row(b3), row(freq), *_split_bf16(w_out), deltas)
    nout = w_out.shape[1]
    return pl.pallas_call(
        functools.partial(_filter_kernel, seqlen=seqlen),
        grid=(seqlen // tr,),
        in_specs=[full(a) for a in args],
        out_specs=pl.BlockSpec((tr, nout), lambda i: (i, 0)),
        out_shape=jax.ShapeDtypeStruct((seqlen, nout), F32),
        compiler_params=_params("parallel"),
        name="hyena_filters",
    )(*args)


class _Neg:
    def __init__(self, arr):
        self.arr = arr


def _r_neg(a):
    if a is None:
        return None
    return a.arr if isinstance(a, _Neg) else _Neg(a)


def _r_add(a, b):
    if a is None:
        return b
    if b is None:
        return a
    if isinstance(a, _Neg) and isinstance(b, _Neg):
        return _Neg(a.arr + b.arr)
    if isinstance(b, _Neg):
        return a - b.arr
    if isinstance(a, _Neg):
        return b - a.arr
    return a + b


def _r_sub(a, b):
    return _r_add(a, _r_neg(b))


def _r_scale(a, s):
    if a is None or s == 0.0:
        return None
    if isinstance(a, _Neg):
        a, s = a.arr, -s
    if s == 1.0:
        return a
    if s == -1.0:
        return _Neg(a)
    return a * s


def _r_value(a):
    return -a.arr if isinstance(a, _Neg) else a


def _c_mul_const(z, w):
    snap = lambda v: float(round(v)) if abs(v - round(v)) < 1e-12 else float(v)
    wr, wi = snap(w.real), snap(w.imag)
    zr, zi = z
    if wr != 0.0 and abs(abs(wr) - abs(wi)) < 1e-12:
        sr, si, c = math.copysign(1.0, wr), math.copysign(1.0, wi), abs(wr)
        return (_r_scale(_r_sub(_r_scale(zr, sr), _r_scale(zi, si)), c),
                _r_scale(_r_add(_r_scale(zr, si), _r_scale(zi, sr)), c))
    return (_r_sub(_r_scale(zr, wr), _r_scale(zi, wi)),
            _r_add(_r_scale(zr, wi), _r_scale(zi, wr)))


def _fft(xs, sign, first_half_only=False):
    n = len(xs)
    if n == 1:
        return list(xs)
    even, odd = _fft(xs[0::2], sign), _fft(xs[1::2], sign)
    out = [None] * n
    for k in range(n // 2):
        tr, ti = _c_mul_const(odd[k], complex(math.cos(2 * math.pi * k / n),
                                              sign * math.sin(2 * math.pi * k / n)))
        out[k] = (_r_add(even[k][0], tr), _r_add(even[k][1], ti))
        if not first_half_only:
            out[k + n // 2] = (_r_sub(even[k][0], tr), _r_sub(even[k][1], ti))
    return out[:n // 2] if first_half_only else out


def _tile_or_zero(v):
    return jnp.zeros((ROW_CHUNK, LANES), F32) if v is None else _r_value(v)


def _radix_fwd_kernel(z_ref, o_ref, *, real_input):
    _, _, half, rows_total, width = z_ref.shape

    def body(i, carry):
        rows = pl.ds(pl.multiple_of(i * ROW_CHUNK, ROW_CHUNK), ROW_CHUNK)
        for j in range(width // LANES):
            lanes = slice(j * LANES, (j + 1) * LANES)
            xs = []
            for t1 in range(half):
                re = z_ref[0, 0, t1, rows, lanes].astype(F32)
                im = None if real_input else z_ref[0, 1, t1, rows, lanes].astype(F32)
                xs.append((re, im))
            xs += [(None, None)] * half
            for k1, (re, im) in enumerate(_fft(xs, -1.0)):
                o_ref[0, k1, 0, rows, lanes] = _tile_or_zero(re).astype(BF16)
                o_ref[0, k1, 1, rows, lanes] = _tile_or_zero(im).astype(BF16)
        return carry

    lax.fori_loop(0, rows_total // ROW_CHUNK, body, 0)


def _radix_fwd(z, col_block, ncols, real_input, name):
    p, parts, half, m, _ = z.shape
    tt, cb = min(256, m), 256
    off = col_block * (H_WIDTH // cb)
    return pl.pallas_call(
        functools.partial(_radix_fwd_kernel, real_input=real_input),
        grid=(p, m // tt, ncols // cb),
        in_specs=[pl.BlockSpec((1, parts, half, tt, cb), lambda pi, ti, ci: (pi, 0, 0, ti, off + ci))],
        out_specs=pl.BlockSpec((1, 2 * half, 2, tt, cb), lambda pi, ti, ci: (pi, 0, 0, ti, ci)),
        out_shape=jax.ShapeDtypeStruct((p, 2 * half, 2, m, ncols), BF16),
        compiler_params=_params("parallel", "parallel", "parallel"),
        name=name,
    )(z)


def _radix_inv_kernel(sb_ref, gate_ref, zin_ref, bias_ref, o_ref):
    _, radix, _, rows_total, width = sb_ref.shape

    def body(i, carry):
        rows = pl.ds(pl.multiple_of(i * ROW_CHUNK, ROW_CHUNK), ROW_CHUNK)
        for j in range(width // LANES):
            lanes = slice(j * LANES, (j + 1) * LANES)
            ys = [(sb_ref[0, k1, 0, rows, lanes].astype(F32),
                   sb_ref[0, k1, 1, rows, lanes].astype(F32)) for k1 in range(radix)]
            bias = bias_ref[:, lanes]
            for t1, parts in enumerate(_fft(ys, 1.0, first_half_only=True)):
                for q in range(2):
                    zin = zin_ref[0, q, t1, rows, lanes].astype(F32)
                    gate = gate_ref[0, q, t1, rows, lanes].astype(F32)
                    o_ref[0, q, t1, rows, lanes] = (
                        gate * _r_add(bias * zin, parts[q])).astype(o_ref.dtype)
        return carry

    lax.fori_loop(0, rows_total // ROW_CHUNK, body, 0)


def _radix_inv(sb, gate_src, gate_block, zin_src, zin_block, bias_row, name):
    p, radix, _, m, ncols = sb.shape
    half = radix // 2
    tt, cb = min(256, m), 256
    goff, zoff = gate_block * (H_WIDTH // cb), zin_block * (H_WIDTH // cb)
    tok = lambda off: pl.BlockSpec((1, 2, half, tt, cb), lambda pi, ti, ci: (pi, 0, 0, ti, off + ci))
    return pl.pallas_call(
        _radix_inv_kernel,
        grid=(p, m // tt, ncols // cb),
        in_specs=[pl.BlockSpec((1, radix, 2, tt, cb), lambda pi, ti, ci: (pi, 0, 0, ti, ci)),
                  tok(goff), tok(zoff),
                  pl.BlockSpec((1, cb), lambda pi, ti, ci: (0, ci))],
        out_specs=tok(0),
        out_shape=jax.ShapeDtypeStruct((p, 2, half, m, ncols), BF16),
        compiler_params=_params("parallel", "parallel", "parallel"),
        name=name,
    )(sb, gate_src, zin_src, bias_row)


def _dft_seed_kernel(cb_ref, sb_ref, ca_ref, sa_ref, *, n):
    m = cb_ref.shape[0]
    r = lax.broadcasted_iota(jnp.int32, (m, m), 0)
    c = lax.broadcasted_iota(jnp.int32, (m, m), 1)
    beta = ((r * c) & (m - 1)).astype(F32) * F32(2.0 * math.pi / m)
    cb_ref[...] = jnp.cos(beta)
    sb_ref[...] = jnp.sin(beta)
    k1 = lax.broadcasted_iota(jnp.int32, ca_ref.shape, 0)
    t2 = lax.broadcasted_iota(jnp.int32, ca_ref.shape, 1)
    alpha = (k1 * t2).astype(F32) * F32(2.0 * math.pi / n)
    ca_ref[...] = jnp.cos(alpha)
    sa_ref[...] = jnp.sin(alpha)


def _dft_table_kernel(cb_ref, sb_ref, car_ref, sar_ref, cac_ref, sac_ref, efwd_ref, einv_ref):
    m = cb_ref.shape[0]
    cb, sb = cb_ref[...], sb_ref[...]
    car, sar = car_ref[0], sar_ref[0]
    cos_f = (car * cb - sar * sb).astype(BF16)
    sin_f = (sar * cb + car * sb).astype(BF16)
    efwd_ref[0, :m, :m] = cos_f
    efwd_ref[0, :m, m:] = sin_f
    efwd_ref[0, m:, :m] = -sin_f
    efwd_ref[0, m:, m:] = cos_f
    cac, sac = cac_ref[0], sac_ref[0]
    cos_i = (cac * cb - sac * sb).astype(BF16)
    sin_i = (sac * cb + cac * sb).astype(BF16)
    einv_ref[0, :m, :m] = cos_i
    einv_ref[0, :m, m:] = -sin_i
    einv_ref[0, m:, :m] = sin_i
    einv_ref[0, m:, m:] = cos_i


def _dft_tables(n):
    m = n // FFT_RADIX
    cb, sb, ca, sa = pl.pallas_call(
        functools.partial(_dft_seed_kernel, n=n),
        out_shape=[jax.ShapeDtypeStruct((m, m), F32)] * 2
        + [jax.ShapeDtypeStruct((FFT_RADIX, m), F32)] * 2,
        compiler_params=pltpu.CompilerParams(vmem_limit_bytes=VMEM_LIMIT),
        name="dft_seed",
    )()
    full = pl.BlockSpec((m, m), lambda i: (0, 0))
    rowspec = pl.BlockSpec((1, 1, m), lambda i: (i, 0, 0))
    colspec = pl.BlockSpec((1, m, 1), lambda i: (i, 0, 0))
    tab = pl.BlockSpec((1, 2 * m, 2 * m), lambda i: (i, 0, 0))
    return pl.pallas_call(
        _dft_table_kernel,
        grid=(FFT_RADIX,),
        in_specs=[full, full, rowspec, rowspec, colspec, colspec],
        out_specs=[tab, tab],
        out_shape=[jax.ShapeDtypeStruct((FFT_RADIX, 2 * m, 2 * m), BF16)] * 2,
        compiler_params=_params("parallel"),
        name="dft_tables",
    )(cb, sb, ca.reshape(FFT_RADIX, 1, m), sa.reshape(FFT_RADIX, 1, m),
      ca.reshape(FFT_RADIX, m, 1), sa.reshape(FFT_RADIX, m, 1))


def _filter_spectrum_kernel(efwd_ref, hsum_ref, hdiff_ref, kr_ref, ki_ref, *, n):
    m = kr_ref.shape[1]
    kr_ref[0] = _dot(efwd_ref[0, :m, :], hsum_ref[0, 0]) * F32(1.0 / n)
    ki_ref[0] = _dot(efwd_ref[0, m:, :], hdiff_ref[0, 0]) * F32(1.0 / n)


def _filter_spectrum(efwd, sah):
    radix, m2, _ = efwd.shape
    n = radix * m2 // 2
    orders = sah.shape[3] // (2 * H_WIDTH)
    slab = lambda off: pl.BlockSpec((1, 1, m2, H_WIDTH), lambda k1, o: (0, k1, 0, off + o))
    kspec = pl.BlockSpec((1, m2 // 2, H_WIDTH), lambda k1, o: (k1, 0, o))
    return pl.pallas_call(
        functools.partial(_filter_spectrum_kernel, n=n),
        grid=(radix, orders),
        in_specs=[pl.BlockSpec((1, m2, m2), lambda k1, o: (k1, 0, 0)), slab(0), slab(orders)],
        out_specs=[kspec, kspec],
        out_shape=[jax.ShapeDtypeStruct((radix, m2 // 2, orders * H_WIDTH), F32)] * 2,
        compiler_params=_params("parallel", "arbitrary"),
        name="filter_spectrum",
    )(efwd, sah, sah)


def _spectral_kernel(efwd_ref, einv_ref, sa_ref, kr_ref, ki_ref, sb_ref):
    slabs, m = kr_ref.shape[0], kr_ref.shape[1]
    width = sa_ref.shape[3]
    cw = min(PROJ_SLAB, width)
    for s in range(slabs):
        for j in range(width // cw):
            cols = slice(j * cw, (j + 1) * cw)
            x = _dot(efwd_ref[s], sa_ref[0, s, :, cols])
            xr, xi = x[:m], x[m:]
            kr, ki = kr_ref[s, :, cols], ki_ref[s, :, cols]
            y = jnp.concatenate([xr * kr - xi * ki, xr * ki + xi * kr], axis=0).astype(BF16)
            sb_ref[0, s, :, cols] = _dot(einv_ref[s], y).astype(BF16)


def _spectral(efwd, einv, sa, kr, ki, order):
    p, radix, m2, ncols = sa.shape
    ns = SPECTRAL_SLABS
    slab = pl.BlockSpec((1, ns, m2, ncols), lambda k1, pi: (pi, k1, 0, 0))
    tab = pl.BlockSpec((ns, m2, m2), lambda k1, pi: (k1, 0, 0))
    kspec = pl.BlockSpec((ns, m2 // 2, ncols), lambda k1, pi: (k1, 0, order))
    return pl.pallas_call(
        _spectral_kernel,
        grid=(radix // ns, p),
        in_specs=[tab, tab, slab, kspec, kspec],
        out_specs=slab,
        out_shape=jax.ShapeDtypeStruct(sa.shape, BF16),
        compiler_params=_params("parallel", "arbitrary"),
        name="hyena_spectral",
    )(efwd, einv, sa, kr, ki)


def _output_kernel(yf_ref, yb_ref, z_ref, hg_ref, gm_ref, gh_ref, yh_ref, x_ref, gate_ref, mnw_ref,
                   fnw_ref, wm_ref, wh_ref, wo_ref, o_ref):
    f32 = lambda ref: ref[0].astype(F32)
    g = (f32(yf_ref) + f32(yb_ref)) * _silu(f32(z_ref))
    gw = D_MODEL // M_GROUPS
    parts = []
    for i in range(M_GROUPS):
        gi = g[:, i * gw:(i + 1) * gw]
        parts.append(gi * lax.rsqrt(jnp.mean(gi * gi, axis=-1, keepdims=True) + NORM_EPS))
    ym = (jnp.concatenate(parts, axis=1) * mnw_ref[...]).astype(BF16)
    yh = (f32(yh_ref) * _silu(f32(hg_ref))).astype(BF16)
    merged = (jax.nn.sigmoid(f32(gm_ref)) * _dot(ym, wm_ref[...])
              + jax.nn.sigmoid(f32(gh_ref)) * _dot(yh, wh_ref[...]))
    out = _dot(merged.astype(BF16), wo_ref[...])
    xn = x_ref[0] + gate_ref[0] * out
    ms = jnp.mean(xn * xn, axis=-1, keepdims=True)
    o_ref[0] = xn * lax.rsqrt(ms + NORM_EPS) * fnw_ref[...]


def _mixer_output(y_f, y_b, cols, yh, x, gate, m_norm_w, final_norm_w, wm, wh, wo):
    b, l, d = x.shape
    tm = min(512, l)
    tok = lambda blk: pl.BlockSpec((1, tm, d), lambda bi, i: (bi, i, blk))
    wspec = pl.BlockSpec((d, d), lambda bi, i: (0, 0))
    rowspec = pl.BlockSpec((1, d), lambda bi, i: (0, 0))
    return pl.pallas_call(
        _output_kernel,
        grid=(b, l // tm),
        in_specs=[tok(0), tok(0),
                  tok(COL_Z), tok(COL_HGATE), tok(COL_GM), tok(COL_GH), tok(0), tok(0),
                  pl.BlockSpec((1, 1, d), lambda bi, i: (bi, 0, 0)),
                  rowspec, rowspec, wspec, wspec, wspec],
        out_specs=tok(0),
        out_shape=jax.ShapeDtypeStruct((b, l, d), F32),
        compiler_params=_params("parallel", "parallel"),
        name="mixer_output",
    )(y_f, y_b, cols, cols, cols, cols, yh, x, gate, m_norm_w.reshape(1, d),
      final_norm_w.reshape(1, d), wm, wh, wo)


def _layer(x, ctx, c, c_ctx, ada_w, ada_b, norm_w, w_in, m_conv_w, m_conv_b, m_dt_bias, m_a_log,
           m_d, m_norm_w, h_conv_w, h_conv_b, h_w1, h_b1, h_w2, h_b2, h_w3, h_b3, h_freq,
           h_w_out, h_bias, w_branch_m, w_branch_h, w_out, final_norm_w):
    b, l, d = x.shape
    lc = ctx.shape[1]
    ndt = 2 * M_HEADS

    pad_rows = -(b + 1) % SUBLANES
    c_rows = jnp.concatenate([c, c_ctx[None], jnp.zeros((pad_rows, d), F32)], axis=0)
    mod = _adaln(c_rows, ada_w, ada_b)
    shift, scale, gate = (mod[:b, i * d:(i + 1) * d].reshape(b, 1, d) for i in range(3))
    shift_c = jnp.broadcast_to(mod[b, :d], (b, 1, d))
    scale_c = jnp.broadcast_to(mod[b, d:2 * d], (b, 1, d))

    o_z, o_xbc, o_dt = 0, d, d + M_XBC
    o_hg = o_dt + ndt
    o_hp = o_hg + H_WIDTH
    o_gm = o_hp + 3 * H_WIDTH
    w_xbc = w_in[:, o_xbc:o_dt].astype(BF16)
    w_conv = jnp.concatenate([w_xbc, w_in[:, o_hp:o_gm].astype(BF16)], axis=1)
    w_plain = jnp.concatenate([w_in[:, o_z:o_xbc], w_in[:, o_hg:o_hp], w_in[:, o_gm:]],
                              axis=1).astype(BF16)
    w_dt_pad = jnp.zeros((d, DT_PAD), F32).at[:, :ndt].set(w_in[:, o_dt:o_hg])
    nw = norm_w.reshape(1, d)
    conv_w = jnp.concatenate([m_conv_w, h_conv_w], axis=1)
    conv_b = jnp.concatenate([m_conv_b, h_conv_b]).reshape(1, -1)
    n_silu = M_XBC // CONV_TN

    bias_r = jnp.zeros((1, DT_PAD), F32).at[0, :ndt].set(m_dt_bias.reshape(ndt))
    cols, cols_p, dt_l = _inproj(x, shift, scale, nw, w_conv, conv_w, conv_b, n_silu, w_plain,
                                 w_dt_pad, bias_r, min(1024, l))
    cols_c, _, dt_c = _inproj(ctx, shift_c, scale_c, nw, w_xbc, conv_w[:, :M_XBC],
                              conv_b[:, :M_XBC], n_silu, None, w_dt_pad, bias_r, min(1024, lc))

    a_r = jnp.zeros((1, DT_PAD), F32).at[0, :ndt].set(m_a_log.reshape(ndt))
    dskip_row = jnp.repeat(m_d, M_HEADDIM).reshape(1, d)
    y_f, y_b = _ssd(cols, cols_c, dt_l, dt_c, a_r, dskip_row)

    assert b % 2 == 0, "batch rows are packed in pairs as complex signals"
    half = FFT_RADIX // 2
    m = l // half
    pairs = b // 2
    u3 = cols.reshape(pairs, 2, half, m, cols.shape[2])
    filt = _hyena_filters(l, h_w1, h_b1, h_w2, h_b2, h_w3, h_b3, h_freq, h_w_out)
    efwd, einv = _dft_tables(2 * l)
    sah = _radix_fwd(filt.reshape(1, 1, half, m, filt.shape[1]), 0, filt.shape[1], True,
                     "filter_radix")
    kr, ki = _filter_spectrum(efwd, sah.reshape(1, FFT_RADIX, 2 * m, filt.shape[1]))
    z, z_block = u3, COL_HPROJ
    for order in range(2):
        sa = _radix_fwd(z, z_block, H_WIDTH, False, "hyena_radix_fwd")
        sb = _spectral(efwd, einv, sa.reshape(pairs, FFT_RADIX, 2 * m, H_WIDTH), kr, ki, order)
        z = _radix_inv(sb.reshape(pairs, FFT_RADIX, 2, m, H_WIDTH), u3, COL_HPROJ + 1 + order,
                       z, z_block, h_bias[order].reshape(1, H_WIDTH), "hyena_radix_inv")
        z_block = 0
    y_h = z.reshape(b, l, H_WIDTH)

    return _mixer_output(y_f, y_b, cols_p, y_h, x, gate, m_norm_w, final_norm_w,
                         w_branch_m.astype(BF16), w_branch_h.astype(BF16), w_out.astype(BF16))


def kernel(x, c, ctx, c_ctx, ada_w, ada_b, norm_w, w_in, m_conv_w, m_conv_b, m_dt_bias, m_a_log,
           m_d, m_norm_w, h_conv_w, h_conv_b, h_w1, h_b1, h_w2, h_b2, h_w3, h_b3, h_freq,
           h_w_out, h_bias, w_branch_m, w_branch_h, w_out, final_norm_w):
    assert ada_w.shape[0] == 1, "one trunk layer"
    return _layer(x, ctx, c, c_ctx, ada_w[0], ada_b[0], norm_w[0], w_in[0], m_conv_w[0],
                  m_conv_b[0], m_dt_bias[0], m_a_log[0], m_d[0], m_norm_w[0], h_conv_w[0],
                  h_conv_b[0], h_w1[0], h_b1[0], h_w2[0], h_b2[0], h_w3[0], h_b3[0], h_freq[0],
                  h_w_out[0], h_bias[0], w_branch_m[0], w_branch_h[0], w_out[0], final_norm_w)
```
